```python
import math
import jax
import jax.numpy as jnp
from jax import lax
import numpy as np

D_MODEL = 2048
BATCH = 4
SEQ = 2048
DEPTH = 2
DEC_BATCH = 128
DEC_SEQ = 8
PAST_LEN = 2048
PAGE_SIZE = 128

LRU_WIDTH = D_MODEL // 2
LRU_BLOCKS = 16
LRU_BLOCK = LRU_WIDTH // LRU_BLOCKS
LRU_CONV = 4
LRU_C = 8.0
MOBA_HEADS = 16
MOBA_HEAD_DIM = (D_MODEL // 2) // MOBA_HEADS
MOBA_BLOCK = 256
MOBA_TOPK = 3
MOBA_Q_CHUNK = 64
DSA_HEADS = 16
DSA_HEAD_DIM = D_MODEL // DSA_HEADS
DSA_KV_HEADS = 4
IDX_HEADS = 8
IDX_DIM = 64
DSA_TOPK = 256
DSA_Q_CHUNK = 128
D_FF = 5632
FFN_CONV = 3
N_BUCKETS = 32
MAX_DISTANCE = 128
EPS = 1e-6
MOBA_WIDTH = MOBA_HEADS * MOBA_HEAD_DIM
EVEN_IN = 2 * LRU_WIDTH + 3 * MOBA_WIDTH
ODD_Q = DSA_HEADS * DSA_HEAD_DIM
ODD_KV = DSA_KV_HEADS * DSA_HEAD_DIM
ODD_IN = ODD_Q + 2 * ODD_KV + IDX_HEADS * IDX_DIM + IDX_DIM + IDX_HEADS

kernel_name = 'hybrid_rglru_moba_dsa_step'

F32 = jnp.float32
NEG_INF = -jnp.inf


def rms_norm(x, g):
    xf = x.astype(F32)
    y = xf * lax.rsqrt(jnp.mean(xf * xf, axis=-1, keepdims=True) + EPS)
    return (y * g.astype(F32)).astype(x.dtype)


def t5_bucket(dist):
    n = jnp.maximum(dist, 0)
    exact = N_BUCKETS // 2
    nf = jnp.maximum(n, 1).astype(F32)
    large = exact + (jnp.log(nf / exact) / math.log(MAX_DISTANCE / exact)
                     * (N_BUCKETS - exact)).astype(jnp.int32)
    return jnp.where(n < exact, n, jnp.minimum(large, N_BUCKETS - 1))


def causal_dwconv(u, buf, w, b):
    width, t = w.shape[0], u.shape[1]
    ext = jnp.concatenate([buf.astype(u.dtype), u], axis=1)
    out = b + sum(ext[:, j:j + t] * w[j] for j in range(width))
    return out.astype(u.dtype), ext[:, t:]


def rg_lru(xc, h0, w_a, b_a, w_i, b_i, lam):
    bsz, t, width = xc.shape
    xf = xc.astype(F32)
    xblk = xf.reshape(bsz, t, LRU_BLOCKS, LRU_BLOCK)
    r = jax.nn.sigmoid(jnp.einsum('btnc,ncd->btnd', xblk, w_a).reshape(bsz, t, width) + b_a)
    i = jax.nn.sigmoid(jnp.einsum('btnc,ncd->btnd', xblk, w_i).reshape(bsz, t, width) + b_i)
    log_a = -LRU_C * r * jax.nn.softplus(-lam.astype(F32))
    a = jnp.exp(log_a)
    u = jnp.sqrt(-jnp.expm1(2.0 * log_a)) * (i * xf)
    u = u.at[:, 0].add(a[:, 0] * h0.astype(F32))
    _, h = lax.associative_scan(lambda c1, c2: (c1[0] * c2[0], c2[0] * c1[1] + c2[1]), (a, u), axis=1)
    return h.astype(xc.dtype), h[:, -1].astype(xc.dtype)


def gqa_logits(q, kg, dist, table):
    hq, dh = q.shape[-2:]
    hkv = kg.shape[-2]
    qg = q.reshape(q.shape[:-2] + (hkv, hq // hkv, dh))
    s = jnp.einsum('...kgd,...nkd->...kgn', qg, kg).reshape(q.shape[:-2] + (hq, kg.shape[-3]))
    s = s * dh ** -0.5 + jnp.swapaxes(table[t5_bucket(dist)], -1, -2)
    return jnp.where((dist >= 0)[..., None, :], s, NEG_INF)


def gqa_combine(p, vg):
    hq, n = p.shape[-2:]
    hkv, dh = vg.shape[-2:]
    pg = p.reshape(p.shape[:-2] + (hkv, hq // hkv, n))
    return jnp.einsum('...kgn,...nkd->...kgd', pg, vg).reshape(p.shape[:-2] + (hq, dh))


def index_scores(qi, ki, wi):
    s = jax.nn.relu(jnp.einsum('bthd,bsd->bths', qi.astype(F32), ki.astype(F32)) * IDX_DIM ** -0.5)
    return jnp.einsum('bth,bths->bts', wi.astype(F32) * IDX_HEADS ** -0.5, s)


def moba_prompt(q, k, v, table):
    b, s, h, dh = q.shape
    nb = -(-s // MOBA_BLOCK)
    pad = nb * MOBA_BLOCK - s
    kb = jnp.pad(k, ((0, 0), (0, pad), (0, 0), (0, 0))).reshape(b, nb, MOBA_BLOCK, h, dh).transpose(0, 3, 1, 2, 4)
    vb = jnp.pad(v, ((0, 0), (0, pad), (0, 0), (0, 0))).reshape(b, nb, MOBA_BLOCK, h, dh).transpose(0, 3, 1, 2, 4)
    kmean = jnp.mean(kb.astype(F32), axis=3)
    n_slot = max(nb, MOBA_TOPK)
    scale = dh ** -0.5
    nsel = MOBA_TOPK * MOBA_BLOCK
    bi = jnp.arange(b)[:, None, None, None]
    hi = jnp.arange(h)[None, None, :, None]
    blk_off = jnp.arange(MOBA_BLOCK)

    def chunk(c):
        t0 = c * MOBA_Q_CHUNK
        qc = lax.dynamic_slice_in_dim(q, t0, MOBA_Q_CHUNK, axis=1).astype(F32)
        tpos = t0 + jnp.arange(MOBA_Q_CHUNK)
        cur = t0 // MOBA_BLOCK
        gate = jnp.einsum('bqhd,bhnd->bqhn', qc, kmean)
        gate = jnp.where(jnp.arange(nb) < cur, gate, NEG_INF)
        gate = jnp.pad(gate, ((0, 0), (0, 0), (0, 0), (0, n_slot - nb)), constant_values=NEG_INF)
        gval, gidx = lax.top_k(gate, MOBA_TOPK)
        gidx = jnp.minimum(gidx, nb - 1)
        ksel = kb[bi, hi, gidx]
        vsel = vb[bi, hi, gidx]
        kpos = gidx[..., None] * MOBA_BLOCK + blk_off
        bias = table[t5_bucket(tpos[None, :, None, None, None] - kpos), hi[..., None]]
        s_sel = jnp.einsum('bqhd,bqhnkd->bqhnk', qc, ksel) * scale + bias
        s_sel = jnp.where(jnp.isfinite(gval)[..., None], s_sel, NEG_INF).reshape(b, MOBA_Q_CHUNK, h, nsel)
        kown = lax.dynamic_index_in_dim(kb, cur, axis=2, keepdims=False)
        vown = lax.dynamic_index_in_dim(vb, cur, axis=2, keepdims=False)
        odist = tpos[:, None] - (cur * MOBA_BLOCK + blk_off)[None, :]
        s_own = jnp.einsum('bqhd,bhkd->bqhk', qc, kown) * scale + table[t5_bucket(odist)].transpose(0, 2, 1)
        s_own = jnp.where((odist >= 0)[:, None, :], s_own, NEG_INF)
        p = jax.nn.softmax(jnp.concatenate([s_sel, s_own], axis=-1), axis=-1)
        out = (jnp.einsum('bqhnk,bqhnkd->bqhd', p[..., :nsel].reshape(b, MOBA_Q_CHUNK, h, MOBA_TOPK, MOBA_BLOCK), vsel)
               + jnp.einsum('bqhk,bhkd->bqhd', p[..., nsel:], vown))
        return out.astype(q.dtype)

    out = lax.map(chunk, jnp.arange(s // MOBA_Q_CHUNK))
    return jnp.swapaxes(out, 0, 1).reshape(b, s, h, dh)


def moba_sample(q, k, v, pool_k, pool_v, page_table, table):
    db, t, h, dh = q.shape
    n_pages = page_table.shape[1]
    past = n_pages * PAGE_SIZE
    ppb = MOBA_BLOCK // PAGE_SIZE
    nbp = past // MOBA_BLOCK
    own_start = nbp * MOBA_BLOCK
    n_own = n_pages - nbp * ppb
    n_slot = max(nbp, MOBA_TOPK)
    scale = dh ** -0.5
    nsel = MOBA_TOPK * MOBA_BLOCK

    def block_mean(j):
        pages = lax.dynamic_slice_in_dim(page_table, j * ppb, ppb, axis=1)
        return jnp.mean(pool_k[pages].astype(F32), axis=(1, 2))

    if nbp > 0:
        kmean = lax.map(block_mean, jnp.arange(nbp))
    else:
        kmean = jnp.zeros((0, db, h, dh), F32)
    if n_own > 0:
        own_pages = page_table[:, nbp * ppb:]
        k_loc = jnp.concatenate([pool_k[own_pages].reshape(db, n_own * PAGE_SIZE, h, dh).astype(k.dtype), k], axis=1)
        v_loc = jnp.concatenate([pool_v[own_pages].reshape(db, n_own * PAGE_SIZE, h, dh).astype(v.dtype), v], axis=1)
    else:
        k_loc, v_loc = k, v
    loc_pos = own_start + jnp.arange(k_loc.shape[1])
    bi = jnp.arange(db)[:, None, None, None]
    hi = jnp.arange(h)[None, :, None, None]
    page_off = jnp.arange(ppb)
    row_off = jnp.arange(PAGE_SIZE)

    def one_pos(args):
        qj, j = args
        qj = qj.astype(F32)
        tpos = past + j
        gate = jnp.einsum('bhd,nbhd->bhn', qj, kmean)
        gate = jnp.pad(gate, ((0, 0), (0, 0), (0, n_slot - nbp)), constant_values=NEG_INF)
        gval, gidx = lax.top_k(gate, MOBA_TOPK)
        gidx = jnp.minimum(gidx, max(nbp - 1, 0))
        lpage = jnp.minimum(gidx[..., None] * ppb + page_off, n_pages - 1)
        ppage = page_table[bi, lpage]
        ksel = pool_k[ppage, :, hi, :]
        vsel = pool_v[ppage, :, hi, :]
        kpos = lpage[..., None] * PAGE_SIZE + row_off
        bias = table[t5_bucket(tpos - kpos), hi[..., None]]
        s_sel = jnp.einsum('bhd,bhnpkd->bhnpk', qj, ksel) * scale + bias
        s_sel = jnp.where(jnp.isfinite(gval)[..., None, None], s_sel, NEG_INF).reshape(db, h, nsel)
        ldist = tpos - loc_pos
        s_loc = jnp.einsum('bhd,bkhd->bhk', qj, k_loc) * scale + table[t5_bucket(ldist)].T
        s_loc = jnp.where(ldist >= 0, s_loc, NEG_INF)
        p = jax.nn.softmax(jnp.concatenate([s_sel, s_loc], axis=-1), axis=-1)
        out = (jnp.einsum('bhnpk,bhnpkd->bhd', p[..., :nsel].reshape(db, h, MOBA_TOPK, ppb, PAGE_SIZE), vsel)
               + jnp.einsum('bhk,bkhd->bhd', p[..., nsel:], v_loc))
        return out.astype(q.dtype)

    out = lax.map(one_pos, (jnp.swapaxes(q, 0, 1), jnp.arange(t)))
    return jnp.swapaxes(out, 0, 1)


def dsa_prompt(q, k, v, qi, ki, wi, table):
    b, s, hq, dh = q.shape
    topk = min(DSA_TOPK, s // 4)
    spos = jnp.arange(s)
    bi = jnp.arange(b)[:, None, None]

    def chunk(c):
        t0 = c * DSA_Q_CHUNK
        tpos = t0 + jnp.arange(DSA_Q_CHUNK)
        qc = lax.dynamic_slice_in_dim(q, t0, DSA_Q_CHUNK, axis=1).astype(F32)
        isc = index_scores(lax.dynamic_slice_in_dim(qi, t0, DSA_Q_CHUNK, axis=1), ki,
                           lax.dynamic_slice_in_dim(wi, t0, DSA_Q_CHUNK, axis=1))
        isc = jnp.where(spos <= tpos[:, None], isc, NEG_INF)
        _, sidx = lax.top_k(isc, topk)
        p = jax.nn.softmax(gqa_logits(qc, k[bi, sidx], tpos[:, None] - sidx, table), axis=-1)
        return gqa_combine(p, v[bi, sidx]).astype(q.dtype)

    out = lax.map(chunk, jnp.arange(s // DSA_Q_CHUNK))
    return jnp.swapaxes(out, 0, 1).reshape(b, s, hq, dh)


def dsa_sample(q, k, v, qi, ki, wi, pool_k, pool_v, pool_ik, page_table, table):
    db, t, hq, dh = q.shape
    n_pages = page_table.shape[1]
    past = n_pages * PAGE_SIZE
    total = past + t
    topk = min(DSA_TOPK, total // 4)
    ki_all = jnp.concatenate([pool_ik[page_table].reshape(db, past, IDX_DIM).astype(ki.dtype), ki], axis=1)
    tpos = past + jnp.arange(t)
    isc = index_scores(qi, ki_all, wi)
    isc = jnp.where(jnp.arange(total) <= tpos[:, None], isc, NEG_INF)
    _, sidx = lax.top_k(isc, topk)
    bi = jnp.arange(db)[:, None]

    def one_pos(args):
        qj, sj, tj = args
        qj = qj.astype(F32)
        sp = jnp.minimum(sj, past - 1)
        phys = page_table[bi, sp // PAGE_SIZE]
        row = sp % PAGE_SIZE
        s_past = gqa_logits(qj, pool_k[phys, row], tj - sj, table)
        s_past = jnp.where((sj < past)[:, None, :], s_past, NEG_INF)
        sel_new = jnp.any(sj[:, :, None] == tpos, axis=1)
        s_new = gqa_logits(qj, k, jnp.broadcast_to(tj - tpos, (db, t)), table)
        s_new = jnp.where(sel_new[:, None, :], s_new, NEG_INF)
        p = jax.nn.softmax(jnp.concatenate([s_past, s_new], axis=-1), axis=-1)
        out = gqa_combine(p[..., :topk], pool_v[phys, row]) + gqa_combine(p[..., topk:], v)
        return out.astype(q.dtype)

    out = lax.map(one_pos, (jnp.swapaxes(q, 0, 1), jnp.swapaxes(sidx, 0, 1), tpos))
    return jnp.swapaxes(out, 0, 1)


def even_mixer(x, h0, conv_buf, attend, norm_g, w_in, conv_w, conv_b, gate_a_w, gate_a_b,
               gate_i_w, gate_i_b, lam, q_norm_g, k_norm_g, w_out):
    b, t, _ = x.shape
    z = rms_norm(x, norm_g) @ w_in
    xa, ga, q, k, v = jnp.split(z, [LRU_WIDTH, 2 * LRU_WIDTH, 2 * LRU_WIDTH + MOBA_WIDTH,
                                    2 * LRU_WIDTH + 2 * MOBA_WIDTH], axis=-1)
    xc, new_conv = causal_dwconv(xa, conv_buf, conv_w, conv_b)
    h, h_last = rg_lru(xc, h0, gate_a_w, gate_a_b, gate_i_w, gate_i_b, lam)
    ya = h * jax.nn.gelu(ga)
    q = rms_norm(q.reshape(b, t, MOBA_HEADS, MOBA_HEAD_DIM), q_norm_g)
    k = rms_norm(k.reshape(b, t, MOBA_HEADS, MOBA_HEAD_DIM), k_norm_g)
    v = v.reshape(b, t, MOBA_HEADS, MOBA_HEAD_DIM)
    yb = attend(q, k, v).reshape(b, t, MOBA_WIDTH)
    out = jnp.concatenate([ya, yb], axis=-1) @ w_out
    return x + out, h_last, new_conv, k, v


def odd_mixer(x, attend, norm_g, w_in, q_norm_g, k_norm_g, w_out):
    b, t, _ = x.shape
    z = rms_norm(x, norm_g) @ w_in
    o1 = ODD_Q
    o2 = o1 + ODD_KV
    o3 = o2 + ODD_KV
    o4 = o3 + IDX_HEADS * IDX_DIM
    o5 = o4 + IDX_DIM
    q, k, v, qi, ki, wi = jnp.split(z, [o1, o2, o3, o4, o5], axis=-1)
    q = rms_norm(q.reshape(b, t, DSA_HEADS, DSA_HEAD_DIM), q_norm_g)
    k = rms_norm(k.reshape(b, t, DSA_KV_HEADS, DSA_HEAD_DIM), k_norm_g)
    v = v.reshape(b, t, DSA_KV_HEADS, DSA_HEAD_DIM)
    qi = qi.reshape(b, t, IDX_HEADS, IDX_DIM)
    y = attend(q, k, v, qi, ki, wi)
    return x + y.reshape(b, t, ODD_Q) @ w_out, k, v, ki


def conv_ffn(x, buf, norm_g, w_up, conv_w, conv_b, w_down):
    u = rms_norm(x, norm_g) @ w_up
    uc, new_buf = causal_dwconv(u, buf, conv_w, conv_b)
    val, gate = jnp.split(uc, 2, axis=-1)
    return x + (jax.nn.gelu(gate) * val) @ w_down, new_buf


def setup_inputs(seed: int = 0) -> dict:
    key = jax.random.key(seed)
    ks = iter(jax.random.split(key, 64))

    def nrm(shape, scale=1.0):
        return jax.random.normal(next(ks), shape, F32) * scale

    def gain(n):
        return 1.0 + nrm((n,), 0.02)

    n_pages = PAST_LEN // PAGE_SIZE
    n_used = DEC_BATCH * n_pages
    n_phys = n_used + max(1, n_used // 4)
    page_table = jax.random.permutation(next(ks), n_phys)[:n_used].reshape(DEC_BATCH, n_pages).astype(jnp.int32)
    u = jax.random.uniform(next(ks), (LRU_WIDTH,), F32, 0.9, 0.999)
    sig = u ** (1.0 / LRU_C)
    lam = jnp.log(sig) - jnp.log1p(-sig)
    return {
        'x_prompt': nrm((BATCH, SEQ, D_MODEL)),
        'x_sample': nrm((DEC_BATCH, DEC_SEQ, D_MODEL)),
        'state_rglru_h': nrm((DEC_BATCH, LRU_WIDTH), 0.5),
        'state_rglru_conv': nrm((DEC_BATCH, LRU_CONV - 1, LRU_WIDTH)),
        'cache_moba_k': nrm((n_phys, PAGE_SIZE, MOBA_HEADS, MOBA_HEAD_DIM)),
        'cache_moba_v': nrm((n_phys, PAGE_SIZE, MOBA_HEADS, MOBA_HEAD_DIM)),
        'state_ffn0_conv': nrm((DEC_BATCH, FFN_CONV - 1, 2 * D_FF)),
        'cache_dsa_k': nrm((n_phys, PAGE_SIZE, DSA_KV_HEADS, DSA_HEAD_DIM)),
        'cache_dsa_v': nrm((n_phys, PAGE_SIZE, DSA_KV_HEADS, DSA_HEAD_DIM)),
        'cache_dsa_idx_k': nrm((n_phys, PAGE_SIZE, IDX_DIM)),
        'state_ffn1_conv': nrm((DEC_BATCH, FFN_CONV - 1, 2 * D_FF)),
        'page_table': page_table,
        'rel_bias_table': nrm((N_BUCKETS, DSA_HEADS), 0.5),
        'l0_norm_g': gain(D_MODEL),
        'l0_w_in': nrm((D_MODEL, EVEN_IN), D_MODEL ** -0.5),
        'l0_conv_w': nrm((LRU_CONV, LRU_WIDTH), LRU_CONV ** -0.5),
        'l0_conv_b': nrm((LRU_WIDTH,), 0.02),
        'l0_gate_a_w': nrm((LRU_BLOCKS, LRU_BLOCK, LRU_BLOCK), LRU_BLOCK ** -0.5),
        'l0_gate_a_b': nrm((LRU_WIDTH,), 0.02),
        'l0_gate_i_w': nrm((LRU_BLOCKS, LRU_BLOCK, LRU_BLOCK), LRU_BLOCK ** -0.5),
        'l0_gate_i_b': nrm((LRU_WIDTH,), 0.02),
        'l0_lambda': lam,
        'l0_q_norm_g': gain(MOBA_HEAD_DIM),
        'l0_k_norm_g': gain(MOBA_HEAD_DIM),
        'l0_w_out': nrm((LRU_WIDTH + MOBA_WIDTH, D_MODEL), (LRU_WIDTH + MOBA_WIDTH) ** -0.5),
        'ffn0_norm_g': gain(D_MODEL),
        'ffn0_w_up': nrm((D_MODEL, 2 * D_FF), D_MODEL ** -0.5),
        'ffn0_conv_w': nrm((FFN_CONV, 2 * D_FF), FFN_CONV ** -0.5),
        'ffn0_conv_b': nrm((2 * D_FF,), 0.02),
        'ffn0_w_down': nrm((D_FF, D_MODEL), D_FF ** -0.5),
        'l1_norm_g': gain(D_MODEL),
        'l1_w_in': nrm((D_MODEL, ODD_IN), D_MODEL ** -0.5),
        'l1_q_norm_g': gain(DSA_HEAD_DIM),
        'l1_k_norm_g': gain(DSA_HEAD_DIM),
        'l1_w_out': nrm((ODD_Q, D_MODEL), ODD_Q ** -0.5),
        'ffn1_norm_g': gain(D_MODEL),
        'ffn1_w_up': nrm((D_MODEL, 2 * D_FF), D_MODEL ** -0.5),
        'ffn1_conv_w': nrm((FFN_CONV, 2 * D_FF), FFN_CONV ** -0.5),
        'ffn1_conv_b': nrm((2 * D_FF,), 0.02),
        'ffn1_w_down': nrm((D_FF, D_MODEL), D_FF ** -0.5),
    }


def reference(x_prompt, x_sample, state_rglru_h, state_rglru_conv, cache_moba_k, cache_moba_v,
              state_ffn0_conv, cache_dsa_k, cache_dsa_v, cache_dsa_idx_k, state_ffn1_conv, page_table,
              rel_bias_table, l0_norm_g, l0_w_in, l0_conv_w, l0_conv_b, l0_gate_a_w, l0_gate_a_b,
              l0_gate_i_w, l0_gate_i_b, l0_lambda, l0_q_norm_g, l0_k_norm_g, l0_w_out,
              ffn0_norm_g, ffn0_w_up, ffn0_conv_w, ffn0_conv_b, ffn0_w_down,
              l1_norm_g, l1_w_in, l1_q_norm_g, l1_k_norm_g, l1_w_out,
              ffn1_norm_g, ffn1_w_up, ffn1_conv_w, ffn1_conv_b, ffn1_w_down):
    xp, xs = x_prompt, x_sample
    bp = xp.shape[0]
    ffn_params = ((ffn0_norm_g, ffn0_w_up, ffn0_conv_w, ffn0_conv_b, ffn0_w_down),
                  (ffn1_norm_g, ffn1_w_up, ffn1_conv_w, ffn1_conv_b, ffn1_w_down))
    ffn_states = (state_ffn0_conv, state_ffn1_conv)
    ffn_new_p, ffn_new_s = [], []
    for layer in range(DEPTH):
        if layer % 2 == 0:
            even_params = (l0_norm_g, l0_w_in, l0_conv_w, l0_conv_b, l0_gate_a_w, l0_gate_a_b,
                           l0_gate_i_w, l0_gate_i_b, l0_lambda, l0_q_norm_g, l0_k_norm_g, l0_w_out)
            xp, h_p, conv_p, mk_p, mv_p = even_mixer(
                xp, jnp.zeros((bp, LRU_WIDTH), xp.dtype), jnp.zeros((bp, LRU_CONV - 1, LRU_WIDTH), xp.dtype),
                lambda q, k, v: moba_prompt(q, k, v, rel_bias_table), *even_params)
            xs, h_s, conv_s, mk_s, mv_s = even_mixer(
                xs, state_rglru_h, state_rglru_conv,
                lambda q, k, v: moba_sample(q, k, v, cache_moba_k, cache_moba_v, page_table, rel_bias_table),
                *even_params)
        else:
            odd_params = (l1_norm_g, l1_w_in, l1_q_norm_g, l1_k_norm_g, l1_w_out)
            xp, dk_p, dv_p, di_p = odd_mixer(
                xp, lambda q, k, v, qi, ki, wi: dsa_prompt(q, k, v, qi, ki, wi, rel_bias_table), *odd_params)
            xs, dk_s, dv_s, di_s = odd_mixer(
                xs, lambda q, k, v, qi, ki, wi: dsa_sample(q, k, v, qi, ki, wi, cache_dsa_k, cache_dsa_v,
                                                         cache_dsa_idx_k, page_table, rel_bias_table),
                *odd_params)
        xp, f_p = conv_ffn(xp, jnp.zeros((bp, FFN_CONV - 1, 2 * D_FF), xp.dtype), *ffn_params[layer])
        xs, f_s = conv_ffn(xs, ffn_states[layer], *ffn_params[layer])
        ffn_new_p.append(f_p)
        ffn_new_s.append(f_s)
    return (xp, xs, h_p, h_s, conv_p, conv_s, mk_p, mk_s, mv_p, mv_s,
            ffn_new_p[0], ffn_new_s[0], dk_p, dk_s, dv_p, dv_s, di_p, di_s,
            ffn_new_p[1], ffn_new_s[1])
```

```python
import functools
import math

import numpy as np
import jax
import jax.numpy as jnp
from jax import lax
from jax.experimental import pallas as pl
from jax.experimental.pallas import tpu as pltpu

F32 = jnp.float32
BF16 = jnp.bfloat16
I32 = jnp.int32
NEG_INF = float("-inf")
INT_MIN = -(2 ** 31)

EPS = 1e-6
LRU_C = 8.0
LRU_GROUP = 256
MOBA_BLOCK = 256
MOBA_TOPK = 3
MOBA_HEAD_DIM = 64
DSA_HEAD_DIM = 128
DSA_TOPK = 256
IDX_DIM = 64
IDX_HEADS = 8
PAGE_SIZE = 128
N_BUCKETS = 32
MAX_DISTANCE = 128
Q_TILE = 256
LANES = 128
MIB = 1024 * 1024


def _t5_thresholds():
    exact = N_BUCKETS // 2
    n = np.arange(0, 4 * MAX_DISTANCE)
    nf = np.maximum(n, 1).astype(np.float32)
    large = exact + (np.log(nf / np.float32(exact)) / np.float32(math.log(MAX_DISTANCE / exact))
                     * np.float32(N_BUCKETS - exact)).astype(np.int32)
    bucket = np.where(n < exact, n, np.minimum(large, N_BUCKETS - 1))
    assert np.all(np.diff(bucket) >= 0) and bucket[-1] == N_BUCKETS - 1
    return [int(np.argmax(bucket >= b)) for b in range(N_BUCKETS)]


T5_THR = _t5_thresholds()


def _cp(sem, vmem_mib=48):
    return pltpu.CompilerParams(dimension_semantics=sem, vmem_limit_bytes=vmem_mib * MIB)


def _iota(shape, dim):
    return lax.broadcasted_iota(I32, shape, dim)


def _dot_nt(a, b):
    return lax.dot_general(a, b, (((1,), (1,)), ((), ())), preferred_element_type=F32)


def _dot_nn(a, b):
    return jnp.dot(a, b, preferred_element_type=F32)


def _split2(x):
    hi = x.astype(BF16)
    lo = (x - hi.astype(F32)).astype(BF16)
    return hi, lo


def _split3(x):
    hi = x.astype(BF16)
    r = x - hi.astype(F32)
    mid = r.astype(BF16)
    lo = (r - mid.astype(F32)).astype(BF16)
    return hi, mid, lo


def _dot3(dot, a, b):
    ah, al = _split2(a)
    bh, bl = _split2(b)
    return dot(ah, bh) + (dot(ah, bl) + dot(al, bh))


def _dot_exact01(dot, a, b01):
    h, m, l = _split3(a)
    return dot(h, b01) + (dot(m, b01) + dot(l, b01))


def _t5_bias(dist, tab_ref, h):
    acc = jnp.full(dist.shape, tab_ref[0, h], F32)
    for b in range(1, N_BUCKETS):
        acc = jnp.where(dist >= T5_THR[b], tab_ref[b, h], acc)
    return acc


def _softplus(x):
    return jnp.maximum(x, 0.0) + jnp.log1p(jnp.exp(-jnp.abs(x)))


def _norm_mm_kernel(x_ref, g_ref, w_ref, o_ref, xn_ref):
    @pl.when(pl.program_id(1) == 0)
    def _():
        x = x_ref[...]
        ms = jnp.mean(x * x, axis=-1, keepdims=True)
        xn_ref[...] = (x * lax.rsqrt(ms + EPS) * g_ref[...]).astype(BF16)

    o_ref[...] = _dot_nn(xn_ref[...], w_ref[...].astype(BF16))


def norm_matmul(x, g, w, tn=512):
    m, d = x.shape
    n = w.shape[1]
    tm = min(m, 1024)
    assert m % tm == 0 and n % tn == 0
    return pl.pallas_call(
        _norm_mm_kernel,
        grid=(m // tm, n // tn),
        in_specs=[pl.BlockSpec((tm, d), lambda i, j: (i, 0)),
                  pl.BlockSpec((1, d), lambda i, j: (0, 0)),
                  pl.BlockSpec((d, tn), lambda i, j: (0, j))],
        out_specs=pl.BlockSpec((tm, tn), lambda i, j: (i, j)),
        out_shape=jax.ShapeDtypeStruct((m, n), F32),
        scratch_shapes=[pltpu.VMEM((tm, d), BF16)],
        compiler_params=_cp(("parallel", "arbitrary")),
        name="norm_matmul",
    )(x, g.reshape(1, d), w)


def _mm_res_kernel(a_ref, w_ref, r_ref, o_ref, acc_ref, *, nk):
    k = pl.program_id(2)

    @pl.when(k == 0)
    def _():
        acc_ref[...] = jnp.zeros_like(acc_ref)

    acc_ref[...] += _dot_nn(a_ref[...], w_ref[...].astype(BF16))

    @pl.when(k == nk - 1)
    def _():
        o_ref[...] = acc_ref[...] + r_ref[...]


def matmul_residual(a, w, res, tn=512):
    m, kdim = a.shape
    n = w.shape[1]
    tm = min(m, 1024)
    tk = kdim if kdim <= 2048 else kdim // 4
    assert m % tm == 0 and n % tn == 0 and kdim % tk == 0 and tk % LANES == 0
    nk = kdim // tk
    return pl.pallas_call(
        functools.partial(_mm_res_kernel, nk=nk),
        grid=(m // tm, n // tn, nk),
        in_specs=[pl.BlockSpec((tm, tk), lambda i, j, k: (i, k)),
                  pl.BlockSpec((tk, tn), lambda i, j, k: (k, j)),
                  pl.BlockSpec((tm, tn), lambda i, j, k: (i, j))],
        out_specs=pl.BlockSpec((tm, tn), lambda i, j, k: (i, j)),
        out_shape=jax.ShapeDtypeStruct((m, n), F32),
        scratch_shapes=[pltpu.VMEM((tm, tn), F32)],
        compiler_params=_cp(("parallel", "parallel", "arbitrary")),
        name="matmul_residual",
    )(a, w, res)


def _ffn_mid_kernel(uv_ref, ug_ref, pv_ref, pg_ref, sv_ref, sg_ref, wv_ref, wg_ref, bv_ref, bg_ref,
                    o_ref, ev_ref, eg_ref, *, tc):
    t = pl.program_id(1)

    def conv(u_ref, p_ref, s_ref, w_ref, b_ref, e_ref):
        @pl.when(t == 0)
        def _():
            e_ref[6:8, :] = s_ref[0]

        @pl.when(t > 0)
        def _():
            e_ref[6:8, :] = p_ref[0, 6:8, :]

        e_ref[8:8 + tc, :] = u_ref[0]
        w = w_ref[...]
        return (b_ref[...] + e_ref[6:6 + tc, :] * w[0:1] + e_ref[7:7 + tc, :] * w[1:2]
                + e_ref[8:8 + tc, :] * w[2:3])

    val = conv(uv_ref, pv_ref, sv_ref, wv_ref, bv_ref, ev_ref)
    gate = conv(ug_ref, pg_ref, sg_ref, wg_ref, bg_ref, eg_ref)
    o_ref[0] = (jax.nn.gelu(gate) * val).astype(o_ref.dtype)


def ffn_mid(u, state, conv_w, conv_b, cn=512):
    b, t, f2 = u.shape
    f = f2 // 2
    tc = min(t, 256)
    assert t % tc == 0 and tc % 8 == 0 and f % cn == 0 and conv_w.shape[0] == 3
    ncb = f // cn
    r8 = tc // 8

    def cur(off):
        return pl.BlockSpec((1, tc, cn), lambda i, j, c: (i, j, c + off))

    def prev(off):
        return pl.BlockSpec((1, 8, cn), lambda i, j, c: (i, jnp.maximum(j * r8 - 1, 0), c + off))

    def st(off):
        return pl.BlockSpec((1, 2, cn), lambda i, j, c: (i, 0, c + off))

    def wspec(rows, off):
        return pl.BlockSpec((rows, cn), lambda i, j, c: (0, c + off))

    return pl.pallas_call(
        functools.partial(_ffn_mid_kernel, tc=tc),
        grid=(b, t // tc, ncb),
        in_specs=[cur(0), cur(ncb), prev(0), prev(ncb), st(0), st(ncb),
                  wspec(3, 0), wspec(3, ncb), wspec(1, 0), wspec(1, ncb)],
        out_specs=pl.BlockSpec((1, tc, cn), lambda i, j, c: (i, j, c)),
        out_shape=jax.ShapeDtypeStruct((b, t, f), BF16),
        scratch_shapes=[pltpu.VMEM((tc + 8, cn), F32), pltpu.VMEM((tc + 8, cn), F32)],
        compiler_params=_cp(("parallel", "arbitrary", "arbitrary")),
        name="ffn_mid",
    )(u, u, u, u, state, state, conv_w, conv_w, conv_b.reshape(1, f2), conv_b.reshape(1, f2))


def _rglru_kernel(xa_ref, ga_ref, pv_ref, cs_ref, h0_ref, cw_ref, cb_ref, wa_ref, ba_ref,
                  wi_ref, bi_ref, lam_ref, ya_ref, hl_ref, e_ref, hc_ref, *, tc):
    t = pl.program_id(1)
    w = xa_ref.shape[-1]

    @pl.when(t == 0)
    def _():
        hc_ref[...] = h0_ref[0]
        e_ref[5:8, :] = cs_ref[0]

    @pl.when(t > 0)
    def _():
        e_ref[5:8, :] = pv_ref[0, 5:8, :]

    e_ref[8:8 + tc, :] = xa_ref[0]
    cw = cw_ref[...]
    xc = (cb_ref[...] + e_ref[5:5 + tc, :] * cw[0:1] + e_ref[6:6 + tc, :] * cw[1:2]
          + e_ref[7:7 + tc, :] * cw[2:3] + e_ref[8:8 + tc, :] * cw[3:4])
    xb = xc.astype(BF16)

    def gate(w_ref, b_ref):
        parts = [_dot_nn(xb[:, g * LRU_GROUP:(g + 1) * LRU_GROUP], w_ref[g])
                 for g in range(w // LRU_GROUP)]
        return jax.nn.sigmoid(jnp.concatenate(parts, axis=1) + b_ref[...])

    r = gate(wa_ref, ba_ref)
    i = gate(wi_ref, bi_ref)
    log_a = -LRU_C * r * _softplus(-lam_ref[...])
    a = jnp.exp(log_a)
    u = jnp.sqrt(-jnp.tanh(log_a) * (1.0 + a * a)) * (i * xc)

    row = _iota((tc, w), 0)
    s = 1
    while s < tc:
        a_sh = jnp.where(row >= s, pltpu.roll(a, s, 0), 1.0)
        u_sh = jnp.where(row >= s, pltpu.roll(u, s, 0), 0.0)
        u = u + a * u_sh
        a = a * a_sh
        s *= 2
    h = u + a * hc_ref[...]
    hlast = h[tc - 1:tc, :]
    hc_ref[...] = hlast
    hl_ref[0] = hlast
    ya_ref[0] = (h * jax.nn.gelu(ga_ref[0])).astype(ya_ref.dtype)


def rglru_branch(z, h0, conv_state, conv_w, conv_b, wa_bd, ba, wi_bd, bi, lam):
    b, t, _ = z.shape
    w = h0.shape[-1]
    tc = min(t, 256)
    assert t % tc == 0 and tc % 8 == 0 and conv_w.shape[0] == 4 and w % LRU_GROUP == 0
    r8 = tc // 8
    ng = w // LRU_GROUP
    vec = lambda: pl.BlockSpec((1, w), lambda i, j: (0, 0))
    ya, hl = pl.pallas_call(
        functools.partial(_rglru_kernel, tc=tc),
        grid=(b, t // tc),
        in_specs=[pl.BlockSpec((1, tc, w), lambda i, j: (i, j, 0)),
                  pl.BlockSpec((1, tc, w), lambda i, j: (i, j, 1)),
                  pl.BlockSpec((1, 8, w), lambda i, j: (i, jnp.maximum(j * r8 - 1, 0), 0)),
                  pl.BlockSpec((1, 3, w), lambda i, j: (i, 0, 0)),
                  pl.BlockSpec((1, 1, w), lambda i, j: (i, 0, 0)),
                  pl.BlockSpec((4, w), lambda i, j: (0, 0)),
                  vec(),
                  pl.BlockSpec((ng, LRU_GROUP, LRU_GROUP), lambda i, j: (0, 0, 0)),
                  vec(),
                  pl.BlockSpec((ng, LRU_GROUP, LRU_GROUP), lambda i, j: (0, 0, 0)),
                  vec(), vec()],
        out_specs=[pl.BlockSpec((1, tc, w), lambda i, j: (i, j, 0)),
                   pl.BlockSpec((1, 1, w), lambda i, j: (i, 0, 0))],
        out_shape=[jax.ShapeDtypeStruct((b, t, w), BF16), jax.ShapeDtypeStruct((b, 1, w), F32)],
        scratch_shapes=[pltpu.VMEM((tc + 8, w), F32), pltpu.VMEM((1, w), F32)],
        compiler_params=_cp(("parallel", "arbitrary")),
        name="rglru",
    )(z, z, z, conv_state, h0.reshape(b, 1, w), conv_w, conv_b.reshape(1, w), wa_bd,
      ba.reshape(1, w), wi_bd, bi.reshape(1, w), lam.reshape(1, w))
    return ya, hl.reshape(b, w)


def _regroup_blockdiag(wb):
    n, c, _ = wb.shape
    per = LRU_GROUP // c
    g = n // per
    eye = jnp.eye(per, dtype=wb.dtype)
    out = jnp.einsum("gacd,ab->gacbd", wb.reshape(g, per, c, c), eye)
    return out.reshape(g, LRU_GROUP, LRU_GROUP).astype(BF16)


def _bias_kernel(tab_ref, bt_ref, bs_ref, *, past, lpad):
    h = pl.program_id(0)
    d0 = _iota((Q_TILE, Q_TILE), 0) - _iota((Q_TILE, Q_TILE), 1)
    bt_ref[0, 0] = _t5_bias(d0, tab_ref, h)
    bt_ref[0, 1] = _t5_bias(d0 + Q_TILE, tab_ref, h)
    t = bs_ref.shape[1]
    ds = past + _iota((t, lpad), 0) - _iota((t, lpad), 1)
    bs_ref[0] = _t5_bias(ds, tab_ref, h)


def bias_tiles(table, past, t_new, lpad):
    nh = table.shape[1]
    return pl.pallas_call(
        functools.partial(_bias_kernel, past=past, lpad=lpad),
        grid=(nh,),
        in_specs=[pl.BlockSpec(memory_space=pltpu.SMEM)],
        out_specs=[pl.BlockSpec((1, 2, Q_TILE, Q_TILE), lambda h: (h, 0, 0, 0)),
                   pl.BlockSpec((1, t_new, lpad), lambda h: (h, 0, 0))],
        out_shape=[jax.ShapeDtypeStruct((nh, 2, Q_TILE, Q_TILE), F32),
                   jax.ShapeDtypeStruct((nh, t_new, lpad), F32)],
        compiler_params=_cp(("arbitrary",)),
        name="t5_bias_tiles",
    )(table)


def _top_lanes(g, nvalid, k):
    lane = _iota(g.shape, 1).astype(F32)
    gg = jnp.where(lane < nvalid, g, NEG_INF)
    sel = jnp.zeros(g.shape, F32)
    for _ in range(k):
        mx = jnp.max(gg, axis=-1, keepdims=True)
        first = jnp.min(jnp.where(gg == mx, lane, float(g.shape[1])), axis=-1, keepdims=True)
        pick = (lane == first) & (mx > NEG_INF)
        sel = jnp.where(pick, 1.0, sel)
        gg = jnp.where(pick, NEG_INF, gg)
    return sel


def _moba_prompt_kernel(tab_ref, q_ref, k_ref, v_ref, qg_ref, kg_ref, bt_ref, y_ref, kn_ref, *, s):
    hp = pl.program_id(0)
    nb = s // MOBA_BLOCK
    dh = MOBA_HEAD_DIM
    scale = dh ** -0.5
    lane = _iota((1, LANES), 1)
    low = lane < dh

    def headnorm(x, g):
        x2 = x * x
        s0 = jnp.sum(jnp.where(low, x2, 0.0), axis=-1, keepdims=True)
        s1 = jnp.sum(jnp.where(low, 0.0, x2), axis=-1, keepdims=True)
        ms = jnp.where(low, s0, s1) * (1.0 / dh)
        return x * lax.rsqrt(ms + EPS) * g

    qn = headnorm(q_ref[0], qg_ref[...])
    kn = headnorm(k_ref[0], kg_ref[...])
    kn_ref[0] = kn
    kb = kn.astype(BF16)
    vb = v_ref[0].astype(BF16)
    blk_row = _iota((LANES, LANES), 0)
    kmat = jnp.zeros((LANES, LANES), F32)
    for j in range(nb):
        kmean = jnp.mean(kn[j * MOBA_BLOCK:(j + 1) * MOBA_BLOCK], axis=0, keepdims=True)
        kmat = jnp.where(blk_row == j, kmean, kmat)
    qi = _iota((Q_TILE, Q_TILE), 0)
    kj = _iota((Q_TILE, Q_TILE), 1)
    causal = qi >= kj

    for c in range(nb):
        qc = qn[c * Q_TILE:(c + 1) * Q_TILE]
        nk = (c + 1) * MOBA_BLOCK
        outs = []
        for hh in range(2):
            hmask = low if hh == 0 else jnp.logical_not(low)
            qm = jnp.where(hmask, qc, 0.0)
            t31 = tab_ref[N_BUCKETS - 1, 2 * hp + hh]
            sc = _dot_nt(qm.astype(BF16), kb[0:nk]) * scale
            if c > MOBA_TOPK:
                sel = _top_lanes(_dot3(_dot_nt, qm, kmat), c, MOBA_TOPK)
            pieces = []
            for j in range(c + 1):
                sj = sc[:, j * MOBA_BLOCK:(j + 1) * MOBA_BLOCK]
                if j == c:
                    sj = jnp.where(causal, sj + bt_ref[hh, 0], NEG_INF)
                else:
                    sj = sj + (bt_ref[hh, 1] if j == c - 1 else t31)
                    if c > MOBA_TOPK:
                        sj = jnp.where(sel[:, j:j + 1] > 0.0, sj, NEG_INF)
                pieces.append(sj)
            srow = jnp.concatenate(pieces, axis=1) if c else pieces[0]
            mx = jnp.max(srow, axis=-1, keepdims=True)
            p = jnp.exp(srow - mx)
            l = jnp.sum(p, axis=-1, keepdims=True)
            outs.append(_dot_nn(p.astype(BF16), vb[0:nk]) / l)
        y_ref[0, c * Q_TILE:(c + 1) * Q_TILE, :] = jnp.where(low, outs[0], outs[1]).astype(y_ref.dtype)


def moba_prompt(z, table, bt, q_gain, k_gain, w):
    b, s, _ = z.shape
    assert s % MOBA_BLOCK == 0 and Q_TILE == MOBA_BLOCK and s // MOBA_BLOCK <= LANES
    nhp = w // LANES
    g2 = lambda g: jnp.tile(g, LANES // MOBA_HEAD_DIM).reshape(1, LANES)
    col = lambda off: pl.BlockSpec((1, s, LANES), lambda hp, i: (i, 0, off * nhp + hp))
    gspec = pl.BlockSpec((1, LANES), lambda hp, i: (0, 0))
    return pl.pallas_call(
        functools.partial(_moba_prompt_kernel, s=s),
        grid=(nhp, b),
        in_specs=[pl.BlockSpec(memory_space=pltpu.SMEM), col(2), col(3), col(4), gspec, gspec,
                  pl.BlockSpec((2, 2, Q_TILE, Q_TILE), lambda hp, i: (hp, 0, 0, 0))],
        out_specs=[pl.BlockSpec((1, s, LANES), lambda hp, i: (i, 0, hp)),
                   pl.BlockSpec((1, s, LANES), lambda hp, i: (i, 0, hp))],
        out_shape=[jax.ShapeDtypeStruct((b, s, w), BF16), jax.ShapeDtypeStruct((b, s, w), F32)],
        compiler_params=_cp(("parallel", "parallel")),
        name="moba_prompt",
    )(table, z, z, z, g2(q_gain), g2(k_gain), bt)


def _group_onehot(ngroups_pad, width, gsize):
    return jnp.where(_iota((ngroups_pad, width), 0) == _iota((ngroups_pad, width), 1) // gsize,
                     1.0, 0.0).astype(BF16)


def _group_rmsnorm(x, gain, gsize):
    onehot = _group_onehot(LANES, x.shape[1], gsize)
    ssq = _dot_exact01(_dot_nt, x * x, onehot)
    rinv = lax.rsqrt(ssq * (1.0 / gsize) + EPS)
    return x * _dot_exact01(_dot_nn, rinv, onehot) * gain


def _moba_sample_kernel(pt_ref, q_ref, k_ref, v_ref, qg_ref, kg_ref, bs_ref, *rest, npages, t):
    kp_refs = rest[:npages]
    vp_refs = rest[npages:2 * npages]
    y_ref, kn_ref = rest[2 * npages:]
    w = q_ref.shape[-1]
    dh = MOBA_HEAD_DIM
    nh = w // dh
    rows = nh * t
    ppb = MOBA_BLOCK // PAGE_SIZE
    nbp = npages // ppb
    past = npages * PAGE_SIZE
    scale = dh ** -0.5

    qn = _group_rmsnorm(q_ref[0], qg_ref[...], dh)
    kn = _group_rmsnorm(k_ref[0], kg_ref[...], dh)
    kn_ref[0] = kn
    own = _iota((rows, w), 0) // t == _iota((rows, w), 1) // dh
    qbd = jnp.where(own, jnp.concatenate([qn] * nh, axis=0), 0.0)
    qbb = qbd.astype(BF16)
    pad_rows = lambda x: jnp.concatenate([x, jnp.zeros((LANES - t, w), x.dtype)], axis=0)

    pieces, sums = [], []
    for p in range(npages):
        kp = kp_refs[p][0]
        sums.append(jnp.sum(kp, axis=0, keepdims=True))
        pieces.append(_dot_nt(qbb, kp.astype(BF16)))
    pieces.append(_dot_nt(qbb, pad_rows(kn).astype(BF16)))
    lpad = past + LANES
    srow = jnp.concatenate(pieces, axis=1) * scale + bs_ref[...]

    blk_row = _iota((LANES, w), 0)
    kmat = jnp.zeros((LANES, w), F32)
    for j in range(nbp):
        kmat = jnp.where(blk_row == j, sum(sums[j * ppb:(j + 1) * ppb]) * (1.0 / MOBA_BLOCK), kmat)
    sel = _top_lanes(_dot3(_dot_nt, qbd, kmat), nbp, MOBA_TOPK)
    col = _iota((rows, lpad), 1)
    jrow = _iota((rows, lpad), 0) % t
    keep = (col >= past) & (col - past <= jrow)
    for j in range(nbp):
        inblk = (col >= j * MOBA_BLOCK) & (col < (j + 1) * MOBA_BLOCK)
        keep = keep | (inblk & (sel[:, j:j + 1] > 0.0))
    srow = jnp.where(keep, srow, NEG_INF)
    mx = jnp.max(srow, axis=-1, keepdims=True)
    pr = jnp.exp(srow - mx)
    l = jnp.sum(pr, axis=-1, keepdims=True)
    pb = pr.astype(BF16)
    acc = _dot_nn(pb[:, past:], pad_rows(v_ref[0]).astype(BF16))
    for p in range(npages):
        acc = acc + _dot_nn(pb[:, p * PAGE_SIZE:(p + 1) * PAGE_SIZE], vp_refs[p][0].astype(BF16))
    acc = acc / l
    lane_head = _iota((t, w), 1) // dh
    y = jnp.zeros((t, w), F32)
    for h in range(nh):
        y = jnp.where(lane_head == h, acc[h * t:(h + 1) * t, :], y)
    y_ref[0] = y.astype(y_ref.dtype)


def moba_sample(z, pool_k, pool_v, page_table, bs, q_gain, k_gain, w):
    db, t, _ = z.shape
    npages = page_table.shape[1]
    nh = w // MOBA_HEAD_DIM
    assert t == 8 and nh * t == LANES and npages % (MOBA_BLOCK // PAGE_SIZE) == 0
    lpad = npages * PAGE_SIZE + LANES
    gt = lambda g: jnp.tile(g, nh).reshape(1, w)
    col = lambda off: pl.BlockSpec((1, t, w), lambda i, pt: (i, 0, off))
    gspec = pl.BlockSpec((1, w), lambda i, pt: (0, 0))
    page = lambda p: pl.BlockSpec((1, PAGE_SIZE, w), lambda i, pt: (pt[i, p], 0, 0))
    grid_spec = pltpu.PrefetchScalarGridSpec(
        num_scalar_prefetch=1,
        grid=(db,),
        in_specs=[col(2), col(3), col(4), gspec, gspec,
                  pl.BlockSpec((nh * t, lpad), lambda i, pt: (0, 0))]
                 + [page(p) for p in range(npages)] * 2,
        out_specs=[pl.BlockSpec((1, t, w), lambda i, pt: (i, 0, 0)),
                   pl.BlockSpec((1, t, w), lambda i, pt: (i, 0, 0))],
    )
    return pl.pallas_call(
        functools.partial(_moba_sample_kernel, npages=npages, t=t),
        grid_spec=grid_spec,
        out_shape=[jax.ShapeDtypeStruct((db, t, w), BF16), jax.ShapeDtypeStruct((db, t, w), F32)],
        compiler_params=_cp(("arbitrary",), vmem_mib=56),
        name="moba_sample",
    )(page_table, z, z, z, gt(q_gain), gt(k_gain), bs.reshape(nh * t, lpad),
      *([pool_k] * npages), *([pool_v] * npages))


def _head_norm_kernel(x_ref, g_ref, o_ref):
    x = x_ref[...]
    ms = jnp.mean(x * x, axis=-1, keepdims=True)
    o_ref[...] = x * lax.rsqrt(ms + EPS) * g_ref[...]


def head_norm128(z, gain, col0, nheads):
    m = z.shape[0]
    tm = min(m, 1024)
    assert m % tm == 0
    return pl.pallas_call(
        _head_norm_kernel,
        grid=(m // tm, nheads),
        in_specs=[pl.BlockSpec((tm, LANES), lambda i, h: (i, col0 + h)),
                  pl.BlockSpec((1, LANES), lambda i, h: (0, 0))],
        out_specs=pl.BlockSpec((tm, LANES), lambda i, h: (i, h)),
        out_shape=jax.ShapeDtypeStruct((m, nheads * LANES), F32),
        compiler_params=_cp(("parallel", "parallel")),
        name="head_norm128",
    )(z, gain.reshape(1, LANES))


def _index_prompt_kernel(qi_ref, qw_ref, kk_ref, o_ref, *, s):
    c = pl.program_id(1)
    lane = _iota((1, LANES), 1)
    low = lane < IDX_DIM
    kk = jnp.where(low, kk_ref[0], 0.0)
    kk_hi = pltpu.roll(kk, IDX_DIM, 1)
    qw = qw_ref[0]
    acc = jnp.zeros((Q_TILE, s), F32)
    for ih in range(IDX_HEADS):
        blk = qi_ref[0][:, (ih // 2) * LANES:(ih // 2 + 1) * LANES]
        half_low = ih % 2 == 0
        qm = jnp.where(low if half_low else jnp.logical_not(low), blk, 0.0)
        sc = _dot3(_dot_nt, qm, kk if half_low else kk_hi)
        wcol = qw[:, IDX_DIM + ih:IDX_DIM + ih + 1] * (IDX_HEADS ** -0.5)
        acc = acc + wcol * jnp.maximum(sc * (IDX_DIM ** -0.5), 0.0)
    qpos = c * Q_TILE + _iota((Q_TILE, s), 0)
    o_ref[0] = jnp.where(_iota((Q_TILE, s), 1) <= qpos, acc, NEG_INF)


def index_prompt(z, qi_blk, kw_blk):
    b, s, _ = z.shape
    assert s % Q_TILE == 0
    return pl.pallas_call(
        functools.partial(_index_prompt_kernel, s=s),
        grid=(b, s // Q_TILE),
        in_specs=[pl.BlockSpec((1, Q_TILE, IDX_HEADS * IDX_DIM), lambda i, c: (i, c, qi_blk)),
                  pl.BlockSpec((1, Q_TILE, LANES), lambda i, c: (i, c, kw_blk)),
                  pl.BlockSpec((1, s, LANES), lambda i, c: (i, 0, kw_blk))],
        out_specs=pl.BlockSpec((1, Q_TILE, s), lambda i, c: (i, c, 0)),
        out_shape=jax.ShapeDtypeStruct((b, s, s), F32),
        compiler_params=_cp(("parallel", "parallel")),
        name="index_prompt",
    )(z, z, z)


def _index_sample_kernel(pt_ref, qi_ref, qw_ref, *rest, npages, t):
    kp_refs = rest[:npages]
    o_ref = rest[npages]
    qi = qi_ref[0]
    qw = qw_ref[0]
    q2 = jnp.concatenate([qi[:, ih * IDX_DIM:(ih + 1) * IDX_DIM] for ih in range(IDX_HEADS)], axis=0)
    wcol = jnp.concatenate([qw[:, IDX_DIM + ih:IDX_DIM + ih + 1] for ih in range(IDX_HEADS)], axis=0)
    wcol = wcol * (IDX_HEADS ** -0.5)
    knew = jnp.concatenate([qw[:, 0:IDX_DIM], jnp.zeros((LANES - t, IDX_DIM), F32)], axis=0)

    def scores(keys):
        sc = wcol * jnp.maximum(_dot3(_dot_nt, q2, keys) * (IDX_DIM ** -0.5), 0.0)
        return sum(sc[ih * t:(ih + 1) * t, :] for ih in range(IDX_HEADS))

    pieces = [scores(kp_refs[p][0]) for p in range(npages)]
    new = scores(knew)
    jj = _iota((t, LANES), 0)
    cc = _iota((t, LANES), 1)
    pieces.append(jnp.where(cc <= jj, new, NEG_INF))
    o_ref[0] = jnp.concatenate(pieces, axis=1)


def index_sample(z, pool_ik, page_table, qi_blk, kw_blk):
    db, t, _ = z.shape
    npages = page_table.shape[1]
    lpad = npages * PAGE_SIZE + LANES
    grid_spec = pltpu.PrefetchScalarGridSpec(
        num_scalar_prefetch=1,
        grid=(db,),
        in_specs=[pl.BlockSpec((1, t, IDX_HEADS * IDX_DIM), lambda i, pt: (i, 0, qi_blk)),
                  pl.BlockSpec((1, t, LANES), lambda i, pt: (i, 0, kw_blk))]
                 + [pl.BlockSpec((1, PAGE_SIZE, IDX_DIM), (lambda i, pt, p=p: (pt[i, p], 0, 0)))
                    for p in range(npages)],
        out_specs=pl.BlockSpec((1, t, lpad), lambda i, pt: (i, 0, 0)),
    )
    return pl.pallas_call(
        functools.partial(_index_sample_kernel, npages=npages, t=t),
        grid_spec=grid_spec,
        out_shape=jax.ShapeDtypeStruct((db, t, lpad), F32),
        compiler_params=_cp(("arbitrary",)),
        name="index_sample",
    )(page_table, z, z, *([pool_ik] * npages))


def _topk_mask_kernel(x_ref, o_ref, *, topk, period, base, tr):
    l = x_ref.shape[1]
    rowg = pl.program_id(0) * tr + _iota((tr, l), 0)
    valid = _iota((tr, l), 1) <= base + rowg % period
    bits = pltpu.bitcast(x_ref[...], I32)
    key = jnp.where(bits < 0, bits ^ 0x7FFFFFFF, bits)
    key = jnp.where(valid, key, INT_MIN)

    def step(i, lo):
        cand = lo + lax.shift_left(jnp.int32(1), 31 - i)
        cnt = jnp.sum(jnp.where(key >= cand, 1.0, 0.0), axis=-1, keepdims=True)
        return jnp.where(cnt >= topk, cand, lo)

    thr = lax.fori_loop(0, 32, step, jnp.full((tr, 1), INT_MIN, I32))
    gt = key > thr
    need = topk - jnp.sum(jnp.where(gt, 1.0, 0.0), axis=-1, keepdims=True)
    eq = jnp.where(key == thr, 1.0, 0.0)
    upper = jnp.where(_iota((LANES, LANES), 0) < _iota((LANES, LANES), 1), 1.0, 0.0).astype(BF16)
    before = jnp.zeros((tr, 1), F32)
    for j in range(l // LANES):
        e = eq[:, j * LANES:(j + 1) * LANES]
        rank = before + _dot_nn(e.astype(BF16), upper)
        pick = gt[:, j * LANES:(j + 1) * LANES] | ((e > 0.0) & (rank < need))
        pick = pick & valid[:, j * LANES:(j + 1) * LANES]
        o_ref[:, j * LANES:(j + 1) * LANES] = jnp.where(pick, 1.0, 0.0).astype(o_ref.dtype)
        before = before + jnp.sum(e, axis=-1, keepdims=True)


def topk_mask(scores, topk, period, base, dtype):
    r, l = scores.shape
    tr = min(r, 256)
    assert r % tr == 0 and l % LANES == 0
    return pl.pallas_call(
        functools.partial(_topk_mask_kernel, topk=topk, period=period, base=base, tr=tr),
        grid=(r // tr,),
        in_specs=[pl.BlockSpec((tr, l), lambda i: (i, 0))],
        out_specs=pl.BlockSpec((tr, l), lambda i: (i, 0)),
        out_shape=jax.ShapeDtypeStruct((r, l), dtype),
        compiler_params=_cp(("parallel",)),
        name="topk_mask",
    )(scores)


def _dsa_prompt_kernel(tab_ref, q_ref, k_ref, v_ref, m_ref, bt_ref, qg_ref, y_ref, *, s):
    c = pl.program_id(1)
    h = pl.program_id(2)
    q = q_ref[0]
    qn = q * lax.rsqrt(jnp.mean(q * q, axis=-1, keepdims=True) + EPS) * qg_ref[...]
    sc = _dot_nt(qn.astype(BF16), k_ref[0].astype(BF16)) * (DSA_HEAD_DIM ** -0.5)
    t31 = tab_ref[N_BUCKETS - 1, h]
    pieces = []
    for j in range(s // Q_TILE):
        bias = jnp.where(c == j, bt_ref[0, 0], jnp.where(c == j + 1, bt_ref[0, 1], t31))
        pieces.append(sc[:, j * Q_TILE:(j + 1) * Q_TILE] + bias)
    srow = jnp.where(m_ref[0] > 0, jnp.concatenate(pieces, axis=1), NEG_INF)
    mx = jnp.max(srow, axis=-1, keepdims=True)
    p = jnp.exp(srow - mx)
    l = jnp.sum(p, axis=-1, keepdims=True)
    y_ref[0] = (_dot_nn(p.astype(BF16), v_ref[0].astype(BF16)) / l).astype(y_ref.dtype)


def dsa_prompt(z, kn, mask, table, bt, q_gain, nq, nkv, v_blk0):
    b, s, _ = z.shape
    grp = nq // nkv
    return pl.pallas_call(
        functools.partial(_dsa_prompt_kernel, s=s),
        grid=(b, s // Q_TILE, nq),
        in_specs=[pl.BlockSpec(memory_space=pltpu.SMEM),
                  pl.BlockSpec((1, Q_TILE, LANES), lambda i, c, h: (i, c, h)),
                  pl.BlockSpec((1, s, LANES), lambda i, c, h: (i, 0, h // grp)),
                  pl.BlockSpec((1, s, LANES), lambda i, c, h: (i, 0, v_blk0 + h // grp)),
                  pl.BlockSpec((1, Q_TILE, s), lambda i, c, h: (i, c, 0)),
                  pl.BlockSpec((1, 2, Q_TILE, Q_TILE), lambda i, c, h: (h, 0, 0, 0)),
                  pl.BlockSpec((1, LANES), lambda i, c, h: (0, 0))],
        out_specs=pl.BlockSpec((1, Q_TILE, LANES), lambda i, c, h: (i, c, h)),
        out_shape=jax.ShapeDtypeStruct((b, s, nq * LANES), BF16),
        compiler_params=_cp(("parallel", "parallel", "arbitrary")),
        name="dsa_prompt",
    )(table, z, kn, z, mask, bt, q_gain.reshape(1, LANES))


def _dsa_sample_kernel(pt_ref, q_ref, kn_ref, v_ref, m_ref, bs_ref, qg_ref, *rest, npages, t, nq, nkv):
    kp_refs = rest[:npages]
    vp_refs = rest[npages:2 * npages]
    y_ref = rest[2 * npages]
    dh = DSA_HEAD_DIM
    grp = nq // nkv
    wkv = nkv * dh
    past = npages * PAGE_SIZE
    q = q_ref[0]
    g = qg_ref[...]
    zero = jnp.zeros((t, dh), F32)
    rows = []
    for h in range(nq):
        qh = q[:, h * dh:(h + 1) * dh]
        qh = qh * lax.rsqrt(jnp.mean(qh * qh, axis=-1, keepdims=True) + EPS) * g
        rows.append(jnp.concatenate([qh if kv == h // grp else zero for kv in range(nkv)], axis=1))
    qbb = jnp.concatenate(rows, axis=0).astype(BF16)
    pad_rows = lambda x: jnp.concatenate([x, jnp.zeros((LANES - t, wkv), x.dtype)], axis=0)

    pieces = [_dot_nt(qbb, kp_refs[p][0].astype(BF16)) for p in range(npages)]
    pieces.append(_dot_nt(qbb, pad_rows(kn_ref[0]).astype(BF16)))
    srow = jnp.concatenate(pieces, axis=1) * (dh ** -0.5) + bs_ref[...]
    keep = jnp.concatenate([m_ref[0]] * nq, axis=0) > 0.0
    srow = jnp.where(keep, srow, NEG_INF)
    mx = jnp.max(srow, axis=-1, keepdims=True)
    pr = jnp.exp(srow - mx)
    l = jnp.sum(pr, axis=-1, keepdims=True)
    pb = pr.astype(BF16)
    acc = _dot_nn(pb[:, past:], pad_rows(v_ref[0]).astype(BF16))
    for p in range(npages):
        acc = acc + _dot_nn(pb[:, p * PAGE_SIZE:(p + 1) * PAGE_SIZE], vp_refs[p][0].astype(BF16))
    acc = acc / l
    y = jnp.concatenate([acc[h * t:(h + 1) * t, (h // grp) * dh:(h // grp + 1) * dh] for h in range(nq)],
                        axis=1)
    y_ref[0] = y.astype(y_ref.dtype)


def dsa_sample(z, kn, mask, pool_k, pool_v, page_table, bs, q_gain, nq, nkv, v_blk):
    db, t, _ = z.shape
    npages = page_table.shape[1]
    wkv = nkv * DSA_HEAD_DIM
    lpad = npages * PAGE_SIZE + LANES
    assert nq * t == LANES
    page = lambda p: pl.BlockSpec((1, PAGE_SIZE, wkv), lambda i, pt: (pt[i, p], 0, 0))
    grid_spec = pltpu.PrefetchScalarGridSpec(
        num_scalar_prefetch=1,
        grid=(db,),
        in_specs=[pl.BlockSpec((1, t, nq * DSA_HEAD_DIM), lambda i, pt: (i, 0, 0)),
                  pl.BlockSpec((1, t, wkv), lambda i, pt: (i, 0, 0)),
                  pl.BlockSpec((1, t, wkv), lambda i, pt: (i, 0, v_blk)),
                  pl.BlockSpec((1, t, lpad), lambda i, pt: (i, 0, 0)),
                  pl.BlockSpec((nq * t, lpad), lambda i, pt: (0, 0)),
                  pl.BlockSpec((1, LANES), lambda i, pt: (0, 0))]
                 + [page(p) for p in range(npages)] * 2,
        out_specs=pl.BlockSpec((1, t, nq * DSA_HEAD_DIM), lambda i, pt: (i, 0, 0)),
    )
    return pl.pallas_call(
        functools.partial(_dsa_sample_kernel, npages=npages, t=t, nq=nq, nkv=nkv),
        grid_spec=grid_spec,
        out_shape=jax.ShapeDtypeStruct((db, t, nq * DSA_HEAD_DIM), BF16),
        compiler_params=_cp(("arbitrary",)),
        name="dsa_sample",
    )(page_table, z, kn, z, mask, bs.reshape(nq * t, lpad), q_gain.reshape(1, LANES),
      *([pool_k] * npages), *([pool_v] * npages))


def _conv_ffn(x2, shape3, state, norm_g, w_up, conv_w, conv_b, w_down):
    b, t, _ = shape3
    u = norm_matmul(x2, norm_g, w_up)
    u3 = u.reshape(b, t, -1)
    mid = ffn_mid(u3, state, conv_w, conv_b)
    y = matmul_residual(mid.reshape(b * t, -1), w_down, x2)
    return y, u3[:, t - (conv_w.shape[0] - 1):, :]


def kernel(x_prompt, x_sample, state_rglru_h, state_rglru_conv, cache_moba_k, cache_moba_v, state_ffn0_conv, cache_dsa_k, cache_dsa_v, cache_dsa_idx_k, state_ffn1_conv, page_table, rel_bias_table, l0_norm_g, l0_w_in, l0_conv_w, l0_conv_b, l0_gate_a_w, l0_gate_a_b, l0_gate_i_w, l0_gate_i_b, l0_lambda, l0_q_norm_g, l0_k_norm_g, l0_w_out, ffn0_norm_g, ffn0_w_up, ffn0_conv_w, ffn0_conv_b, ffn0_w_down, l1_norm_g, l1_w_in, l1_q_norm_g, l1_k_norm_g, l1_w_out, ffn1_norm_g, ffn1_w_up, ffn1_conv_w, ffn1_conv_b, ffn1_w_down):
    bp, s, d = x_prompt.shape
    db, t, _ = x_sample.shape
    w = state_rglru_h.shape[-1]
    f2 = ffn0_w_up.shape[1]
    n_phys = cache_moba_k.shape[0]
    npages = page_table.shape[1]
    past = npages * PAGE_SIZE
    lpad = past + LANES
    nmh = cache_moba_k.shape[2]
    nkv = cache_dsa_k.shape[2]
    nq = l1_w_out.shape[0] // DSA_HEAD_DIM
    assert cache_moba_k.shape[3] == MOBA_HEAD_DIM and cache_dsa_k.shape[3] == DSA_HEAD_DIM
    assert cache_moba_k.shape[1] == PAGE_SIZE and nmh * MOBA_HEAD_DIM == w
    assert rel_bias_table.shape == (N_BUCKETS, nmh) and nq == nmh
    assert l0_w_in.shape[1] == 5 * w

    xp = x_prompt.reshape(bp * s, d)
    xs = x_sample.reshape(db * t, d)
    bt, bsamp = bias_tiles(rel_bias_table, past, t, lpad)

    wa_bd = _regroup_blockdiag(l0_gate_a_w)
    wi_bd = _regroup_blockdiag(l0_gate_i_w)
    pool_mk = cache_moba_k.reshape(n_phys, PAGE_SIZE, w)
    pool_mv = cache_moba_v.reshape(n_phys, PAGE_SIZE, w)

    def even(x2, b, tt, h0, conv_state, attend):
        z = norm_matmul(x2, l0_norm_g, l0_w_in).reshape(b, tt, 5 * w)
        ya, h_last = rglru_branch(z, h0, conv_state, l0_conv_w, l0_conv_b, wa_bd, l0_gate_a_b,
                                  wi_bd, l0_gate_i_b, l0_lambda)
        yb, kn = attend(z)
        mix = jnp.concatenate([ya, yb], axis=-1).reshape(b * tt, 2 * w)
        out = matmul_residual(mix, l0_w_out, x2)
        new_conv = z[:, tt - (l0_conv_w.shape[0] - 1):, :w]
        v_rows = z[:, :, 4 * w:].reshape(b, tt, nmh, MOBA_HEAD_DIM)
        return out, h_last, new_conv, kn.reshape(b, tt, nmh, MOBA_HEAD_DIM), v_rows

    xp, h_p, conv_p, mk_p, mv_p = even(
        xp, bp, s, jnp.zeros((bp, w), F32), jnp.zeros((bp, l0_conv_w.shape[0] - 1, w), F32),
        lambda z: moba_prompt(z, rel_bias_table, bt, l0_q_norm_g, l0_k_norm_g, w))
    xs, h_s, conv_s, mk_s, mv_s = even(
        xs, db, t, state_rglru_h, state_rglru_conv,
        lambda z: moba_sample(z, pool_mk, pool_mv, page_table, bsamp, l0_q_norm_g, l0_k_norm_g, w))

    xp, f0_p = _conv_ffn(xp, (bp, s, d), jnp.zeros((bp, ffn0_conv_w.shape[0] - 1, f2), F32),
                         ffn0_norm_g, ffn0_w_up, ffn0_conv_w, ffn0_conv_b, ffn0_w_down)
    xs, f0_s = _conv_ffn(xs, (db, t, d), state_ffn0_conv,
                         ffn0_norm_g, ffn0_w_up, ffn0_conv_w, ffn0_conv_b, ffn0_w_down)

    wq = nq * DSA_HEAD_DIM
    wkv = nkv * DSA_HEAD_DIM
    wqi = IDX_HEADS * IDX_DIM
    n_in = l1_w_in.shape[1]
    assert n_in == wq + 2 * wkv + wqi + IDX_DIM + IDX_HEADS
    n_pad = -(-n_in // 512) * 512
    w_in1 = jnp.pad(l1_w_in, ((0, 0), (0, n_pad - n_in)))
    k_blk0 = wq // LANES
    v_blk0 = (wq + wkv) // LANES
    qi_off = wq + 2 * wkv
    assert qi_off % wqi == 0 and (qi_off + wqi) % LANES == 0 and (wq + wkv) % wkv == 0
    qi_blk = qi_off // wqi
    kw_blk = (qi_off + wqi) // LANES
    pool_dk = cache_dsa_k.reshape(n_phys, PAGE_SIZE, wkv)
    pool_dv = cache_dsa_v.reshape(n_phys, PAGE_SIZE, wkv)

    def odd(x2, b, tt, attend):
        z2 = norm_matmul(x2, l1_norm_g, w_in1)
        kn = head_norm128(z2, l1_k_norm_g, k_blk0, nkv)
        z = z2.reshape(b, tt, n_pad)
        y = attend(z, kn.reshape(b, tt, wkv))
        out = matmul_residual(y.reshape(b * tt, wq), l1_w_out, x2)
        v_rows = z[:, :, wq + wkv:wq + 2 * wkv].reshape(b, tt, nkv, DSA_HEAD_DIM)
        ki_rows = z[:, :, qi_off + wqi:qi_off + wqi + IDX_DIM]
        return out, kn.reshape(b, tt, nkv, DSA_HEAD_DIM), v_rows, ki_rows

    def attend_prompt(z, kn):
        isc = index_prompt(z, qi_blk, kw_blk)
        mask = topk_mask(isc.reshape(bp * s, s), min(DSA_TOPK, s // 4), s, 0, BF16).reshape(bp, s, s)
        return dsa_prompt(z, kn, mask, rel_bias_table, bt, l1_q_norm_g, nq, nkv, v_blk0)

    def attend_sample(z, kn):
        isc = index_sample(z, cache_dsa_idx_k, page_table, qi_blk, kw_blk)
        mask = topk_mask(isc.reshape(db * t, lpad), min(DSA_TOPK, (past + t) // 4), t, past, F32)
        return dsa_sample(z, kn, mask.reshape(db, t, lpad), pool_dk, pool_dv, page_table, bsamp,
                          l1_q_norm_g, nq, nkv, (wq + wkv) // wkv)

    xp, dk_p, dv_p, di_p = odd(xp, bp, s, attend_prompt)
    xs, dk_s, dv_s, di_s = odd(xs, db, t, attend_sample)

    xp, f1_p = _conv_ffn(xp, (bp, s, d), jnp.zeros((bp, ffn1_conv_w.shape[0] - 1, f2), F32),
                         ffn1_norm_g, ffn1_w_up, ffn1_conv_w, ffn1_conv_b, ffn1_w_down)
    xs, f1_s = _conv_ffn(xs, (db, t, d), state_ffn1_conv,
                         ffn1_norm_g, ffn1_w_up, ffn1_conv_w, ffn1_conv_b, ffn1_w_down)

    return (xp.reshape(bp, s, d), xs.reshape(db, t, d), h_p, h_s, conv_p, conv_s, mk_p, mk_s, mv_p, mv_s,
            f0_p, f0_s, dk_p, dk_s, dv_p, dv_s, di_p, di_s, f1_p, f1_s)
```

```python
import functools
import math

import numpy as np
import jax
import jax.numpy as jnp
from jax import lax
from jax.experimental import pallas as pl
from jax.experimental.pallas import tpu as pltpu

F32 = jnp.float32
BF16 = jnp.bfloat16
I32 = jnp.int32
NEG_INF = float("-inf")
INT_MIN = -(2 ** 31)

EPS = 1e-6
LRU_C = 8.0
LRU_GROUP = 256
MOBA_BLOCK = 256
MOBA_TOPK = 3
MOBA_HEAD_DIM = 64
DSA_HEAD_DIM = 128
DSA_TOPK = 256
IDX_DIM = 64
IDX_HEADS = 8
PAGE_SIZE = 128
N_BUCKETS = 32
MAX_DISTANCE = 128
Q_TILE = 256
LANES = 128
MIB = 1024 * 1024


def _t5_thresholds():
    exact = N_BUCKETS // 2
    n = np.arange(0, 4 * MAX_DISTANCE)
    nf = np.maximum(n, 1).astype(np.float32)
    large = exact + (np.log(nf / np.float32(exact)) / np.float32(math.log(MAX_DISTANCE / exact))
                     * np.float32(N_BUCKETS - exact)).astype(np.int32)
    bucket = np.where(n < exact, n, np.minimum(large, N_BUCKETS - 1))
    assert np.all(np.diff(bucket) >= 0) and bucket[-1] == N_BUCKETS - 1
    return [int(np.argmax(bucket >= b)) for b in range(N_BUCKETS)]


T5_THR = _t5_thresholds()


def _cp(sem, vmem_mib=48):
    return pltpu.CompilerParams(dimension_semantics=sem, vmem_limit_bytes=vmem_mib * MIB)


def _iota(shape, dim):
    return lax.broadcasted_iota(I32, shape, dim)


def _dot_nt(a, b):
    return lax.dot_general(a, b, (((1,), (1,)), ((), ())), preferred_element_type=F32)


def _dot_nn(a, b):
    return jnp.dot(a, b, preferred_element_type=F32)


def _split2(x):
    hi = x.astype(BF16)
    lo = (x - hi.astype(F32)).astype(BF16)
    return hi, lo


def _split3(x):
    hi = x.astype(BF16)
    r = x - hi.astype(F32)
    mid = r.astype(BF16)
    lo = (r - mid.astype(F32)).astype(BF16)
    return hi, mid, lo


def _dot3(dot, a, b):
    ah, al = _split2(a)
    bh, bl = _split2(b)
    return dot(ah, bh) + (dot(ah, bl) + dot(al, bh))


def _dot_exact01(dot, a, b01):
    h, m, l = _split3(a)
    return dot(h, b01) + (dot(m, b01) + dot(l, b01))


def _t5_bias(dist, tab_ref, h):
    acc = jnp.full(dist.shape, tab_ref[0, h], F32)
    for b in range(1, N_BUCKETS):
        acc = jnp.where(dist >= T5_THR[b], tab_ref[b, h], acc)
    return acc


def _softplus(x):
    return jnp.maximum(x, 0.0) + jnp.log1p(jnp.exp(-jnp.abs(x)))


def _norm_mm_kernel(x_ref, g_ref, w_ref, o_ref, xn_ref):
    @pl.when(pl.program_id(1) == 0)
    def _():
        x = x_ref[...]
        ms = jnp.mean(x * x, axis=-1, keepdims=True)
        xn_ref[...] = (x * lax.rsqrt(ms + EPS) * g_ref[...]).astype(BF16)

    o_ref[...] = _dot_nn(xn_ref[...], w_ref[...].astype(BF16))


def norm_matmul(x, g, w, tn=512):
    m, d = x.shape
    n = w.shape[1]
    tm = min(m, 1024)
    assert m % tm == 0 and n % tn == 0
    return pl.pallas_call(
        _norm_mm_kernel,
        grid=(m // tm, n // tn),
        in_specs=[pl.BlockSpec((tm, d), lambda i, j: (i, 0)),
                  pl.BlockSpec((1, d), lambda i, j: (0, 0)),
                  pl.BlockSpec((d, tn), lambda i, j: (0, j))],
        out_specs=pl.BlockSpec((tm, tn), lambda i, j: (i, j)),
        out_shape=jax.ShapeDtypeStruct((m, n), F32),
        scratch_shapes=[pltpu.VMEM((tm, d), BF16)],
        compiler_params=_cp(("parallel", "arbitrary")),
        name="norm_matmul",
    )(x, g.reshape(1, d), w)


def _mm_res_kernel(a_ref, w_ref, r_ref, o_ref, acc_ref, *, nk):
    k = pl.program_id(2)

    @pl.when(k == 0)
    def _():
        acc_ref[...] = jnp.zeros_like(acc_ref)

    acc_ref[...] += _dot_nn(a_ref[...], w_ref[...].astype(BF16))

    @pl.when(k == nk - 1)
    def _():
        o_ref[...] = acc_ref[...] + r_ref[...]


def matmul_residual(a, w, res, tn=512):
    m, kdim = a.shape
    n = w.shape[1]
    tm = min(m, 1024)
    tk = kdim if kdim <= 2048 else kdim // 4
    assert m % tm == 0 and n % tn == 0 and kdim % tk == 0 and tk % LANES == 0
    nk = kdim // tk
    return pl.pallas_call(
        functools.partial(_mm_res_kernel, nk=nk),
        grid=(m // tm, n // tn, nk),
        in_specs=[pl.BlockSpec((tm, tk), lambda i, j, k: (i, k)),
                  pl.BlockSpec((tk, tn), lambda i, j, k: (k, j)),
                  pl.BlockSpec((tm, tn), lambda i, j, k: (i, j))],
        out_specs=pl.BlockSpec((tm, tn), lambda i, j, k: (i, j)),
        out_shape=jax.ShapeDtypeStruct((m, n), F32),
        scratch_shapes=[pltpu.VMEM((tm, tn), F32)],
        compiler_params=_cp(("parallel", "parallel", "arbitrary")),
        name="matmul_residual",
    )(a, w, res)


def _ffn_mid_kernel(uv_ref, ug_ref, pv_ref, pg_ref, sv_ref, sg_ref, wv_ref, wg_ref, bv_ref, bg_ref,
                    o_ref, ev_ref, eg_ref, *, tc):
    t = pl.program_id(1)

    def conv(u_ref, p_ref, s_ref, w_ref, b_ref, e_ref):
        @pl.when(t == 0)
        def _():
            e_ref[:, 6:8, :] = s_ref[...]

        @pl.when(t > 0)
        def _():
            e_ref[:, 6:8, :] = p_ref[:, 6:8, :]

        e_ref[:, 8:8 + tc, :] = u_ref[...]
        w = w_ref[...]
        return (b_ref[...] + e_ref[:, 6:6 + tc, :] * w[0:1] + e_ref[:, 7:7 + tc, :] * w[1:2]
                + e_ref[:, 8:8 + tc, :] * w[2:3])

    val = conv(uv_ref, pv_ref, sv_ref, wv_ref, bv_ref, ev_ref)
    gate = conv(ug_ref, pg_ref, sg_ref, wg_ref, bg_ref, eg_ref)
    o_ref[...] = (jax.nn.gelu(gate) * val).astype(o_ref.dtype)


def ffn_mid(u, state, conv_w, conv_b):
    b, t, f2 = u.shape
    f = f2 // 2
    tc = min(t, 256)
    cn = f // 4 if (f // 4) % LANES == 0 else 512
    bb = 8 if (tc == t and b % 8 == 0) else 1
    assert t % tc == 0 and tc % 8 == 0 and f % cn == 0 and conv_w.shape[0] == 3
    ncb = f // cn
    r8 = tc // 8

    def cur(off):
        return pl.BlockSpec((bb, tc, cn), lambda i, j, c: (i, j, c + off))

    def prev(off):
        return pl.BlockSpec((bb, 8, cn), lambda i, j, c: (i, jnp.maximum(j * r8 - 1, 0), c + off))

    def st(off):
        return pl.BlockSpec((bb, 2, cn), lambda i, j, c: (i, 0, c + off))

    def wspec(rows, off):
        return pl.BlockSpec((rows, cn), lambda i, j, c: (0, c + off))

    return pl.pallas_call(
        functools.partial(_ffn_mid_kernel, tc=tc),
        grid=(b // bb, t // tc, ncb),
        in_specs=[cur(0), cur(ncb), prev(0), prev(ncb), st(0), st(ncb),
                  wspec(3, 0), wspec(3, ncb), wspec(1, 0), wspec(1, ncb)],
        out_specs=pl.BlockSpec((bb, tc, cn), lambda i, j, c: (i, j, c)),
        out_shape=jax.ShapeDtypeStruct((b, t, f), BF16),
        scratch_shapes=[pltpu.VMEM((bb, tc + 8, cn), F32), pltpu.VMEM((bb, tc + 8, cn), F32)],
        compiler_params=_cp(("parallel", "arbitrary", "arbitrary")),
        name="ffn_mid",
    )(u, u, u, u, state, state, conv_w, conv_w, conv_b.reshape(1, f2), conv_b.reshape(1, f2))


def _rglru_kernel(xa_ref, ga_ref, pv_ref, cs_ref, h0_ref, cw_ref, cb_ref, wa_ref, ba_ref,
                  wi_ref, bi_ref, lam_ref, ya_ref, hl_ref, e_ref, hc_ref, *, tc):
    t = pl.program_id(1)
    w = xa_ref.shape[-1]

    @pl.when(t == 0)
    def _():
        hc_ref[...] = h0_ref[0]
        e_ref[5:8, :] = cs_ref[0]

    @pl.when(t > 0)
    def _():
        e_ref[5:8, :] = pv_ref[0, 5:8, :]

    e_ref[8:8 + tc, :] = xa_ref[0]
    cw = cw_ref[...]
    xc = (cb_ref[...] + e_ref[5:5 + tc, :] * cw[0:1] + e_ref[6:6 + tc, :] * cw[1:2]
          + e_ref[7:7 + tc, :] * cw[2:3] + e_ref[8:8 + tc, :] * cw[3:4])
    xb = xc.astype(BF16)

    def gate(w_ref, b_ref):
        parts = [_dot_nn(xb[:, g * LRU_GROUP:(g + 1) * LRU_GROUP], w_ref[g])
                 for g in range(w // LRU_GROUP)]
        return jax.nn.sigmoid(jnp.concatenate(parts, axis=1) + b_ref[...])

    r = gate(wa_ref, ba_ref)
    i = gate(wi_ref, bi_ref)
    log_a = -LRU_C * r * _softplus(-lam_ref[...])
    a = jnp.exp(log_a)
    u = jnp.sqrt(-jnp.tanh(log_a) * (1.0 + a * a)) * (i * xc)

    row = _iota((tc, w), 0)
    s = 1
    while s < tc:
        a_sh = jnp.where(row >= s, pltpu.roll(a, s, 0), 1.0)
        u_sh = jnp.where(row >= s, pltpu.roll(u, s, 0), 0.0)
        u = u + a * u_sh
        a = a * a_sh
        s *= 2
    h = u + a * hc_ref[...]
    hlast = h[tc - 1:tc, :]
    hc_ref[...] = hlast
    hl_ref[0] = hlast
    ya_ref[0] = (h * jax.nn.gelu(ga_ref[0])).astype(ya_ref.dtype)


def rglru_branch(z, h0, conv_state, conv_w, conv_b, wa_bd, ba, wi_bd, bi, lam):
    b, t, _ = z.shape
    w = h0.shape[-1]
    tc = min(t, 256)
    assert t % tc == 0 and tc % 8 == 0 and conv_w.shape[0] == 4 and w % LRU_GROUP == 0
    r8 = tc // 8
    ng = w // LRU_GROUP
    vec = lambda: pl.BlockSpec((1, w), lambda i, j: (0, 0))
    ya, hl = pl.pallas_call(
        functools.partial(_rglru_kernel, tc=tc),
        grid=(b, t // tc),
        in_specs=[pl.BlockSpec((1, tc, w), lambda i, j: (i, j, 0)),
                  pl.BlockSpec((1, tc, w), lambda i, j: (i, j, 1)),
                  pl.BlockSpec((1, 8, w), lambda i, j: (i, jnp.maximum(j * r8 - 1, 0), 0)),
                  pl.BlockSpec((1, 3, w), lambda i, j: (i, 0, 0)),
                  pl.BlockSpec((1, 1, w), lambda i, j: (i, 0, 0)),
                  pl.BlockSpec((4, w), lambda i, j: (0, 0)),
                  vec(),
                  pl.BlockSpec((ng, LRU_GROUP, LRU_GROUP), lambda i, j: (0, 0, 0)),
                  vec(),
                  pl.BlockSpec((ng, LRU_GROUP, LRU_GROUP), lambda i, j: (0, 0, 0)),
                  vec(), vec()],
        out_specs=[pl.BlockSpec((1, tc, w), lambda i, j: (i, j, 0)),
                   pl.BlockSpec((1, 1, w), lambda i, j: (i, 0, 0))],
        out_shape=[jax.ShapeDtypeStruct((b, t, w), BF16), jax.ShapeDtypeStruct((b, 1, w), F32)],
        scratch_shapes=[pltpu.VMEM((tc + 8, w), F32), pltpu.VMEM((1, w), F32)],
        compiler_params=_cp(("parallel", "arbitrary")),
        name="rglru",
    )(z, z, z, conv_state, h0.reshape(b, 1, w), conv_w, conv_b.reshape(1, w), wa_bd,
      ba.reshape(1, w), wi_bd, bi.reshape(1, w), lam.reshape(1, w))
    return ya, hl.reshape(b, w)


def _regroup_blockdiag(wb):
    n, c, _ = wb.shape
    per = LRU_GROUP // c
    g = n // per
    eye = jnp.eye(per, dtype=wb.dtype)
    out = jnp.einsum("gacd,ab->gacbd", wb.reshape(g, per, c, c), eye)
    return out.reshape(g, LRU_GROUP, LRU_GROUP).astype(BF16)


def _bias_kernel(tab_ref, bt_ref, bs_ref, bx_ref, *, past, lpad, nkv, grp):
    h = pl.program_id(0)
    d0 = _iota((Q_TILE, Q_TILE), 0) - _iota((Q_TILE, Q_TILE), 1)
    bt_ref[0, 0] = _t5_bias(d0, tab_ref, h)
    bt_ref[0, 1] = _t5_bias(d0 + Q_TILE, tab_ref, h)
    t = bs_ref.shape[1]
    ds = past + _iota((t, lpad), 0) - _iota((t, lpad), 1)
    bs_ref[0] = _t5_bias(ds, tab_ref, h)
    col = _iota((t, past * nkv), 1)
    dx = past + _iota((t, past * nkv), 0) - col // nkv
    bx_ref[0] = jnp.where(col % nkv == h // grp, _t5_bias(dx, tab_ref, h), NEG_INF)


def bias_tiles(table, past, t_new, lpad, nkv):
    nh = table.shape[1]
    return pl.pallas_call(
        functools.partial(_bias_kernel, past=past, lpad=lpad, nkv=nkv, grp=nh // nkv),
        grid=(nh,),
        in_specs=[pl.BlockSpec(memory_space=pltpu.SMEM)],
        out_specs=[pl.BlockSpec((1, 2, Q_TILE, Q_TILE), lambda h: (h, 0, 0, 0)),
                   pl.BlockSpec((1, t_new, lpad), lambda h: (h, 0, 0)),
                   pl.BlockSpec((1, t_new, past * nkv), lambda h: (h, 0, 0))],
        out_shape=[jax.ShapeDtypeStruct((nh, 2, Q_TILE, Q_TILE), F32),
                   jax.ShapeDtypeStruct((nh, t_new, lpad), F32),
                   jax.ShapeDtypeStruct((nh, t_new, past * nkv), F32)],
        compiler_params=_cp(("arbitrary",)),
        name="t5_bias_tiles",
    )(table)


def _top_lanes(g, nvalid, k):
    lane = _iota(g.shape, 1).astype(F32)
    gg = jnp.where(lane < nvalid, g, NEG_INF)
    sel = jnp.zeros(g.shape, F32)
    for _ in range(k):
        mx = jnp.max(gg, axis=-1, keepdims=True)
        first = jnp.min(jnp.where(gg == mx, lane, float(g.shape[1])), axis=-1, keepdims=True)
        pick = (lane == first) & (mx > NEG_INF)
        sel = jnp.where(pick, 1.0, sel)
        gg = jnp.where(pick, NEG_INF, gg)
    return sel


def _moba_prompt_kernel(tab_ref, q_ref, k_ref, v_ref, qg_ref, kg_ref, bt_ref, y_ref, kn_ref, *, s):
    hp = pl.program_id(0)
    nb = s // MOBA_BLOCK
    dh = MOBA_HEAD_DIM
    scale = dh ** -0.5
    lane = _iota((1, LANES), 1)
    low = lane < dh

    def headnorm(x, g):
        x2 = x * x
        s0 = jnp.sum(jnp.where(low, x2, 0.0), axis=-1, keepdims=True)
        s1 = jnp.sum(jnp.where(low, 0.0, x2), axis=-1, keepdims=True)
        ms = jnp.where(low, s0, s1) * (1.0 / dh)
        return x * lax.rsqrt(ms + EPS) * g

    qn = headnorm(q_ref[0], qg_ref[...])
    kn = headnorm(k_ref[0], kg_ref[...])
    kn_ref[0] = kn
    kb = kn.astype(BF16)
    vb = v_ref[0].astype(BF16)
    blk_row = _iota((LANES, LANES), 0)
    kmat = jnp.zeros((LANES, LANES), F32)
    for j in range(nb):
        kmean = jnp.mean(kn[j * MOBA_BLOCK:(j + 1) * MOBA_BLOCK], axis=0, keepdims=True)
        kmat = jnp.where(blk_row == j, kmean, kmat)
    qi = _iota((Q_TILE, Q_TILE), 0)
    kj = _iota((Q_TILE, Q_TILE), 1)
    causal = qi >= kj

    for c in range(nb):
        qc = qn[c * Q_TILE:(c + 1) * Q_TILE]
        nk = (c + 1) * MOBA_BLOCK
        outs = []
        for hh in range(2):
            hmask = low if hh == 0 else jnp.logical_not(low)
            qm = jnp.where(hmask, qc, 0.0)
            t31 = tab_ref[N_BUCKETS - 1, 2 * hp + hh]
            sc = _dot_nt(qm.astype(BF16), kb[0:nk]) * scale
            if c > MOBA_TOPK:
                sel = _top_lanes(_dot3(_dot_nt, qm, kmat), c, MOBA_TOPK)
            pieces = []
            for j in range(c + 1):
                sj = sc[:, j * MOBA_BLOCK:(j + 1) * MOBA_BLOCK]
                if j == c:
                    sj = jnp.where(causal, sj + bt_ref[hh, 0], NEG_INF)
                else:
                    sj = sj + (bt_ref[hh, 1] if j == c - 1 else t31)
                    if c > MOBA_TOPK:
                        sj = jnp.where(sel[:, j:j + 1] > 0.0, sj, NEG_INF)
                pieces.append(sj)
            srow = jnp.concatenate(pieces, axis=1) if c else pieces[0]
            mx = jnp.max(srow, axis=-1, keepdims=True)
            p = jnp.exp(srow - mx)
            l = jnp.sum(p, axis=-1, keepdims=True)
            outs.append(_dot_nn(p.astype(BF16), vb[0:nk]) / l)
        y_ref[0, c * Q_TILE:(c + 1) * Q_TILE, :] = jnp.where(low, outs[0], outs[1]).astype(y_ref.dtype)


def moba_prompt(z, table, bt, q_gain, k_gain, w):
    b, s, _ = z.shape
    assert s % MOBA_BLOCK == 0 and Q_TILE == MOBA_BLOCK and s // MOBA_BLOCK <= LANES
    nhp = w // LANES
    g2 = lambda g: jnp.tile(g, LANES // MOBA_HEAD_DIM).reshape(1, LANES)
    col = lambda off: pl.BlockSpec((1, s, LANES), lambda hp, i: (i, 0, off * nhp + hp))
    gspec = pl.BlockSpec((1, LANES), lambda hp, i: (0, 0))
    return pl.pallas_call(
        functools.partial(_moba_prompt_kernel, s=s),
        grid=(nhp, b),
        in_specs=[pl.BlockSpec(memory_space=pltpu.SMEM), col(2), col(3), col(4), gspec, gspec,
                  pl.BlockSpec((2, 2, Q_TILE, Q_TILE), lambda hp, i: (hp, 0, 0, 0))],
        out_specs=[pl.BlockSpec((1, s, LANES), lambda hp, i: (i, 0, hp)),
                   pl.BlockSpec((1, s, LANES), lambda hp, i: (i, 0, hp))],
        out_shape=[jax.ShapeDtypeStruct((b, s, w), BF16), jax.ShapeDtypeStruct((b, s, w), F32)],
        compiler_params=_cp(("parallel", "parallel")),
        name="moba_prompt",
    )(table, z, z, z, g2(q_gain), g2(k_gain), bt)


def _group_onehot(ngroups_pad, width, gsize):
    return jnp.where(_iota((ngroups_pad, width), 0) == _iota((ngroups_pad, width), 1) // gsize,
                     1.0, 0.0).astype(BF16)


def _group_rmsnorm(x, gain, gsize):
    onehot = _group_onehot(LANES, x.shape[1], gsize)
    ssq = _dot_exact01(_dot_nt, x * x, onehot)
    rinv = lax.rsqrt(ssq * (1.0 / gsize) + EPS)
    return x * _dot_exact01(_dot_nn, rinv, onehot) * gain


def _moba_sample_kernel(pt_ref, q_ref, k_ref, v_ref, qg_ref, kg_ref, bs_ref, *rest, npages, t):
    kp_refs = rest[:npages]
    vp_refs = rest[npages:2 * npages]
    y_ref, kn_ref = rest[2 * npages:]
    w = q_ref.shape[-1]
    dh = MOBA_HEAD_DIM
    nh = w // dh
    rows = nh * t
    ppb = MOBA_BLOCK // PAGE_SIZE
    nbp = npages // ppb
    past = npages * PAGE_SIZE
    scale = dh ** -0.5

    qn = _group_rmsnorm(q_ref[0], qg_ref[...], dh)
    kn = _group_rmsnorm(k_ref[0], kg_ref[...], dh)
    kn_ref[0] = kn
    own = _iota((rows, w), 0) // t == _iota((rows, w), 1) // dh
    qbd = jnp.where(own, jnp.concatenate([qn] * nh, axis=0), 0.0)
    qbb = qbd.astype(BF16)
    pad_rows = lambda x: jnp.concatenate([x, jnp.zeros((LANES - t, w), x.dtype)], axis=0)

    pieces, sums = [], []
    for p in range(npages):
        kpt = kp_refs[p][0]
        sums.append(jnp.sum(kpt, axis=1, keepdims=True))
        pieces.append(_dot_nn(qbb, kpt.astype(BF16)))
    pieces.append(_dot_nt(qbb, pad_rows(kn).astype(BF16)))
    lpad = past + LANES
    srow = jnp.concatenate(pieces, axis=1) * scale + bs_ref[...]

    blk_lane = _iota((w, LANES), 1)
    kmat = jnp.zeros((w, LANES), F32)
    for j in range(nbp):
        kmat = jnp.where(blk_lane == j, sum(sums[j * ppb:(j + 1) * ppb]) * (1.0 / MOBA_BLOCK), kmat)
    sel = _top_lanes(_dot3(_dot_nn, qbd, kmat), nbp, MOBA_TOPK)
    col = _iota((rows, lpad), 1)
    jrow = _iota((rows, lpad), 0) % t
    keep = (col >= past) & (col - past <= jrow)
    for j in range(nbp):
        inblk = (col >= j * MOBA_BLOCK) & (col < (j + 1) * MOBA_BLOCK)
        keep = keep | (inblk & (sel[:, j:j + 1] > 0.0))
    srow = jnp.where(keep, srow, NEG_INF)
    mx = jnp.max(srow, axis=-1, keepdims=True)
    pr = jnp.exp(srow - mx)
    l = jnp.sum(pr, axis=-1, keepdims=True)
    pb = pr.astype(BF16)
    acc = _dot_nn(pb[:, past:], pad_rows(v_ref[0]).astype(BF16))
    for p in range(npages):
        acc = acc + _dot_nt(pb[:, p * PAGE_SIZE:(p + 1) * PAGE_SIZE], vp_refs[p][0].astype(BF16))
    acc = acc / l
    lane_head = _iota((t, w), 1) // dh
    y = jnp.zeros((t, w), F32)
    for h in range(nh):
        y = jnp.where(lane_head == h, acc[h * t:(h + 1) * t, :], y)
    y_ref[0] = y.astype(y_ref.dtype)


def moba_sample(z, pool_k, pool_v, page_table, bs, q_gain, k_gain, w):
    db, t, _ = z.shape
    npages = page_table.shape[1]
    nh = w // MOBA_HEAD_DIM
    assert t == 8 and nh * t == LANES and npages % (MOBA_BLOCK // PAGE_SIZE) == 0
    lpad = npages * PAGE_SIZE + LANES
    gt = lambda g: jnp.tile(g, nh).reshape(1, w)
    col = lambda off: pl.BlockSpec((1, t, w), lambda i, pt: (i, 0, off))
    gspec = pl.BlockSpec((1, w), lambda i, pt: (0, 0))
    page = lambda p: pl.BlockSpec((1, w, PAGE_SIZE), lambda i, pt: (pt[i, p], 0, 0))
    grid_spec = pltpu.PrefetchScalarGridSpec(
        num_scalar_prefetch=1,
        grid=(db,),
        in_specs=[col(2), col(3), col(4), gspec, gspec,
                  pl.BlockSpec((nh * t, lpad), lambda i, pt: (0, 0))]
                 + [page(p) for p in range(npages)] * 2,
        out_specs=[pl.BlockSpec((1, t, w), lambda i, pt: (i, 0, 0)),
                   pl.BlockSpec((1, t, w), lambda i, pt: (i, 0, 0))],
    )
    return pl.pallas_call(
        functools.partial(_moba_sample_kernel, npages=npages, t=t),
        grid_spec=grid_spec,
        out_shape=[jax.ShapeDtypeStruct((db, t, w), BF16), jax.ShapeDtypeStruct((db, t, w), F32)],
        compiler_params=_cp(("arbitrary",), vmem_mib=56),
        name="moba_sample",
    )(page_table, z, z, z, gt(q_gain), gt(k_gain), bs.reshape(nh * t, lpad),
      *([pool_k] * npages), *([pool_v] * npages))


def _kv_prep_kernel(k_ref, v_ref, g_ref, kn_ref, knb_ref, vb_ref):
    x = k_ref[...]
    ms = jnp.mean(x * x, axis=-1, keepdims=True)
    kn = x * lax.rsqrt(ms + EPS) * g_ref[...]
    kn_ref[...] = kn
    knb_ref[...] = kn.astype(BF16)
    vb_ref[...] = v_ref[...].astype(BF16)


def kv_prep(z, gain, k_blk0, v_blk0, nheads):
    m = z.shape[0]
    tm = min(m, 1024)
    assert m % tm == 0
    out = pl.BlockSpec((tm, LANES), lambda i, h: (i, h))
    return pl.pallas_call(
        _kv_prep_kernel,
        grid=(m // tm, nheads),
        in_specs=[pl.BlockSpec((tm, LANES), lambda i, h: (i, k_blk0 + h)),
                  pl.BlockSpec((tm, LANES), lambda i, h: (i, v_blk0 + h)),
                  pl.BlockSpec((1, LANES), lambda i, h: (0, 0))],
        out_specs=[out, out, out],
        out_shape=[jax.ShapeDtypeStruct((m, nheads * LANES), F32),
                   jax.ShapeDtypeStruct((m, nheads * LANES), BF16),
                   jax.ShapeDtypeStruct((m, nheads * LANES), BF16)],
        compiler_params=_cp(("parallel", "parallel")),
        name="kv_prep",
    )(z, z, gain.reshape(1, LANES))


def _index_prompt_kernel(qi_ref, qw_ref, kk_ref, o_ref, *, s):
    c = pl.program_id(1)
    lane = _iota((1, LANES), 1)
    low = lane < IDX_DIM
    kk = jnp.where(low, kk_ref[0], 0.0)
    kk_hi = pltpu.roll(kk, IDX_DIM, 1)
    qw = qw_ref[0]
    acc = jnp.zeros((Q_TILE, s), F32)
    for ih in range(IDX_HEADS):
        blk = qi_ref[0][:, (ih // 2) * LANES:(ih // 2 + 1) * LANES]
        half_low = ih % 2 == 0
        qm = jnp.where(low if half_low else jnp.logical_not(low), blk, 0.0)
        sc = _dot3(_dot_nt, qm, kk if half_low else kk_hi)
        wcol = qw[:, IDX_DIM + ih:IDX_DIM + ih + 1] * (IDX_HEADS ** -0.5)
        acc = acc + wcol * jnp.maximum(sc * (IDX_DIM ** -0.5), 0.0)
    qpos = c * Q_TILE + _iota((Q_TILE, s), 0)
    o_ref[0] = jnp.where(_iota((Q_TILE, s), 1) <= qpos, acc, NEG_INF)


def index_prompt(z, qi_blk, kw_blk):
    b, s, _ = z.shape
    assert s % Q_TILE == 0
    return pl.pallas_call(
        functools.partial(_index_prompt_kernel, s=s),
        grid=(b, s // Q_TILE),
        in_specs=[pl.BlockSpec((1, Q_TILE, IDX_HEADS * IDX_DIM), lambda i, c: (i, c, qi_blk)),
                  pl.BlockSpec((1, Q_TILE, LANES), lambda i, c: (i, c, kw_blk)),
                  pl.BlockSpec((1, s, LANES), lambda i, c: (i, 0, kw_blk))],
        out_specs=pl.BlockSpec((1, Q_TILE, s), lambda i, c: (i, c, 0)),
        out_shape=jax.ShapeDtypeStruct((b, s, s), F32),
        compiler_params=_cp(("parallel", "parallel")),
        name="index_prompt",
    )(z, z, z)


def _index_sample_kernel(pt_ref, qi_ref, qw_ref, *rest, npages, t):
    kp_refs = rest[:npages]
    o_ref = rest[npages]
    qi = qi_ref[0]
    qw = qw_ref[0]
    q2 = jnp.concatenate([qi[:, ih * IDX_DIM:(ih + 1) * IDX_DIM] for ih in range(IDX_HEADS)], axis=0)
    wcol = jnp.concatenate([qw[:, IDX_DIM + ih:IDX_DIM + ih + 1] for ih in range(IDX_HEADS)], axis=0)
    wcol = wcol * (IDX_HEADS ** -0.5)
    knew = jnp.concatenate([qw[:, 0:IDX_DIM], jnp.zeros((LANES - t, IDX_DIM), F32)], axis=0)

    def scores(dot, keys):
        sc = wcol * jnp.maximum(_dot3(dot, q2, keys) * (IDX_DIM ** -0.5), 0.0)
        return sum(sc[ih * t:(ih + 1) * t, :] for ih in range(IDX_HEADS))

    pieces = [scores(_dot_nn, kp_refs[p][0]) for p in range(npages)]
    new = scores(_dot_nt, knew)
    jj = _iota((t, LANES), 0)
    cc = _iota((t, LANES), 1)
    pieces.append(jnp.where(cc <= jj, new, NEG_INF))
    o_ref[0] = jnp.concatenate(pieces, axis=1)


def index_sample(z, pool_ik, page_table, qi_blk, kw_blk):
    db, t, _ = z.shape
    npages = page_table.shape[1]
    lpad = npages * PAGE_SIZE + LANES
    grid_spec = pltpu.PrefetchScalarGridSpec(
        num_scalar_prefetch=1,
        grid=(db,),
        in_specs=[pl.BlockSpec((1, t, IDX_HEADS * IDX_DIM), lambda i, pt: (i, 0, qi_blk)),
                  pl.BlockSpec((1, t, LANES), lambda i, pt: (i, 0, kw_blk))]
                 + [pl.BlockSpec((1, IDX_DIM, PAGE_SIZE), (lambda i, pt, p=p: (pt[i, p], 0, 0)))
                    for p in range(npages)],
        out_specs=pl.BlockSpec((1, t, lpad), lambda i, pt: (i, 0, 0)),
    )
    return pl.pallas_call(
        functools.partial(_index_sample_kernel, npages=npages, t=t),
        grid_spec=grid_spec,
        out_shape=jax.ShapeDtypeStruct((db, t, lpad), F32),
        compiler_params=_cp(("arbitrary",)),
        name="index_sample",
    )(page_table, z, z, *([pool_ik] * npages))


def _topk_mask_kernel(x_ref, o_ref, *, topk, period, base, tr):
    l = x_ref.shape[1]
    rowg = pl.program_id(0) * tr + _iota((tr, l), 0)
    valid = _iota((tr, l), 1) <= base + rowg % period
    bits = pltpu.bitcast(x_ref[...], I32)
    key = jnp.where(bits < 0, bits ^ 0x7FFFFFFF, bits)
    key = jnp.where(valid, key, INT_MIN)

    def step(i, lo):
        cand = lo + lax.shift_left(jnp.int32(1), 31 - i)
        cnt = jnp.sum(jnp.where(key >= cand, 1.0, 0.0), axis=-1, keepdims=True)
        return jnp.where(cnt >= topk, cand, lo)

    thr = lax.fori_loop(0, 32, step, jnp.full((tr, 1), INT_MIN, I32))
    gt = key > thr
    need = topk - jnp.sum(jnp.where(gt, 1.0, 0.0), axis=-1, keepdims=True)
    eq = jnp.where(key == thr, 1.0, 0.0)
    upper = jnp.where(_iota((LANES, LANES), 0) < _iota((LANES, LANES), 1), 1.0, 0.0).astype(BF16)
    before = jnp.zeros((tr, 1), F32)
    for j in range(l // LANES):
        e = eq[:, j * LANES:(j + 1) * LANES]
        rank = before + _dot_nn(e.astype(BF16), upper)
        pick = gt[:, j * LANES:(j + 1) * LANES] | ((e > 0.0) & (rank < need))
        pick = pick & valid[:, j * LANES:(j + 1) * LANES]
        o_ref[:, j * LANES:(j + 1) * LANES] = jnp.where(pick, 1.0, 0.0).astype(o_ref.dtype)
        before = before + jnp.sum(e, axis=-1, keepdims=True)


def topk_mask(scores, topk, period, base, dtype):
    r, l = scores.shape
    tr = min(r, 256)
    assert r % tr == 0 and l % LANES == 0
    return pl.pallas_call(
        functools.partial(_topk_mask_kernel, topk=topk, period=period, base=base, tr=tr),
        grid=(r // tr,),
        in_specs=[pl.BlockSpec((tr, l), lambda i: (i, 0))],
        out_specs=pl.BlockSpec((tr, l), lambda i: (i, 0)),
        out_shape=jax.ShapeDtypeStruct((r, l), dtype),
        compiler_params=_cp(("parallel",)),
        name="topk_mask",
    )(scores)


def _dsa_prompt_kernel(tab_ref, q_ref, k_ref, v_ref, m_ref, bt_ref, qg_ref, y_ref, *, s):
    c = pl.program_id(1)
    h = pl.program_id(2)
    q = q_ref[0]
    qb = (q * lax.rsqrt(jnp.mean(q * q, axis=-1, keepdims=True) + EPS) * qg_ref[...]).astype(BF16)
    t31 = tab_ref[N_BUCKETS - 1, h]

    def tile(cc):
        nk = (cc + 1) * Q_TILE
        sc = _dot_nt(qb, k_ref[0, 0:nk, :]) * (DSA_HEAD_DIM ** -0.5)
        pieces = []
        for j in range(cc + 1):
            bias = bt_ref[0, 0] if j == cc else (bt_ref[0, 1] if j == cc - 1 else t31)
            pieces.append(sc[:, j * Q_TILE:(j + 1) * Q_TILE] + bias)
        srow = jnp.concatenate(pieces, axis=1) if cc else pieces[0]
        srow = jnp.where(m_ref[0, :, 0:nk] > 0, srow, NEG_INF)
        mx = jnp.max(srow, axis=-1, keepdims=True)
        p = jnp.exp(srow - mx)
        l = jnp.sum(p, axis=-1, keepdims=True)
        y_ref[0] = (_dot_nn(p.astype(BF16), v_ref[0, 0:nk, :]) / l).astype(y_ref.dtype)

    for cc in range(s // Q_TILE):
        pl.when(c == cc)(functools.partial(tile, cc))


def dsa_prompt(z, knb, vb, mask, table, bt, q_gain, nq, nkv):
    b, s, _ = z.shape
    grp = nq // nkv
    return pl.pallas_call(
        functools.partial(_dsa_prompt_kernel, s=s),
        grid=(b, s // Q_TILE, nq),
        in_specs=[pl.BlockSpec(memory_space=pltpu.SMEM),
                  pl.BlockSpec((1, Q_TILE, LANES), lambda i, c, h: (i, c, h)),
                  pl.BlockSpec((1, s, LANES), lambda i, c, h: (i, 0, h // grp)),
                  pl.BlockSpec((1, s, LANES), lambda i, c, h: (i, 0, h // grp)),
                  pl.BlockSpec((1, Q_TILE, s), lambda i, c, h: (i, c, 0)),
                  pl.BlockSpec((1, 2, Q_TILE, Q_TILE), lambda i, c, h: (h, 0, 0, 0)),
                  pl.BlockSpec((1, LANES), lambda i, c, h: (0, 0))],
        out_specs=pl.BlockSpec((1, Q_TILE, LANES), lambda i, c, h: (i, c, h)),
        out_shape=jax.ShapeDtypeStruct((b, s, nq * LANES), BF16),
        compiler_params=_cp(("parallel", "parallel", "arbitrary")),
        name="dsa_prompt",
    )(table, z, knb, vb, mask, bt, q_gain.reshape(1, LANES))


def _dsa_sample_kernel(pt_ref, q_ref, kn_ref, v_ref, m_ref, bs_ref, bx_ref, qg_ref, *rest,
                       npages, t, nq, nkv):
    kp_refs = rest[:npages]
    vp_refs = rest[npages:2 * npages]
    y_ref = rest[2 * npages]
    dh = DSA_HEAD_DIM
    grp = nq // nkv
    wkv = nkv * dh
    past = npages * PAGE_SIZE
    pcols = PAGE_SIZE * nkv
    scale = dh ** -0.5
    q = q_ref[0]
    g = qg_ref[...]
    zero = jnp.zeros((t, dh), F32)
    qrows, qstruct = [], []
    for h in range(nq):
        qh = q[:, h * dh:(h + 1) * dh]
        qh = qh * lax.rsqrt(jnp.mean(qh * qh, axis=-1, keepdims=True) + EPS) * g
        qrows.append(qh)
        qstruct.append(jnp.concatenate([qh if kv == h // grp else zero for kv in range(nkv)], axis=1))
    qall = jnp.concatenate(qrows, axis=0).astype(BF16)
    qsb = jnp.concatenate(qstruct, axis=0).astype(BF16)
    pad_rows = lambda x: jnp.concatenate([x, jnp.zeros((LANES - t, wkv), x.dtype)], axis=0)
    keep = jnp.concatenate([m_ref[0]] * nq, axis=0)
    keepb = keep.astype(BF16)
    spread = jnp.where(_iota((PAGE_SIZE, pcols), 0) == _iota((PAGE_SIZE, pcols), 1) // nkv,
                       1.0, 0.0).astype(BF16)

    pieces = []
    for p in range(npages):
        sp = _dot_nt(qall, kp_refs[p][...].astype(BF16)) * scale + bx_ref[:, p * pcols:(p + 1) * pcols]
        kexp = _dot_nn(keepb[:, p * PAGE_SIZE:(p + 1) * PAGE_SIZE], spread)
        pieces.append(jnp.where(kexp > 0.5, sp, NEG_INF))
    loc = _dot_nt(qsb, pad_rows(kn_ref[0]).astype(BF16)) * scale + bs_ref[:, past:]
    pieces.append(jnp.where(keep[:, past:] > 0.5, loc, NEG_INF))
    srow = jnp.concatenate(pieces, axis=1)
    mx = jnp.max(srow, axis=-1, keepdims=True)
    pr = jnp.exp(srow - mx)
    l = jnp.sum(pr, axis=-1, keepdims=True)
    pb = pr.astype(BF16)
    accl = _dot_nn(pb[:, npages * pcols:], pad_rows(v_ref[0]).astype(BF16))
    row_grp = _iota((nq * t, dh), 0) // (t * grp)
    acc = jnp.zeros((nq * t, dh), F32)
    for kv in range(nkv):
        acc = jnp.where(row_grp == kv, accl[:, kv * dh:(kv + 1) * dh], acc)
    for p in range(npages):
        acc = acc + _dot_nn(pb[:, p * pcols:(p + 1) * pcols], vp_refs[p][...].astype(BF16))
    acc = acc / l
    y_ref[0] = jnp.concatenate([acc[h * t:(h + 1) * t, :] for h in range(nq)], axis=1).astype(y_ref.dtype)


def dsa_sample(z, kn, mask, pool_k, pool_v, page_table, bs, bx, q_gain, nq, nkv, v_blk):
    db, t, _ = z.shape
    npages = page_table.shape[1]
    wkv = nkv * DSA_HEAD_DIM
    past = npages * PAGE_SIZE
    lpad = past + LANES
    assert nq * t == LANES
    page = lambda p: pl.BlockSpec((PAGE_SIZE * nkv, DSA_HEAD_DIM), lambda i, pt: (pt[i, p], 0))
    grid_spec = pltpu.PrefetchScalarGridSpec(
        num_scalar_prefetch=1,
        grid=(db,),
        in_specs=[pl.BlockSpec((1, t, nq * DSA_HEAD_DIM), lambda i, pt: (i, 0, 0)),
                  pl.BlockSpec((1, t, wkv), lambda i, pt: (i, 0, 0)),
                  pl.BlockSpec((1, t, wkv), lambda i, pt: (i, 0, v_blk)),
                  pl.BlockSpec((1, t, lpad), lambda i, pt: (i, 0, 0)),
                  pl.BlockSpec((nq * t, lpad), lambda i, pt: (0, 0)),
                  pl.BlockSpec((nq * t, past * nkv), lambda i, pt: (0, 0)),
                  pl.BlockSpec((1, LANES), lambda i, pt: (0, 0))]
                 + [page(p) for p in range(npages)] * 2,
        out_specs=pl.BlockSpec((1, t, nq * DSA_HEAD_DIM), lambda i, pt: (i, 0, 0)),
    )
    return pl.pallas_call(
        functools.partial(_dsa_sample_kernel, npages=npages, t=t, nq=nq, nkv=nkv),
        grid_spec=grid_spec,
        out_shape=jax.ShapeDtypeStruct((db, t, nq * DSA_HEAD_DIM), BF16),
        compiler_params=_cp(("arbitrary",)),
        name="dsa_sample",
    )(page_table, z, kn, z, mask, bs.reshape(nq * t, lpad), bx.reshape(nq * t, past * nkv),
      q_gain.reshape(1, LANES), *([pool_k] * npages), *([pool_v] * npages))


def _conv_ffn(x2, shape3, state, norm_g, w_up, conv_w, conv_b, w_down):
    b, t, _ = shape3
    u = norm_matmul(x2, norm_g, w_up)
    u3 = u.reshape(b, t, -1)
    mid = ffn_mid(u3, state, conv_w, conv_b)
    y = matmul_residual(mid.reshape(b * t, -1), w_down, x2)
    return y, u3[:, t - (conv_w.shape[0] - 1):, :]


def kernel(x_prompt, x_sample, state_rglru_h, state_rglru_conv, cache_moba_k, cache_moba_v, state_ffn0_conv, cache_dsa_k, cache_dsa_v, cache_dsa_idx_k, state_ffn1_conv, page_table, rel_bias_table, l0_norm_g, l0_w_in, l0_conv_w, l0_conv_b, l0_gate_a_w, l0_gate_a_b, l0_gate_i_w, l0_gate_i_b, l0_lambda, l0_q_norm_g, l0_k_norm_g, l0_w_out, ffn0_norm_g, ffn0_w_up, ffn0_conv_w, ffn0_conv_b, ffn0_w_down, l1_norm_g, l1_w_in, l1_q_norm_g, l1_k_norm_g, l1_w_out, ffn1_norm_g, ffn1_w_up, ffn1_conv_w, ffn1_conv_b, ffn1_w_down):
    bp, s, d = x_prompt.shape
    db, t, _ = x_sample.shape
    w = state_rglru_h.shape[-1]
    f2 = ffn0_w_up.shape[1]
    n_phys = cache_moba_k.shape[0]
    npages = page_table.shape[1]
    past = npages * PAGE_SIZE
    lpad = past + LANES
    nmh = cache_moba_k.shape[2]
    nkv = cache_dsa_k.shape[2]
    nq = l1_w_out.shape[0] // DSA_HEAD_DIM
    assert cache_moba_k.shape[3] == MOBA_HEAD_DIM and cache_dsa_k.shape[3] == DSA_HEAD_DIM
    assert cache_moba_k.shape[1] == PAGE_SIZE and nmh * MOBA_HEAD_DIM == w
    assert rel_bias_table.shape == (N_BUCKETS, nmh) and nq == nmh
    assert l0_w_in.shape[1] == 5 * w

    xp = x_prompt.reshape(bp * s, d)
    xs = x_sample.reshape(db * t, d)
    bt, bsamp, bsx = bias_tiles(rel_bias_table, past, t, lpad, nkv)

    wa_bd = _regroup_blockdiag(l0_gate_a_w)
    wi_bd = _regroup_blockdiag(l0_gate_i_w)
    pool_mk = cache_moba_k.transpose(0, 2, 3, 1).reshape(n_phys, w, PAGE_SIZE)
    pool_mv = cache_moba_v.transpose(0, 2, 3, 1).reshape(n_phys, w, PAGE_SIZE)

    def even(x2, b, tt, h0, conv_state, attend):
        z = norm_matmul(x2, l0_norm_g, l0_w_in).reshape(b, tt, 5 * w)
        ya, h_last = rglru_branch(z, h0, conv_state, l0_conv_w, l0_conv_b, wa_bd, l0_gate_a_b,
                                  wi_bd, l0_gate_i_b, l0_lambda)
        yb, kn = attend(z)
        mix = jnp.concatenate([ya, yb], axis=-1).reshape(b * tt, 2 * w)
        out = matmul_residual(mix, l0_w_out, x2)
        new_conv = z[:, tt - (l0_conv_w.shape[0] - 1):, :w]
        v_rows = z[:, :, 4 * w:].reshape(b, tt, nmh, MOBA_HEAD_DIM)
        return out, h_last, new_conv, kn.reshape(b, tt, nmh, MOBA_HEAD_DIM), v_rows

    xp, h_p, conv_p, mk_p, mv_p = even(
        xp, bp, s, jnp.zeros((bp, w), F32), jnp.zeros((bp, l0_conv_w.shape[0] - 1, w), F32),
        lambda z: moba_prompt(z, rel_bias_table, bt, l0_q_norm_g, l0_k_norm_g, w))
    xs, h_s, conv_s, mk_s, mv_s = even(
        xs, db, t, state_rglru_h, state_rglru_conv,
        lambda z: moba_sample(z, pool_mk, pool_mv, page_table, bsamp, l0_q_norm_g, l0_k_norm_g, w))

    xp, f0_p = _conv_ffn(xp, (bp, s, d), jnp.zeros((bp, ffn0_conv_w.shape[0] - 1, f2), F32),
                         ffn0_norm_g, ffn0_w_up, ffn0_conv_w, ffn0_conv_b, ffn0_w_down)
    xs, f0_s = _conv_ffn(xs, (db, t, d), state_ffn0_conv,
                         ffn0_norm_g, ffn0_w_up, ffn0_conv_w, ffn0_conv_b, ffn0_w_down)

    wq = nq * DSA_HEAD_DIM
    wkv = nkv * DSA_HEAD_DIM
    wqi = IDX_HEADS * IDX_DIM
    n_in = l1_w_in.shape[1]
    assert n_in == wq + 2 * wkv + wqi + IDX_DIM + IDX_HEADS
    n_pad = -(-n_in // 512) * 512
    w_in1 = jnp.pad(l1_w_in, ((0, 0), (0, n_pad - n_in)))
    k_blk0 = wq // LANES
    v_blk0 = (wq + wkv) // LANES
    qi_off = wq + 2 * wkv
    assert qi_off % wqi == 0 and (qi_off + wqi) % LANES == 0 and (wq + wkv) % wkv == 0
    qi_blk = qi_off // wqi
    kw_blk = (qi_off + wqi) // LANES
    pool_dk = cache_dsa_k.reshape(n_phys * PAGE_SIZE * nkv, DSA_HEAD_DIM)
    pool_dv = cache_dsa_v.reshape(n_phys * PAGE_SIZE * nkv, DSA_HEAD_DIM)
    pool_ik = cache_dsa_idx_k.transpose(0, 2, 1)

    def odd(x2, b, tt, attend):
        z2 = norm_matmul(x2, l1_norm_g, w_in1)
        kn, knb, vb = kv_prep(z2, l1_k_norm_g, k_blk0, v_blk0, nkv)
        z = z2.reshape(b, tt, n_pad)
        y = attend(z, kn.reshape(b, tt, wkv), knb.reshape(b, tt, wkv), vb.reshape(b, tt, wkv))
        out = matmul_residual(y.reshape(b * tt, wq), l1_w_out, x2)
        v_rows = z[:, :, wq + wkv:wq + 2 * wkv].reshape(b, tt, nkv, DSA_HEAD_DIM)
        ki_rows = z[:, :, qi_off + wqi:qi_off + wqi + IDX_DIM]
        return out, kn.reshape(b, tt, nkv, DSA_HEAD_DIM), v_rows, ki_rows

    def attend_prompt(z, kn, knb, vb):
        isc = index_prompt(z, qi_blk, kw_blk)
        mask = topk_mask(isc.reshape(bp * s, s), min(DSA_TOPK, s // 4), s, 0, BF16).reshape(bp, s, s)
        return dsa_prompt(z, knb, vb, mask, rel_bias_table, bt, l1_q_norm_g, nq, nkv)

    def attend_sample(z, kn, knb, vb):
        isc = index_sample(z, pool_ik, page_table, qi_blk, kw_blk)
        mask = topk_mask(isc.reshape(db * t, lpad), min(DSA_TOPK, (past + t) // 4), t, past, F32)
        return dsa_sample(z, kn, mask.reshape(db, t, lpad), pool_dk, pool_dv, page_table, bsamp, bsx,
                          l1_q_norm_g, nq, nkv, (wq + wkv) // wkv)

    xp, dk_p, dv_p, di_p = odd(xp, bp, s, attend_prompt)
    xs, dk_s, dv_s, di_s = odd(xs, db, t, attend_sample)

    xp, f1_p = _conv_ffn(xp, (bp, s, d), jnp.zeros((bp, ffn1_conv_w.shape[0] - 1, f2), F32),
                         ffn1_norm_g, ffn1_w_up, ffn1_conv_w, ffn1_conv_b, ffn1_w_down)
    xs, f1_s = _conv_ffn(xs, (db, t, d), state_ffn1_conv,
                         ffn1_norm_g, ffn1_w_up, ffn1_conv_w, ffn1_conv_b, ffn1_w_down)

    return (xp.reshape(bp, s, d), xs.reshape(db, t, d), h_p, h_s, conv_p, conv_s, mk_p, mk_s, mv_p, mv_s,
            f0_p, f0_s, dk_p, dk_s, dv_p, dv_s, di_p, di_s, f1_p, f1_s)
```

```python
import functools
import math

import numpy as np
import jax
import jax.numpy as jnp
from jax import lax
from jax.experimental import pallas as pl
from jax.experimental.pallas import tpu as pltpu

F32 = jnp.float32
BF16 = jnp.bfloat16
I32 = jnp.int32
NEG_INF = float("-inf")
INT_MIN = -(2 ** 31)

EPS = 1e-6
LRU_C = 8.0
LRU_GROUP = 256
MOBA_BLOCK = 256
MOBA_TOPK = 3
MOBA_HEAD_DIM = 64
DSA_HEAD_DIM = 128
DSA_TOPK = 256
IDX_DIM = 64
IDX_HEADS = 8
PAGE_SIZE = 128
N_BUCKETS = 32
MAX_DISTANCE = 128
Q_TILE = 256
LANES = 128
MIB = 1024 * 1024


def _t5_thresholds():
    exact = N_BUCKETS // 2
    n = np.arange(0, 4 * MAX_DISTANCE)
    nf = np.maximum(n, 1).astype(np.float32)
    large = exact + (np.log(nf / np.float32(exact)) / np.float32(math.log(MAX_DISTANCE / exact))
                     * np.float32(N_BUCKETS - exact)).astype(np.int32)
    bucket = np.where(n < exact, n, np.minimum(large, N_BUCKETS - 1))
    assert np.all(np.diff(bucket) >= 0) and bucket[-1] == N_BUCKETS - 1
    return [int(np.argmax(bucket >= b)) for b in range(N_BUCKETS)]


T5_THR = _t5_thresholds()


def _cp(sem, vmem_mib=48):
    return pltpu.CompilerParams(dimension_semantics=sem, vmem_limit_bytes=vmem_mib * MIB)


def _iota(shape, dim):
    return lax.broadcasted_iota(I32, shape, dim)


def _dot_nt(a, b):
    return lax.dot_general(a, b, (((1,), (1,)), ((), ())), preferred_element_type=F32)


def _dot_nn(a, b):
    return jnp.dot(a, b, preferred_element_type=F32)


def _split2(x):
    hi = x.astype(BF16)
    lo = (x - hi.astype(F32)).astype(BF16)
    return hi, lo


def _split3(x):
    hi = x.astype(BF16)
    r = x - hi.astype(F32)
    mid = r.astype(BF16)
    lo = (r - mid.astype(F32)).astype(BF16)
    return hi, mid, lo


def _dot3(dot, a, b):
    ah, al = _split2(a)
    bh, bl = _split2(b)
    return dot(ah, bh) + (dot(ah, bl) + dot(al, bh))


def _dot_exact01(dot, a, b01):
    h, m, l = _split3(a)
    return dot(h, b01) + (dot(m, b01) + dot(l, b01))


def _t5_bias(dist, tab_ref, h):
    acc = jnp.full(dist.shape, tab_ref[0, h], F32)
    for b in range(1, N_BUCKETS):
        acc = jnp.where(dist >= T5_THR[b], tab_ref[b, h], acc)
    return acc


def _softplus(x):
    return jnp.maximum(x, 0.0) + jnp.log1p(jnp.exp(-jnp.abs(x)))


def _norm_mm_kernel(x_ref, g_ref, w_ref, o_ref, xn_ref):
    @pl.when(pl.program_id(1) == 0)
    def _():
        x = x_ref[...]
        ms = jnp.mean(x * x, axis=-1, keepdims=True)
        xn_ref[...] = (x * lax.rsqrt(ms + EPS) * g_ref[...]).astype(BF16)

    o_ref[...] = _dot_nn(xn_ref[...], w_ref[...].astype(BF16))


def norm_matmul(x, g, w, tn=512):
    m, d = x.shape
    n = w.shape[1]
    tm = min(m, 1024)
    assert m % tm == 0 and n % tn == 0
    return pl.pallas_call(
        _norm_mm_kernel,
        grid=(m // tm, n // tn),
        in_specs=[pl.BlockSpec((tm, d), lambda i, j: (i, 0)),
                  pl.BlockSpec((1, d), lambda i, j: (0, 0)),
                  pl.BlockSpec((d, tn), lambda i, j: (0, j))],
        out_specs=pl.BlockSpec((tm, tn), lambda i, j: (i, j)),
        out_shape=jax.ShapeDtypeStruct((m, n), F32),
        scratch_shapes=[pltpu.VMEM((tm, d), BF16)],
        compiler_params=_cp(("parallel", "arbitrary")),
        name="norm_matmul",
    )(x, g.reshape(1, d), w)


def _mm_res_kernel(a_ref, w_ref, r_ref, o_ref, acc_ref, *, nk):
    k = pl.program_id(2)

    @pl.when(k == 0)
    def _():
        acc_ref[...] = jnp.zeros_like(acc_ref)

    acc_ref[...] += _dot_nn(a_ref[...], w_ref[...].astype(BF16))

    @pl.when(k == nk - 1)
    def _():
        o_ref[...] = acc_ref[...] + r_ref[...]


def matmul_residual(a, w, res, tn=512):
    m, kdim = a.shape
    n = w.shape[1]
    tm = min(m, 1024)
    tk = kdim if kdim <= 2048 else kdim // 4
    assert m % tm == 0 and n % tn == 0 and kdim % tk == 0 and tk % LANES == 0
    nk = kdim // tk
    return pl.pallas_call(
        functools.partial(_mm_res_kernel, nk=nk),
        grid=(m // tm, n // tn, nk),
        in_specs=[pl.BlockSpec((tm, tk), lambda i, j, k: (i, k)),
                  pl.BlockSpec((tk, tn), lambda i, j, k: (k, j)),
                  pl.BlockSpec((tm, tn), lambda i, j, k: (i, j))],
        out_specs=pl.BlockSpec((tm, tn), lambda i, j, k: (i, j)),
        out_shape=jax.ShapeDtypeStruct((m, n), F32),
        scratch_shapes=[pltpu.VMEM((tm, tn), F32)],
        compiler_params=_cp(("parallel", "parallel", "arbitrary")),
        name="matmul_residual",
    )(a, w, res)


def _ffn_up_kernel(x_ref, xp_ref, g_ref, wv_ref, wg_ref, sv_ref, sg_ref, cwv_ref, cwg_ref, cbv_ref, cbg_ref,
                   o_ref, xn_ref, xnp_ref, ev_ref, eg_ref, *, nb, tc, chunks):
    i = pl.program_id(0)
    tm, tn = o_ref.shape

    def norm(x):
        return x * lax.rsqrt(jnp.mean(x * x, axis=-1, keepdims=True) + EPS) * g_ref[...]

    @pl.when(pl.program_id(1) == 0)
    def _():
        xn_ref[...] = norm(x_ref[...]).astype(BF16)
        if chunks > 1:
            xnp_ref[...] = norm(xp_ref[...])

    first = i % chunks == 0

    def half(w_ref, s_ref, cw_ref, cb_ref, e_ref):
        wb = w_ref[...].astype(BF16)
        e_ref[:, 8:8 + tc, :] = _dot_nn(xn_ref[...], wb).reshape(nb, tc, tn)
        if chunks > 1:
            up = _dot_nn(xnp_ref[...].astype(BF16), wb)
            e_ref[0, 6:8, :] = jnp.where(first, s_ref[0], up[6:8, :])
        else:
            e_ref[:, 6:8, :] = s_ref[...]
        cw = cw_ref[...]
        return (cb_ref[...] + e_ref[:, 6:6 + tc, :] * cw[0:1] + e_ref[:, 7:7 + tc, :] * cw[1:2]
                + e_ref[:, 8:8 + tc, :] * cw[2:3])

    val = half(wv_ref, sv_ref, cwv_ref, cbv_ref, ev_ref)
    gate = half(wg_ref, sg_ref, cwg_ref, cbg_ref, eg_ref)
    o_ref[...] = (jax.nn.gelu(gate) * val).reshape(tm, tn).astype(o_ref.dtype)


def ffn_up(x, seq_len, state, norm_g, w_up, conv_w, conv_b, tn=512):
    m, d = x.shape
    f2 = w_up.shape[1]
    f = f2 // 2
    tm = min(m, 1024)
    tc = min(seq_len, tm)
    nb = tm // tc
    chunks = seq_len // tc
    assert m % tm == 0 and tm % tc == 0 and seq_len % tc == 0 and tc % 8 == 0 and (nb == 1 or chunks == 1)
    assert f % tn == 0 and conv_w.shape[0] == 3
    ncb = f // tn
    r8 = tm // 8

    def wspec(rows, off):
        return pl.BlockSpec((rows, tn), lambda i, j: (0, j + off))

    def st(off):
        return pl.BlockSpec((nb, 2, tn), lambda i, j: (i // chunks, 0, j + off))

    return pl.pallas_call(
        functools.partial(_ffn_up_kernel, nb=nb, tc=tc, chunks=chunks),
        grid=(m // tm, ncb),
        in_specs=[pl.BlockSpec((tm, d), lambda i, j: (i, 0)),
                  pl.BlockSpec((8, d), lambda i, j: (jnp.maximum(i * r8 - 1, 0), 0)),
                  pl.BlockSpec((1, d), lambda i, j: (0, 0)),
                  wspec(d, 0), wspec(d, ncb), st(0), st(ncb),
                  wspec(3, 0), wspec(3, ncb), wspec(1, 0), wspec(1, ncb)],
        out_specs=pl.BlockSpec((tm, tn), lambda i, j: (i, j)),
        out_shape=jax.ShapeDtypeStruct((m, f), BF16),
        scratch_shapes=[pltpu.VMEM((tm, d), BF16), pltpu.VMEM((8, d), F32),
                        pltpu.VMEM((nb, tc + 8, tn), F32), pltpu.VMEM((nb, tc + 8, tn), F32)],
        compiler_params=_cp(("parallel", "arbitrary"), vmem_mib=56),
        name="ffn_up",
    )(x, x, norm_g.reshape(1, d), w_up, w_up, state, state, conv_w, conv_w,
      conv_b.reshape(1, f2), conv_b.reshape(1, f2))


def _rglru_kernel(xa_ref, ga_ref, pv_ref, cs_ref, h0_ref, cw_ref, cb_ref, wa_ref, ba_ref,
                  wi_ref, bi_ref, lam_ref, ya_ref, hl_ref, e_ref, hc_ref, *, tc):
    t = pl.program_id(1)
    w = xa_ref.shape[-1]

    @pl.when(t == 0)
    def _():
        hc_ref[...] = h0_ref[0]
        e_ref[5:8, :] = cs_ref[0]

    @pl.when(t > 0)
    def _():
        e_ref[5:8, :] = pv_ref[0, 5:8, :]

    e_ref[8:8 + tc, :] = xa_ref[0]
    cw = cw_ref[...]
    xc = (cb_ref[...] + e_ref[5:5 + tc, :] * cw[0:1] + e_ref[6:6 + tc, :] * cw[1:2]
          + e_ref[7:7 + tc, :] * cw[2:3] + e_ref[8:8 + tc, :] * cw[3:4])
    xb = xc.astype(BF16)

    def gate(w_ref, b_ref):
        parts = [_dot_nn(xb[:, g * LRU_GROUP:(g + 1) * LRU_GROUP], w_ref[g])
                 for g in range(w // LRU_GROUP)]
        return jax.nn.sigmoid(jnp.concatenate(parts, axis=1) + b_ref[...])

    r = gate(wa_ref, ba_ref)
    i = gate(wi_ref, bi_ref)
    log_a = -LRU_C * r * _softplus(-lam_ref[...])
    a = jnp.exp(log_a)
    u = jnp.sqrt(-jnp.tanh(log_a) * (1.0 + a * a)) * (i * xc)

    row = _iota((tc, w), 0)
    s = 1
    while s < tc:
        a_sh = jnp.where(row >= s, pltpu.roll(a, s, 0), 1.0)
        u_sh = jnp.where(row >= s, pltpu.roll(u, s, 0), 0.0)
        u = u + a * u_sh
        a = a * a_sh
        s *= 2
    h = u + a * hc_ref[...]
    hlast = h[tc - 1:tc, :]
    hc_ref[...] = hlast
    hl_ref[0] = hlast
    ya_ref[0] = (h * jax.nn.gelu(ga_ref[0])).astype(ya_ref.dtype)


def rglru_branch(z, h0, conv_state, conv_w, conv_b, wa_bd, ba, wi_bd, bi, lam):
    b, t, _ = z.shape
    w = h0.shape[-1]
    tc = min(t, 256)
    assert t % tc == 0 and tc % 8 == 0 and conv_w.shape[0] == 4 and w % LRU_GROUP == 0
    r8 = tc // 8
    ng = w // LRU_GROUP
    vec = lambda: pl.BlockSpec((1, w), lambda i, j: (0, 0))
    ya, hl = pl.pallas_call(
        functools.partial(_rglru_kernel, tc=tc),
        grid=(b, t // tc),
        in_specs=[pl.BlockSpec((1, tc, w), lambda i, j: (i, j, 0)),
                  pl.BlockSpec((1, tc, w), lambda i, j: (i, j, 1)),
                  pl.BlockSpec((1, 8, w), lambda i, j: (i, jnp.maximum(j * r8 - 1, 0), 0)),
                  pl.BlockSpec((1, 3, w), lambda i, j: (i, 0, 0)),
                  pl.BlockSpec((1, 1, w), lambda i, j: (i, 0, 0)),
                  pl.BlockSpec((4, w), lambda i, j: (0, 0)),
                  vec(),
                  pl.BlockSpec((ng, LRU_GROUP, LRU_GROUP), lambda i, j: (0, 0, 0)),
                  vec(),
                  pl.BlockSpec((ng, LRU_GROUP, LRU_GROUP), lambda i, j: (0, 0, 0)),
                  vec(), vec()],
        out_specs=[pl.BlockSpec((1, tc, w), lambda i, j: (i, j, 0)),
                   pl.BlockSpec((1, 1, w), lambda i, j: (i, 0, 0))],
        out_shape=[jax.ShapeDtypeStruct((b, t, w), BF16), jax.ShapeDtypeStruct((b, 1, w), F32)],
        scratch_shapes=[pltpu.VMEM((tc + 8, w), F32), pltpu.VMEM((1, w), F32)],
        compiler_params=_cp(("parallel", "arbitrary")),
        name="rglru",
    )(z, z, z, conv_state, h0.reshape(b, 1, w), conv_w, conv_b.reshape(1, w), wa_bd,
      ba.reshape(1, w), wi_bd, bi.reshape(1, w), lam.reshape(1, w))
    return ya, hl.reshape(b, w)


def _regroup_blockdiag(wb):
    n, c, _ = wb.shape
    per = LRU_GROUP // c
    g = n // per
    eye = jnp.eye(per, dtype=wb.dtype)
    out = jnp.einsum("gacd,ab->gacbd", wb.reshape(g, per, c, c), eye)
    return out.reshape(g, LRU_GROUP, LRU_GROUP).astype(BF16)


def _bias_kernel(tab_ref, bt_ref, bs_ref, bx_ref, *, past, lpad, nkv, grp):
    h = pl.program_id(0)
    d0 = _iota((Q_TILE, Q_TILE), 0) - _iota((Q_TILE, Q_TILE), 1)
    bt_ref[0, 0] = _t5_bias(d0, tab_ref, h)
    bt_ref[0, 1] = _t5_bias(d0 + Q_TILE, tab_ref, h)
    t = bs_ref.shape[1]
    ds = past + _iota((t, lpad), 0) - _iota((t, lpad), 1)
    bs_ref[0] = _t5_bias(ds, tab_ref, h)
    col = _iota((t, past * nkv), 1)
    dx = past + _iota((t, past * nkv), 0) - col // nkv
    bx_ref[0] = jnp.where(col % nkv == h // grp, _t5_bias(dx, tab_ref, h), NEG_INF)


def bias_tiles(table, past, t_new, lpad, nkv):
    nh = table.shape[1]
    return pl.pallas_call(
        functools.partial(_bias_kernel, past=past, lpad=lpad, nkv=nkv, grp=nh // nkv),
        grid=(nh,),
        in_specs=[pl.BlockSpec(memory_space=pltpu.SMEM)],
        out_specs=[pl.BlockSpec((1, 2, Q_TILE, Q_TILE), lambda h: (h, 0, 0, 0)),
                   pl.BlockSpec((1, t_new, lpad), lambda h: (h, 0, 0)),
                   pl.BlockSpec((1, t_new, past * nkv), lambda h: (h, 0, 0))],
        out_shape=[jax.ShapeDtypeStruct((nh, 2, Q_TILE, Q_TILE), F32),
                   jax.ShapeDtypeStruct((nh, t_new, lpad), F32),
                   jax.ShapeDtypeStruct((nh, t_new, past * nkv), F32)],
        compiler_params=_cp(("arbitrary",)),
        name="t5_bias_tiles",
    )(table)


def _top_lanes(g, nvalid, k):
    lane = _iota(g.shape, 1).astype(F32)
    gg = jnp.where(lane < nvalid, g, NEG_INF)
    sel = jnp.zeros(g.shape, F32)
    for _ in range(k):
        mx = jnp.max(gg, axis=-1, keepdims=True)
        first = jnp.min(jnp.where(gg == mx, lane, float(g.shape[1])), axis=-1, keepdims=True)
        pick = (lane == first) & (mx > NEG_INF)
        sel = jnp.where(pick, 1.0, sel)
        gg = jnp.where(pick, NEG_INF, gg)
    return sel


def _moba_prompt_kernel(tab_ref, q_ref, k_ref, v_ref, qg_ref, kg_ref, bt_ref, y_ref, kn_ref, *, s):
    hp = pl.program_id(0)
    nb = s // MOBA_BLOCK
    dh = MOBA_HEAD_DIM
    scale = dh ** -0.5
    lane = _iota((1, LANES), 1)
    low = lane < dh

    def headnorm(x, g):
        x2 = x * x
        s0 = jnp.sum(jnp.where(low, x2, 0.0), axis=-1, keepdims=True)
        s1 = jnp.sum(jnp.where(low, 0.0, x2), axis=-1, keepdims=True)
        ms = jnp.where(low, s0, s1) * (1.0 / dh)
        return x * lax.rsqrt(ms + EPS) * g

    qn = headnorm(q_ref[0], qg_ref[...])
    kn = headnorm(k_ref[0], kg_ref[...])
    kn_ref[0] = kn
    kb = kn.astype(BF16)
    vb = v_ref[0].astype(BF16)
    blk_row = _iota((LANES, LANES), 0)
    kmat = jnp.zeros((LANES, LANES), F32)
    for j in range(nb):
        kmean = jnp.mean(kn[j * MOBA_BLOCK:(j + 1) * MOBA_BLOCK], axis=0, keepdims=True)
        kmat = jnp.where(blk_row == j, kmean, kmat)
    qi = _iota((Q_TILE, Q_TILE), 0)
    kj = _iota((Q_TILE, Q_TILE), 1)
    causal = qi >= kj

    near = []
    for hh in range(2):
        t31 = tab_ref[N_BUCKETS - 1, 2 * hp + hh]
        near.append((jnp.where(causal, bt_ref[hh, 0] - t31, NEG_INF), bt_ref[hh, 1] - t31))

    for c in range(nb):
        qc = qn[c * Q_TILE:(c + 1) * Q_TILE]
        nk = (c + 1) * MOBA_BLOCK
        outs = []
        for hh in range(2):
            hmask = low if hh == 0 else jnp.logical_not(low)
            qm = jnp.where(hmask, qc, 0.0)
            sc = _dot_nt((qm * scale).astype(BF16), kb[0:nk])
            if c > MOBA_TOPK:
                sel = _top_lanes(_dot3(_dot_nt, qm, kmat), c, MOBA_TOPK)
            pieces = []
            for j in range(c + 1):
                sj = sc[:, j * MOBA_BLOCK:(j + 1) * MOBA_BLOCK]
                if j == c:
                    sj = sj + near[hh][0]
                else:
                    if j == c - 1:
                        sj = sj + near[hh][1]
                    if c > MOBA_TOPK:
                        sj = jnp.where(sel[:, j:j + 1] > 0.0, sj, NEG_INF)
                pieces.append(sj)
            srow = jnp.concatenate(pieces, axis=1) if c else pieces[0]
            mx = jnp.max(srow, axis=-1, keepdims=True)
            p = jnp.exp(srow - mx)
            l = jnp.sum(p, axis=-1, keepdims=True)
            outs.append(_dot_nn(p.astype(BF16), vb[0:nk]) / l)
        y_ref[0, c * Q_TILE:(c + 1) * Q_TILE, :] = jnp.where(low, outs[0], outs[1]).astype(y_ref.dtype)


def moba_prompt(z, table, bt, q_gain, k_gain, w):
    b, s, _ = z.shape
    assert s % MOBA_BLOCK == 0 and Q_TILE == MOBA_BLOCK and s // MOBA_BLOCK <= LANES
    nhp = w // LANES
    g2 = lambda g: jnp.tile(g, LANES // MOBA_HEAD_DIM).reshape(1, LANES)
    col = lambda off: pl.BlockSpec((1, s, LANES), lambda hp, i: (i, 0, off * nhp + hp))
    gspec = pl.BlockSpec((1, LANES), lambda hp, i: (0, 0))
    return pl.pallas_call(
        functools.partial(_moba_prompt_kernel, s=s),
        grid=(nhp, b),
        in_specs=[pl.BlockSpec(memory_space=pltpu.SMEM), col(2), col(3), col(4), gspec, gspec,
                  pl.BlockSpec((2, 2, Q_TILE, Q_TILE), lambda hp, i: (hp, 0, 0, 0))],
        out_specs=[pl.BlockSpec((1, s, LANES), lambda hp, i: (i, 0, hp)),
                   pl.BlockSpec((1, s, LANES), lambda hp, i: (i, 0, hp))],
        out_shape=[jax.ShapeDtypeStruct((b, s, w), BF16), jax.ShapeDtypeStruct((b, s, w), F32)],
        compiler_params=_cp(("parallel", "parallel")),
        name="moba_prompt",
    )(table, z, z, z, g2(q_gain), g2(k_gain), bt)


def _group_onehot(ngroups_pad, width, gsize):
    return jnp.where(_iota((ngroups_pad, width), 0) == _iota((ngroups_pad, width), 1) // gsize,
                     1.0, 0.0).astype(BF16)


def _group_rmsnorm(x, gain, gsize):
    onehot = _group_onehot(LANES, x.shape[1], gsize)
    ssq = _dot_exact01(_dot_nt, x * x, onehot)
    rinv = lax.rsqrt(ssq * (1.0 / gsize) + EPS)
    return x * _dot_exact01(_dot_nn, rinv, onehot) * gain


def _moba_sample_kernel(pt_ref, q_ref, k_ref, v_ref, qg_ref, kg_ref, bs_ref, *rest, npages, t):
    kp_refs = rest[:npages]
    vp_refs = rest[npages:2 * npages]
    y_ref, kn_ref = rest[2 * npages:]
    w = q_ref.shape[-1]
    dh = MOBA_HEAD_DIM
    nh = w // dh
    rows = nh * t
    ppb = MOBA_BLOCK // PAGE_SIZE
    nbp = npages // ppb
    past = npages * PAGE_SIZE
    scale = dh ** -0.5

    qn = _group_rmsnorm(q_ref[0], qg_ref[...], dh)
    kn = _group_rmsnorm(k_ref[0], kg_ref[...], dh)
    kn_ref[0] = kn
    own = _iota((rows, w), 0) // t == _iota((rows, w), 1) // dh
    qbd = jnp.where(own, jnp.concatenate([qn] * nh, axis=0), 0.0)
    qbb = qbd.astype(BF16)
    pad_rows = lambda x: jnp.concatenate([x, jnp.zeros((LANES - t, w), x.dtype)], axis=0)

    pieces, sums = [], []
    for p in range(npages):
        kpt = kp_refs[p][0]
        sums.append(jnp.sum(kpt, axis=1, keepdims=True))
        pieces.append(_dot_nn(qbb, kpt.astype(BF16)))
    pieces.append(_dot_nt(qbb, pad_rows(kn).astype(BF16)))
    lpad = past + LANES
    srow = jnp.concatenate(pieces, axis=1) * scale + bs_ref[...]

    blk_lane = _iota((w, LANES), 1)
    kmat = jnp.zeros((w, LANES), F32)
    for j in range(nbp):
        kmat = jnp.where(blk_lane == j, sum(sums[j * ppb:(j + 1) * ppb]) * (1.0 / MOBA_BLOCK), kmat)
    sel = _top_lanes(_dot3(_dot_nn, qbd, kmat), nbp, MOBA_TOPK)
    col = _iota((rows, lpad), 1)
    jrow = _iota((rows, lpad), 0) % t
    keep = (col >= past) & (col - past <= jrow)
    for j in range(nbp):
        inblk = (col >= j * MOBA_BLOCK) & (col < (j + 1) * MOBA_BLOCK)
        keep = keep | (inblk & (sel[:, j:j + 1] > 0.0))
    srow = jnp.where(keep, srow, NEG_INF)
    mx = jnp.max(srow, axis=-1, keepdims=True)
    pr = jnp.exp(srow - mx)
    l = jnp.sum(pr, axis=-1, keepdims=True)
    pb = pr.astype(BF16)
    acc = _dot_nn(pb[:, past:], pad_rows(v_ref[0]).astype(BF16))
    for p in range(npages):
        acc = acc + _dot_nt(pb[:, p * PAGE_SIZE:(p + 1) * PAGE_SIZE], vp_refs[p][0].astype(BF16))
    acc = acc / l
    lane_head = _iota((t, w), 1) // dh
    y = jnp.zeros((t, w), F32)
    for h in range(nh):
        y = jnp.where(lane_head == h, acc[h * t:(h + 1) * t, :], y)
    y_ref[0] = y.astype(y_ref.dtype)


def moba_sample(z, pool_k, pool_v, page_table, bs, q_gain, k_gain, w):
    db, t, _ = z.shape
    npages = page_table.shape[1]
    nh = w // MOBA_HEAD_DIM
    assert t == 8 and nh * t == LANES and npages % (MOBA_BLOCK // PAGE_SIZE) == 0
    lpad = npages * PAGE_SIZE + LANES
    gt = lambda g: jnp.tile(g, nh).reshape(1, w)
    col = lambda off: pl.BlockSpec((1, t, w), lambda i, pt: (i, 0, off))
    gspec = pl.BlockSpec((1, w), lambda i, pt: (0, 0))
    page = lambda p: pl.BlockSpec((1, w, PAGE_SIZE), lambda i, pt: (pt[i, p], 0, 0))
    grid_spec = pltpu.PrefetchScalarGridSpec(
        num_scalar_prefetch=1,
        grid=(db,),
        in_specs=[col(2), col(3), col(4), gspec, gspec,
                  pl.BlockSpec((nh * t, lpad), lambda i, pt: (0, 0))]
                 + [page(p) for p in range(npages)] * 2,
        out_specs=[pl.BlockSpec((1, t, w), lambda i, pt: (i, 0, 0)),
                   pl.BlockSpec((1, t, w), lambda i, pt: (i, 0, 0))],
    )
    return pl.pallas_call(
        functools.partial(_moba_sample_kernel, npages=npages, t=t),
        grid_spec=grid_spec,
        out_shape=[jax.ShapeDtypeStruct((db, t, w), BF16), jax.ShapeDtypeStruct((db, t, w), F32)],
        compiler_params=_cp(("arbitrary",), vmem_mib=56),
        name="moba_sample",
    )(page_table, z, z, z, gt(q_gain), gt(k_gain), bs.reshape(nh * t, lpad),
      *([pool_k] * npages), *([pool_v] * npages))


def _kv_prep_kernel(k_ref, v_ref, g_ref, kn_ref, knb_ref, vb_ref):
    x = k_ref[...]
    ms = jnp.mean(x * x, axis=-1, keepdims=True)
    kn = x * lax.rsqrt(ms + EPS) * g_ref[...]
    kn_ref[...] = kn
    knb_ref[...] = kn.astype(BF16)
    vb_ref[...] = v_ref[...].astype(BF16)


def kv_prep(z, gain, k_blk0, v_blk0, nheads):
    m = z.shape[0]
    tm = min(m, 1024)
    assert m % tm == 0
    out = pl.BlockSpec((tm, LANES), lambda i, h: (i, h))
    return pl.pallas_call(
        _kv_prep_kernel,
        grid=(m // tm, nheads),
        in_specs=[pl.BlockSpec((tm, LANES), lambda i, h: (i, k_blk0 + h)),
                  pl.BlockSpec((tm, LANES), lambda i, h: (i, v_blk0 + h)),
                  pl.BlockSpec((1, LANES), lambda i, h: (0, 0))],
        out_specs=[out, out, out],
        out_shape=[jax.ShapeDtypeStruct((m, nheads * LANES), F32),
                   jax.ShapeDtypeStruct((m, nheads * LANES), BF16),
                   jax.ShapeDtypeStruct((m, nheads * LANES), BF16)],
        compiler_params=_cp(("parallel", "parallel")),
        name="kv_prep",
    )(z, z, gain.reshape(1, LANES))


def _index_prompt_kernel(qi_ref, qw_ref, kk_ref, o_ref, *, s):
    c = pl.program_id(1)
    lane = _iota((1, LANES), 1)
    low = lane < IDX_DIM
    kk = jnp.where(low, kk_ref[0], 0.0)
    kk_hi = pltpu.roll(kk, IDX_DIM, 1)
    qw = qw_ref[0]
    acc = jnp.zeros((Q_TILE, s), F32)
    for ih in range(IDX_HEADS):
        blk = qi_ref[0][:, (ih // 2) * LANES:(ih // 2 + 1) * LANES]
        half_low = ih % 2 == 0
        qm = jnp.where(low if half_low else jnp.logical_not(low), blk, 0.0)
        sc = _dot3(_dot_nt, qm, kk if half_low else kk_hi)
        wcol = qw[:, IDX_DIM + ih:IDX_DIM + ih + 1] * (IDX_HEADS ** -0.5)
        acc = acc + wcol * jnp.maximum(sc * (IDX_DIM ** -0.5), 0.0)
    qpos = c * Q_TILE + _iota((Q_TILE, s), 0)
    o_ref[0] = jnp.where(_iota((Q_TILE, s), 1) <= qpos, acc, NEG_INF)


def index_prompt(z, qi_blk, kw_blk):
    b, s, _ = z.shape
    assert s % Q_TILE == 0
    return pl.pallas_call(
        functools.partial(_index_prompt_kernel, s=s),
        grid=(b, s // Q_TILE),
        in_specs=[pl.BlockSpec((1, Q_TILE, IDX_HEADS * IDX_DIM), lambda i, c: (i, c, qi_blk)),
                  pl.BlockSpec((1, Q_TILE, LANES), lambda i, c: (i, c, kw_blk)),
                  pl.BlockSpec((1, s, LANES), lambda i, c: (i, 0, kw_blk))],
        out_specs=pl.BlockSpec((1, Q_TILE, s), lambda i, c: (i, c, 0)),
        out_shape=jax.ShapeDtypeStruct((b, s, s), F32),
        compiler_params=_cp(("parallel", "parallel")),
        name="index_prompt",
    )(z, z, z)


def _index_sample_kernel(pt_ref, qi_ref, qw_ref, *rest, npages, t):
    kp_refs = rest[:npages]
    o_ref = rest[npages]
    qi = qi_ref[0]
    qw = qw_ref[0]
    q2 = jnp.concatenate([qi[:, ih * IDX_DIM:(ih + 1) * IDX_DIM] for ih in range(IDX_HEADS)], axis=0)
    wcol = jnp.concatenate([qw[:, IDX_DIM + ih:IDX_DIM + ih + 1] for ih in range(IDX_HEADS)], axis=0)
    wcol = wcol * (IDX_HEADS ** -0.5)
    knew = jnp.concatenate([qw[:, 0:IDX_DIM], jnp.zeros((LANES - t, IDX_DIM), F32)], axis=0)

    def scores(dot, keys):
        sc = wcol * jnp.maximum(_dot3(dot, q2, keys) * (IDX_DIM ** -0.5), 0.0)
        return sum(sc[ih * t:(ih + 1) * t, :] for ih in range(IDX_HEADS))

    pieces = [scores(_dot_nn, kp_refs[p][0]) for p in range(npages)]
    new = scores(_dot_nt, knew)
    jj = _iota((t, LANES), 0)
    cc = _iota((t, LANES), 1)
    pieces.append(jnp.where(cc <= jj, new, NEG_INF))
    o_ref[0] = jnp.concatenate(pieces, axis=1)


def index_sample(z, pool_ik, page_table, qi_blk, kw_blk):
    db, t, _ = z.shape
    npages = page_table.shape[1]
    lpad = npages * PAGE_SIZE + LANES
    grid_spec = pltpu.PrefetchScalarGridSpec(
        num_scalar_prefetch=1,
        grid=(db,),
        in_specs=[pl.BlockSpec((1, t, IDX_HEADS * IDX_DIM), lambda i, pt: (i, 0, qi_blk)),
                  pl.BlockSpec((1, t, LANES), lambda i, pt: (i, 0, kw_blk))]
                 + [pl.BlockSpec((1, IDX_DIM, PAGE_SIZE), (lambda i, pt, p=p: (pt[i, p], 0, 0)))
                    for p in range(npages)],
        out_specs=pl.BlockSpec((1, t, lpad), lambda i, pt: (i, 0, 0)),
    )
    return pl.pallas_call(
        functools.partial(_index_sample_kernel, npages=npages, t=t),
        grid_spec=grid_spec,
        out_shape=jax.ShapeDtypeStruct((db, t, lpad), F32),
        compiler_params=_cp(("arbitrary",)),
        name="index_sample",
    )(page_table, z, z, *([pool_ik] * npages))


def _topk_mask_kernel(x_ref, o_ref, *, topk, period, base, tr, additive):
    l = x_ref.shape[1]
    rowg = pl.program_id(0) * tr + _iota((tr, l), 0)
    valid = _iota((tr, l), 1) <= base + rowg % period
    bits = pltpu.bitcast(x_ref[...], I32)
    key = jnp.where(bits < 0, bits ^ 0x7FFFFFFF, bits)
    key = jnp.where(valid, key, INT_MIN)

    def step(i, lo):
        cand = lo + lax.shift_left(jnp.int32(1), 31 - i)
        cnt = jnp.sum(jnp.where(key >= cand, 1.0, 0.0), axis=-1, keepdims=True)
        return jnp.where(cnt >= topk, cand, lo)

    thr = lax.fori_loop(0, 32, step, jnp.full((tr, 1), INT_MIN, I32))
    gt = key > thr
    need = topk - jnp.sum(jnp.where(gt, 1.0, 0.0), axis=-1, keepdims=True)
    eq = jnp.where(key == thr, 1.0, 0.0)
    upper = jnp.where(_iota((LANES, LANES), 0) < _iota((LANES, LANES), 1), 1.0, 0.0).astype(BF16)
    before = jnp.zeros((tr, 1), F32)
    for j in range(l // LANES):
        e = eq[:, j * LANES:(j + 1) * LANES]
        rank = before + _dot_nn(e.astype(BF16), upper)
        pick = gt[:, j * LANES:(j + 1) * LANES] | ((e > 0.0) & (rank < need))
        pick = pick & valid[:, j * LANES:(j + 1) * LANES]
        on, off = (0.0, NEG_INF) if additive else (1.0, 0.0)
        o_ref[:, j * LANES:(j + 1) * LANES] = jnp.where(pick, on, off)
        before = before + jnp.sum(e, axis=-1, keepdims=True)


def topk_mask(scores, topk, period, base, additive):
    r, l = scores.shape
    tr = min(r, 256)
    assert r % tr == 0 and l % LANES == 0
    return pl.pallas_call(
        functools.partial(_topk_mask_kernel, topk=topk, period=period, base=base, tr=tr, additive=additive),
        grid=(r // tr,),
        in_specs=[pl.BlockSpec((tr, l), lambda i: (i, 0))],
        out_specs=pl.BlockSpec((tr, l), lambda i: (i, 0)),
        out_shape=jax.ShapeDtypeStruct((r, l), F32),
        compiler_params=_cp(("parallel",)),
        name="topk_mask",
    )(scores)


def _dsa_prompt_kernel(tab_ref, q_ref, k_ref, v_ref, m_ref, bt_ref, qg_ref, y_ref, *, s):
    c = pl.program_id(1)
    g = pl.program_id(2)
    dh = DSA_HEAD_DIM
    grp = q_ref.shape[-1] // dh
    qbs, t31s = [], []
    for hh in range(grp):
        q = q_ref[0, :, hh * dh:(hh + 1) * dh]
        qn = q * lax.rsqrt(jnp.mean(q * q, axis=-1, keepdims=True) + EPS) * qg_ref[...]
        qbs.append((qn * (dh ** -0.5)).astype(BF16))
        t31s.append(tab_ref[N_BUCKETS - 1, g * grp + hh])

    def tile(cc):
        nk = (cc + 1) * Q_TILE
        madd = m_ref[0, :, 0:nk]
        for hh in range(grp):
            sc = _dot_nt(qbs[hh], k_ref[0, 0:nk, :]) + madd
            pieces = [sc[:, 0:(cc - 1) * Q_TILE]] if cc > 1 else []
            if cc > 0:
                pieces.append(sc[:, (cc - 1) * Q_TILE:cc * Q_TILE] + (bt_ref[hh, 1] - t31s[hh]))
            pieces.append(sc[:, cc * Q_TILE:nk] + (bt_ref[hh, 0] - t31s[hh]))
            srow = jnp.concatenate(pieces, axis=1) if cc else pieces[0]
            mx = jnp.max(srow, axis=-1, keepdims=True)
            p = jnp.exp(srow - mx)
            l = jnp.sum(p, axis=-1, keepdims=True)
            y_ref[0, :, hh * dh:(hh + 1) * dh] = (_dot_nn(p.astype(BF16), v_ref[0, 0:nk, :]) / l
                                                  ).astype(y_ref.dtype)

    for cc in range(s // Q_TILE):
        pl.when(c == cc)(functools.partial(tile, cc))


def dsa_prompt(z, knb, vb, madd, table, bt, q_gain, nq, nkv):
    b, s, _ = z.shape
    grp = nq // nkv
    return pl.pallas_call(
        functools.partial(_dsa_prompt_kernel, s=s),
        grid=(b, s // Q_TILE, nkv),
        in_specs=[pl.BlockSpec(memory_space=pltpu.SMEM),
                  pl.BlockSpec((1, Q_TILE, grp * LANES), lambda i, c, g: (i, c, g)),
                  pl.BlockSpec((1, s, LANES), lambda i, c, g: (i, 0, g)),
                  pl.BlockSpec((1, s, LANES), lambda i, c, g: (i, 0, g)),
                  pl.BlockSpec((1, Q_TILE, s), lambda i, c, g: (i, c, 0)),
                  pl.BlockSpec((grp, 2, Q_TILE, Q_TILE), lambda i, c, g: (g, 0, 0, 0)),
                  pl.BlockSpec((1, LANES), lambda i, c, g: (0, 0))],
        out_specs=pl.BlockSpec((1, Q_TILE, grp * LANES), lambda i, c, g: (i, c, g)),
        out_shape=jax.ShapeDtypeStruct((b, s, nq * LANES), BF16),
        compiler_params=_cp(("parallel", "parallel", "arbitrary")),
        name="dsa_prompt",
    )(table, z, knb, vb, madd, bt, q_gain.reshape(1, LANES))


def _dsa_sample_kernel(pt_ref, q_ref, kn_ref, v_ref, m_ref, bs_ref, bx_ref, qg_ref, *rest,
                       npages, t, nq, nkv):
    kp_refs = rest[:npages]
    vp_refs = rest[npages:2 * npages]
    y_ref = rest[2 * npages]
    dh = DSA_HEAD_DIM
    grp = nq // nkv
    wkv = nkv * dh
    past = npages * PAGE_SIZE
    pcols = PAGE_SIZE * nkv
    scale = dh ** -0.5
    q = q_ref[0]
    g = qg_ref[...]
    zero = jnp.zeros((t, dh), F32)
    qrows, qstruct = [], []
    for h in range(nq):
        qh = q[:, h * dh:(h + 1) * dh]
        qh = qh * lax.rsqrt(jnp.mean(qh * qh, axis=-1, keepdims=True) + EPS) * g
        qrows.append(qh)
        qstruct.append(jnp.concatenate([qh if kv == h // grp else zero for kv in range(nkv)], axis=1))
    qall = jnp.concatenate(qrows, axis=0).astype(BF16)
    qsb = jnp.concatenate(qstruct, axis=0).astype(BF16)
    pad_rows = lambda x: jnp.concatenate([x, jnp.zeros((LANES - t, wkv), x.dtype)], axis=0)
    keep = jnp.concatenate([m_ref[0]] * nq, axis=0)
    keepb = keep.astype(BF16)
    spread = jnp.where(_iota((PAGE_SIZE, pcols), 0) == _iota((PAGE_SIZE, pcols), 1) // nkv,
                       1.0, 0.0).astype(BF16)

    pieces = []
    for p in range(npages):
        sp = _dot_nt(qall, kp_refs[p][...].astype(BF16)) * scale + bx_ref[:, p * pcols:(p + 1) * pcols]
        kexp = _dot_nn(keepb[:, p * PAGE_SIZE:(p + 1) * PAGE_SIZE], spread)
        pieces.append(jnp.where(kexp > 0.5, sp, NEG_INF))
    loc = _dot_nt(qsb, pad_rows(kn_ref[0]).astype(BF16)) * scale + bs_ref[:, past:]
    pieces.append(jnp.where(keep[:, past:] > 0.5, loc, NEG_INF))
    srow = jnp.concatenate(pieces, axis=1)
    mx = jnp.max(srow, axis=-1, keepdims=True)
    pr = jnp.exp(srow - mx)
    l = jnp.sum(pr, axis=-1, keepdims=True)
    pb = pr.astype(BF16)
    accl = _dot_nn(pb[:, npages * pcols:], pad_rows(v_ref[0]).astype(BF16))
    row_grp = _iota((nq * t, dh), 0) // (t * grp)
    acc = jnp.zeros((nq * t, dh), F32)
    for kv in range(nkv):
        acc = jnp.where(row_grp == kv, accl[:, kv * dh:(kv + 1) * dh], acc)
    for p in range(npages):
        acc = acc + _dot_nn(pb[:, p * pcols:(p + 1) * pcols], vp_refs[p][...].astype(BF16))
    acc = acc / l
    y_ref[0] = jnp.concatenate([acc[h * t:(h + 1) * t, :] for h in range(nq)], axis=1).astype(y_ref.dtype)


def dsa_sample(z, kn, mask, pool_k, pool_v, page_table, bs, bx, q_gain, nq, nkv, v_blk):
    db, t, _ = z.shape
    npages = page_table.shape[1]
    wkv = nkv * DSA_HEAD_DIM
    past = npages * PAGE_SIZE
    lpad = past + LANES
    assert nq * t == LANES
    page = lambda p: pl.BlockSpec((PAGE_SIZE * nkv, DSA_HEAD_DIM), lambda i, pt: (pt[i, p], 0))
    grid_spec = pltpu.PrefetchScalarGridSpec(
        num_scalar_prefetch=1,
        grid=(db,),
        in_specs=[pl.BlockSpec((1, t, nq * DSA_HEAD_DIM), lambda i, pt: (i, 0, 0)),
                  pl.BlockSpec((1, t, wkv), lambda i, pt: (i, 0, 0)),
                  pl.BlockSpec((1, t, wkv), lambda i, pt: (i, 0, v_blk)),
                  pl.BlockSpec((1, t, lpad), lambda i, pt: (i, 0, 0)),
                  pl.BlockSpec((nq * t, lpad), lambda i, pt: (0, 0)),
                  pl.BlockSpec((nq * t, past * nkv), lambda i, pt: (0, 0)),
                  pl.BlockSpec((1, LANES), lambda i, pt: (0, 0))]
                 + [page(p) for p in range(npages)] * 2,
        out_specs=pl.BlockSpec((1, t, nq * DSA_HEAD_DIM), lambda i, pt: (i, 0, 0)),
    )
    return pl.pallas_call(
        functools.partial(_dsa_sample_kernel, npages=npages, t=t, nq=nq, nkv=nkv),
        grid_spec=grid_spec,
        out_shape=jax.ShapeDtypeStruct((db, t, nq * DSA_HEAD_DIM), BF16),
        compiler_params=_cp(("arbitrary",)),
        name="dsa_sample",
    )(page_table, z, kn, z, mask, bs.reshape(nq * t, lpad), bx.reshape(nq * t, past * nkv),
      q_gain.reshape(1, LANES), *([pool_k] * npages), *([pool_v] * npages))


def _conv_ffn(x2, shape3, state, norm_g, w_up, conv_w, conv_b, w_down):
    b, t, d = shape3
    keep = conv_w.shape[0] - 1
    mid = ffn_up(x2, t, state, norm_g, w_up, conv_w, conv_b)
    y = matmul_residual(mid, w_down, x2)
    tail = x2.reshape(b, t, d)[:, t - keep:, :].reshape(b * keep, d)
    return y, norm_matmul(tail, norm_g, w_up).reshape(b, keep, -1)


def kernel(x_prompt, x_sample, state_rglru_h, state_rglru_conv, cache_moba_k, cache_moba_v, state_ffn0_conv, cache_dsa_k, cache_dsa_v, cache_dsa_idx_k, state_ffn1_conv, page_table, rel_bias_table, l0_norm_g, l0_w_in, l0_conv_w, l0_conv_b, l0_gate_a_w, l0_gate_a_b, l0_gate_i_w, l0_gate_i_b, l0_lambda, l0_q_norm_g, l0_k_norm_g, l0_w_out, ffn0_norm_g, ffn0_w_up, ffn0_conv_w, ffn0_conv_b, ffn0_w_down, l1_norm_g, l1_w_in, l1_q_norm_g, l1_k_norm_g, l1_w_out, ffn1_norm_g, ffn1_w_up, ffn1_conv_w, ffn1_conv_b, ffn1_w_down):
    bp, s, d = x_prompt.shape
    db, t, _ = x_sample.shape
    w = state_rglru_h.shape[-1]
    f2 = ffn0_w_up.shape[1]
    n_phys = cache_moba_k.shape[0]
    npages = page_table.shape[1]
    past = npages * PAGE_SIZE
    lpad = past + LANES
    nmh = cache_moba_k.shape[2]
    nkv = cache_dsa_k.shape[2]
    nq = l1_w_out.shape[0] // DSA_HEAD_DIM
    assert cache_moba_k.shape[3] == MOBA_HEAD_DIM and cache_dsa_k.shape[3] == DSA_HEAD_DIM
    assert cache_moba_k.shape[1] == PAGE_SIZE and nmh * MOBA_HEAD_DIM == w
    assert rel_bias_table.shape == (N_BUCKETS, nmh) and nq == nmh
    assert l0_w_in.shape[1] == 5 * w

    xp = x_prompt.reshape(bp * s, d)
    xs = x_sample.reshape(db * t, d)
    bt, bsamp, bsx = bias_tiles(rel_bias_table, past, t, lpad, nkv)
    l0_w_in, l0_w_out, l1_w_out = (x.astype(BF16) for x in (l0_w_in, l0_w_out, l1_w_out))
    ffn0_w_up, ffn0_w_down, ffn1_w_up, ffn1_w_down = (
        x.astype(BF16) for x in (ffn0_w_up, ffn0_w_down, ffn1_w_up, ffn1_w_down))

    wa_bd = _regroup_blockdiag(l0_gate_a_w)
    wi_bd = _regroup_blockdiag(l0_gate_i_w)
    pool_mk = cache_moba_k.transpose(0, 2, 3, 1).reshape(n_phys, w, PAGE_SIZE)
    pool_mv = cache_moba_v.transpose(0, 2, 3, 1).reshape(n_phys, w, PAGE_SIZE)

    def even(x2, b, tt, h0, conv_state, attend):
        z = norm_matmul(x2, l0_norm_g, l0_w_in).reshape(b, tt, 5 * w)
        ya, h_last = rglru_branch(z, h0, conv_state, l0_conv_w, l0_conv_b, wa_bd, l0_gate_a_b,
                                  wi_bd, l0_gate_i_b, l0_lambda)
        yb, kn = attend(z)
        mix = jnp.concatenate([ya, yb], axis=-1).reshape(b * tt, 2 * w)
        out = matmul_residual(mix, l0_w_out, x2)
        new_conv = z[:, tt - (l0_conv_w.shape[0] - 1):, :w]
        v_rows = z[:, :, 4 * w:].reshape(b, tt, nmh, MOBA_HEAD_DIM)
        return out, h_last, new_conv, kn.reshape(b, tt, nmh, MOBA_HEAD_DIM), v_rows

    xp, h_p, conv_p, mk_p, mv_p = even(
        xp, bp, s, jnp.zeros((bp, w), F32), jnp.zeros((bp, l0_conv_w.shape[0] - 1, w), F32),
        lambda z: moba_prompt(z, rel_bias_table, bt, l0_q_norm_g, l0_k_norm_g, w))
    xs, h_s, conv_s, mk_s, mv_s = even(
        xs, db, t, state_rglru_h, state_rglru_conv,
        lambda z: moba_sample(z, pool_mk, pool_mv, page_table, bsamp, l0_q_norm_g, l0_k_norm_g, w))

    xp, f0_p = _conv_ffn(xp, (bp, s, d), jnp.zeros((bp, ffn0_conv_w.shape[0] - 1, f2), F32),
                         ffn0_norm_g, ffn0_w_up, ffn0_conv_w, ffn0_conv_b, ffn0_w_down)
    xs, f0_s = _conv_ffn(xs, (db, t, d), state_ffn0_conv,
                         ffn0_norm_g, ffn0_w_up, ffn0_conv_w, ffn0_conv_b, ffn0_w_down)

    wq = nq * DSA_HEAD_DIM
    wkv = nkv * DSA_HEAD_DIM
    wqi = IDX_HEADS * IDX_DIM
    n_in = l1_w_in.shape[1]
    assert n_in == wq + 2 * wkv + wqi + IDX_DIM + IDX_HEADS
    n_pad = -(-n_in // 512) * 512
    w_in1 = jnp.pad(l1_w_in.astype(BF16), ((0, 0), (0, n_pad - n_in)))
    k_blk0 = wq // LANES
    v_blk0 = (wq + wkv) // LANES
    qi_off = wq + 2 * wkv
    assert qi_off % wqi == 0 and (qi_off + wqi) % LANES == 0 and (wq + wkv) % wkv == 0
    qi_blk = qi_off // wqi
    kw_blk = (qi_off + wqi) // LANES
    pool_dk = cache_dsa_k.reshape(n_phys * PAGE_SIZE * nkv, DSA_HEAD_DIM)
    pool_dv = cache_dsa_v.reshape(n_phys * PAGE_SIZE * nkv, DSA_HEAD_DIM)
    pool_ik = cache_dsa_idx_k.transpose(0, 2, 1)

    def odd(x2, b, tt, attend):
        z2 = norm_matmul(x2, l1_norm_g, w_in1)
        kn, knb, vb = kv_prep(z2, l1_k_norm_g, k_blk0, v_blk0, nkv)
        z = z2.reshape(b, tt, n_pad)
        y = attend(z, kn.reshape(b, tt, wkv), knb.reshape(b, tt, wkv), vb.reshape(b, tt, wkv))
        out = matmul_residual(y.reshape(b * tt, wq), l1_w_out, x2)
        v_rows = z[:, :, wq + wkv:wq + 2 * wkv].reshape(b, tt, nkv, DSA_HEAD_DIM)
        ki_rows = z[:, :, qi_off + wqi:qi_off + wqi + IDX_DIM]
        return out, kn.reshape(b, tt, nkv, DSA_HEAD_DIM), v_rows, ki_rows

    def attend_prompt(z, kn, knb, vb):
        isc = index_prompt(z, qi_blk, kw_blk)
        mask = topk_mask(isc.reshape(bp * s, s), min(DSA_TOPK, s // 4), s, 0, True).reshape(bp, s, s)
        return dsa_prompt(z, knb, vb, mask, rel_bias_table, bt, l1_q_norm_g, nq, nkv)

    def attend_sample(z, kn, knb, vb):
        isc = index_sample(z, pool_ik, page_table, qi_blk, kw_blk)
        mask = topk_mask(isc.reshape(db * t, lpad), min(DSA_TOPK, (past + t) // 4), t, past, False)
        return dsa_sample(z, kn, mask.reshape(db, t, lpad), pool_dk, pool_dv, page_table, bsamp, bsx,
                          l1_q_norm_g, nq, nkv, (wq + wkv) // wkv)

    xp, dk_p, dv_p, di_p = odd(xp, bp, s, attend_prompt)
    xs, dk_s, dv_s, di_s = odd(xs, db, t, attend_sample)

    xp, f1_p = _conv_ffn(xp, (bp, s, d), jnp.zeros((bp, ffn1_conv_w.shape[0] - 1, f2), F32),
                         ffn1_norm_g, ffn1_w_up, ffn1_conv_w, ffn1_conv_b, ffn1_w_down)
    xs, f1_s = _conv_ffn(xs, (db, t, d), state_ffn1_conv,
                         ffn1_norm_g, ffn1_w_up, ffn1_conv_w, ffn1_conv_b, ffn1_w_down)

    return (xp.reshape(bp, s, d), xs.reshape(db, t, d), h_p, h_s, conv_p, conv_s, mk_p, mk_s, mv_p, mv_s,
            f0_p, f0_s, dk_p, dk_s, dv_p, dv_s, di_p, di_s, f1_p, f1_s)
```

```python
import functools
import math

import numpy as np
import jax
import jax.numpy as jnp
from jax import lax
from jax.experimental import pallas as pl
from jax.experimental.pallas import tpu as pltpu

F32 = jnp.float32
BF16 = jnp.bfloat16
I32 = jnp.int32
NEG_INF = float("-inf")
INT_MIN = -(2 ** 31)

EPS = 1e-6
LRU_C = 8.0
LRU_GROUP = 256
MOBA_BLOCK = 256
MOBA_TOPK = 3
MOBA_HEAD_DIM = 64
DSA_HEAD_DIM = 128
DSA_TOPK = 256
IDX_DIM = 64
IDX_HEADS = 8
PAGE_SIZE = 128
N_BUCKETS = 32
MAX_DISTANCE = 128
Q_TILE = 256
FFN_COL_TILE = 512
LANES = 128
MIB = 1024 * 1024


def _t5_thresholds():
    exact = N_BUCKETS // 2
    n = np.arange(0, 4 * MAX_DISTANCE)
    nf = np.maximum(n, 1).astype(np.float32)
    large = exact + (np.log(nf / np.float32(exact)) / np.float32(math.log(MAX_DISTANCE / exact))
                     * np.float32(N_BUCKETS - exact)).astype(np.int32)
    bucket = np.where(n < exact, n, np.minimum(large, N_BUCKETS - 1))
    assert np.all(np.diff(bucket) >= 0) and bucket[-1] == N_BUCKETS - 1
    return [int(np.argmax(bucket >= b)) for b in range(N_BUCKETS)]


T5_THR = _t5_thresholds()


def _cp(sem, vmem_mib=48):
    return pltpu.CompilerParams(dimension_semantics=sem, vmem_limit_bytes=vmem_mib * MIB)


def _iota(shape, dim):
    return lax.broadcasted_iota(I32, shape, dim)


def _dot_nt(a, b):
    return lax.dot_general(a, b, (((1,), (1,)), ((), ())), preferred_element_type=F32)


def _dot_nn(a, b):
    return jnp.dot(a, b, preferred_element_type=F32)


def _split2(x):
    hi = x.astype(BF16)
    lo = (x - hi.astype(F32)).astype(BF16)
    return hi, lo


def _split3(x):
    hi = x.astype(BF16)
    r = x - hi.astype(F32)
    mid = r.astype(BF16)
    lo = (r - mid.astype(F32)).astype(BF16)
    return hi, mid, lo


def _dot3(dot, a, b):
    ah, al = _split2(a)
    bh, bl = _split2(b)
    return dot(ah, bh) + (dot(ah, bl) + dot(al, bh))


def _dot_exact01(dot, a, b01):
    h, m, l = _split3(a)
    return dot(h, b01) + (dot(m, b01) + dot(l, b01))


def _t5_bias(dist, tab_ref, h):
    acc = jnp.full(dist.shape, tab_ref[0, h], F32)
    for b in range(1, N_BUCKETS):
        acc = jnp.where(dist >= T5_THR[b], tab_ref[b, h], acc)
    return acc


def _softplus(x):
    return jnp.maximum(x, 0.0) + jnp.log1p(jnp.exp(-jnp.abs(x)))


def _norm_mm_kernel(x_ref, g_ref, w_ref, o_ref, xn_ref):
    @pl.when(pl.program_id(1) == 0)
    def _():
        x = x_ref[...]
        ms = jnp.mean(x * x, axis=-1, keepdims=True)
        xn_ref[...] = (x * lax.rsqrt(ms + EPS) * g_ref[...]).astype(BF16)

    o_ref[...] = _dot_nn(xn_ref[...], w_ref[...].astype(BF16))


def norm_matmul(x, g, w, tn=512):
    m, d = x.shape
    n = w.shape[1]
    tm = min(m, 1024)
    assert m % tm == 0 and n % tn == 0
    return pl.pallas_call(
        _norm_mm_kernel,
        grid=(m // tm, n // tn),
        in_specs=[pl.BlockSpec((tm, d), lambda i, j: (i, 0)),
                  pl.BlockSpec((1, d), lambda i, j: (0, 0)),
                  pl.BlockSpec((d, tn), lambda i, j: (0, j))],
        out_specs=pl.BlockSpec((tm, tn), lambda i, j: (i, j)),
        out_shape=jax.ShapeDtypeStruct((m, n), F32),
        scratch_shapes=[pltpu.VMEM((tm, d), BF16)],
        compiler_params=_cp(("parallel", "arbitrary")),
        name="norm_matmul",
    )(x, g.reshape(1, d), w)


def _mm_res_kernel(a_ref, w_ref, r_ref, o_ref):
    o_ref[...] = r_ref[...] + _dot_nn(a_ref[...], w_ref[...].astype(BF16))


def matmul_residual(a, w, res, tn=512):
    m, kdim = a.shape
    n = w.shape[1]
    tm = min(m, 1024)
    assert m % tm == 0 and n % tn == 0
    return pl.pallas_call(
        _mm_res_kernel,
        grid=(m // tm, n // tn),
        in_specs=[pl.BlockSpec((tm, kdim), lambda i, j: (i, 0)),
                  pl.BlockSpec((kdim, tn), lambda i, j: (0, j)),
                  pl.BlockSpec((tm, tn), lambda i, j: (i, j))],
        out_specs=pl.BlockSpec((tm, tn), lambda i, j: (i, j)),
        out_shape=jax.ShapeDtypeStruct((m, n), F32),
        compiler_params=_cp(("parallel", "arbitrary"), vmem_mib=56),
        name="matmul_residual",
    )(a, w, res)


def _ffn_up_kernel(x_ref, xp_ref, g_ref, w_ref, s_ref, cw_ref, cb_ref, o_ref, xn_ref, xnp_ref, e_ref,
                   *, nb, tc, chunks):
    i = pl.program_id(0)
    tm, tn = o_ref.shape

    def norm(x):
        return x * lax.rsqrt(jnp.mean(x * x, axis=-1, keepdims=True) + EPS) * g_ref[...]

    @pl.when(pl.program_id(1) == 0)
    def _():
        xn_ref[...] = norm(x_ref[...]).astype(BF16)
        if chunks > 1:
            xnp_ref[...] = norm(xp_ref[...])

    wb = w_ref[...].astype(BF16)
    u = _dot_nn(xn_ref[...], wb)
    cw = cw_ref[...]
    if nb == 1:
        prev = s_ref[0]
        if chunks > 1:
            up = _dot_nn(xnp_ref[...].astype(BF16), wb)
            prev = jnp.where(i % chunks == 0, prev, up[6:8, :])
        u1 = pltpu.roll(u, 1, 0)
        u2 = pltpu.roll(u, 2, 0)
        c = cb_ref[...] + u2 * cw[0:1] + u1 * cw[1:2] + u * cw[2:3]
        row = _iota((8, 2 * tn), 0)
        t1 = jnp.where(row == 0, prev[1:2, :], u1[0:8, :])
        t2 = jnp.where(row == 0, prev[0:1, :], jnp.where(row == 1, prev[1:2, :], u2[0:8, :]))
        top = cb_ref[...] + t2 * cw[0:1] + t1 * cw[1:2] + u[0:8, :] * cw[2:3]
        c = jnp.concatenate([top, c[8:, :]], axis=0)
    else:
        e_ref[:, 8:8 + tc, :] = u.reshape(nb, tc, 2 * tn)
        e_ref[:, 6:8, :] = s_ref[...]
        c = (cb_ref[...] + e_ref[:, 6:6 + tc, :] * cw[0:1] + e_ref[:, 7:7 + tc, :] * cw[1:2]
             + e_ref[:, 8:8 + tc, :] * cw[2:3]).reshape(tm, 2 * tn)
    o_ref[...] = (jax.nn.gelu(c[:, tn:]) * c[:, :tn]).astype(o_ref.dtype)


def _pair_tiles(a, tn):
    f = a.shape[-1] // 2
    lead = a.shape[:-1]
    halves = [a[..., h * f:(h + 1) * f].reshape(*lead, f // tn, tn) for h in range(2)]
    return jnp.stack(halves, axis=-2).reshape(*lead, 2 * f)


def _unpair_tiles(a, tn):
    f = a.shape[-1] // 2
    lead = a.shape[:-1]
    t = a.reshape(*lead, f // tn, 2, tn)
    return jnp.concatenate([t[..., 0, :].reshape(*lead, f), t[..., 1, :].reshape(*lead, f)], axis=-1)


def ffn_up(x, seq_len, state_p, norm_g, w_up_p, conv_w_p, conv_b_p, tn):
    m, d = x.shape
    f2 = w_up_p.shape[1]
    f = f2 // 2
    tm = min(m, 1024)
    tc = min(seq_len, tm)
    nb = tm // tc
    chunks = seq_len // tc
    assert m % tm == 0 and tm % tc == 0 and seq_len % tc == 0 and tc % 8 == 0 and (nb == 1 or chunks == 1)
    assert f % tn == 0 and conv_w_p.shape[0] == 3
    r8 = tm // 8
    return pl.pallas_call(
        functools.partial(_ffn_up_kernel, nb=nb, tc=tc, chunks=chunks),
        grid=(m // tm, f // tn),
        in_specs=[pl.BlockSpec((tm, d), lambda i, j: (i, 0)),
                  pl.BlockSpec((8, d), lambda i, j: (jnp.maximum(i * r8 - 1, 0), 0)),
                  pl.BlockSpec((1, d), lambda i, j: (0, 0)),
                  pl.BlockSpec((d, 2 * tn), lambda i, j: (0, j)),
                  pl.BlockSpec((nb, 2, 2 * tn), lambda i, j: (i // chunks, 0, j)),
                  pl.BlockSpec((3, 2 * tn), lambda i, j: (0, j)),
                  pl.BlockSpec((1, 2 * tn), lambda i, j: (0, j))],
        out_specs=pl.BlockSpec((tm, tn), lambda i, j: (i, j)),
        out_shape=jax.ShapeDtypeStruct((m, f), BF16),
        scratch_shapes=[pltpu.VMEM((tm, d), BF16), pltpu.VMEM((8, d), F32),
                        pltpu.VMEM((nb, tc + 8, 2 * tn), F32)],
        compiler_params=_cp(("parallel", "arbitrary"), vmem_mib=56),
        name="ffn_up",
    )(x, x, norm_g.reshape(1, d), w_up_p, state_p, conv_w_p, conv_b_p.reshape(1, f2))


def _rglru_kernel(xa_ref, ga_ref, pv_ref, cs_ref, h0_ref, cw_ref, cb_ref, wa_ref, ba_ref,
                  wi_ref, bi_ref, lam_ref, ya_ref, hl_ref, e_ref, hc_ref, *, tc):
    t = pl.program_id(1)
    w = xa_ref.shape[-1]

    @pl.when(t == 0)
    def _():
        hc_ref[...] = h0_ref[0]
        e_ref[5:8, :] = cs_ref[0]

    @pl.when(t > 0)
    def _():
        e_ref[5:8, :] = pv_ref[0, 5:8, :]

    e_ref[8:8 + tc, :] = xa_ref[0]
    cw = cw_ref[...]
    xc = (cb_ref[...] + e_ref[5:5 + tc, :] * cw[0:1] + e_ref[6:6 + tc, :] * cw[1:2]
          + e_ref[7:7 + tc, :] * cw[2:3] + e_ref[8:8 + tc, :] * cw[3:4])
    xb = xc.astype(BF16)

    def gate(w_ref, b_ref):
        parts = [_dot_nn(xb[:, g * LRU_GROUP:(g + 1) * LRU_GROUP], w_ref[g])
                 for g in range(w // LRU_GROUP)]
        return jax.nn.sigmoid(jnp.concatenate(parts, axis=1) + b_ref[...])

    r = gate(wa_ref, ba_ref)
    i = gate(wi_ref, bi_ref)
    log_a = -LRU_C * r * _softplus(-lam_ref[...])
    a = jnp.exp(log_a)
    u = jnp.sqrt(-jnp.tanh(log_a) * (1.0 + a * a)) * (i * xc)

    row = _iota((tc, w), 0)
    s = 1
    while s < tc:
        a_sh = jnp.where(row >= s, pltpu.roll(a, s, 0), 1.0)
        u_sh = jnp.where(row >= s, pltpu.roll(u, s, 0), 0.0)
        u = u + a * u_sh
        a = a * a_sh
        s *= 2
    h = u + a * hc_ref[...]
    hlast = h[tc - 1:tc, :]
    hc_ref[...] = hlast
    hl_ref[0] = hlast
    ya_ref[0] = (h * jax.nn.gelu(ga_ref[0])).astype(ya_ref.dtype)


def rglru_branch(z, h0, conv_state, conv_w, conv_b, wa_bd, ba, wi_bd, bi, lam):
    b, t, _ = z.shape
    w = h0.shape[-1]
    tc = min(t, 256)
    assert t % tc == 0 and tc % 8 == 0 and conv_w.shape[0] == 4 and w % LRU_GROUP == 0
    r8 = tc // 8
    ng = w // LRU_GROUP
    vec = lambda: pl.BlockSpec((1, w), lambda i, j: (0, 0))
    ya, hl = pl.pallas_call(
        functools.partial(_rglru_kernel, tc=tc),
        grid=(b, t // tc),
        in_specs=[pl.BlockSpec((1, tc, w), lambda i, j: (i, j, 0)),
                  pl.BlockSpec((1, tc, w), lambda i, j: (i, j, 1)),
                  pl.BlockSpec((1, 8, w), lambda i, j: (i, jnp.maximum(j * r8 - 1, 0), 0)),
                  pl.BlockSpec((1, 3, w), lambda i, j: (i, 0, 0)),
                  pl.BlockSpec((1, 1, w), lambda i, j: (i, 0, 0)),
                  pl.BlockSpec((4, w), lambda i, j: (0, 0)),
                  vec(),
                  pl.BlockSpec((ng, LRU_GROUP, LRU_GROUP), lambda i, j: (0, 0, 0)),
                  vec(),
                  pl.BlockSpec((ng, LRU_GROUP, LRU_GROUP), lambda i, j: (0, 0, 0)),
                  vec(), vec()],
        out_specs=[pl.BlockSpec((1, tc, w), lambda i, j: (i, j, 0)),
                   pl.BlockSpec((1, 1, w), lambda i, j: (i, 0, 0))],
        out_shape=[jax.ShapeDtypeStruct((b, t, w), BF16), jax.ShapeDtypeStruct((b, 1, w), F32)],
        scratch_shapes=[pltpu.VMEM((tc + 8, w), F32), pltpu.VMEM((1, w), F32)],
        compiler_params=_cp(("parallel", "arbitrary")),
        name="rglru",
    )(z, z, z, conv_state, h0.reshape(b, 1, w), conv_w, conv_b.reshape(1, w), wa_bd,
      ba.reshape(1, w), wi_bd, bi.reshape(1, w), lam.reshape(1, w))
    return ya, hl.reshape(b, w)


def _regroup_blockdiag(wb):
    n, c, _ = wb.shape
    per = LRU_GROUP // c
    g = n // per
    eye = jnp.eye(per, dtype=wb.dtype)
    out = jnp.einsum("gacd,ab->gacbd", wb.reshape(g, per, c, c), eye)
    return out.reshape(g, LRU_GROUP, LRU_GROUP).astype(BF16)


def _bias_kernel(tab_ref, bt_ref, bs_ref, bx_ref, *, past, lpad, nkv, grp):
    h = pl.program_id(0)
    d0 = _iota((Q_TILE, Q_TILE), 0) - _iota((Q_TILE, Q_TILE), 1)
    bt_ref[0, 0] = _t5_bias(d0, tab_ref, h)
    bt_ref[0, 1] = _t5_bias(d0 + Q_TILE, tab_ref, h)
    t = bs_ref.shape[1]
    ds = past + _iota((t, lpad), 0) - _iota((t, lpad), 1)
    bs_ref[0] = _t5_bias(ds, tab_ref, h)
    col = _iota((t, past * nkv), 1)
    dx = past + _iota((t, past * nkv), 0) - col // nkv
    bx_ref[0] = jnp.where(col % nkv == h // grp, _t5_bias(dx, tab_ref, h), NEG_INF)


def bias_tiles(table, past, t_new, lpad, nkv):
    nh = table.shape[1]
    return pl.pallas_call(
        functools.partial(_bias_kernel, past=past, lpad=lpad, nkv=nkv, grp=nh // nkv),
        grid=(nh,),
        in_specs=[pl.BlockSpec(memory_space=pltpu.SMEM)],
        out_specs=[pl.BlockSpec((1, 2, Q_TILE, Q_TILE), lambda h: (h, 0, 0, 0)),
                   pl.BlockSpec((1, t_new, lpad), lambda h: (h, 0, 0)),
                   pl.BlockSpec((1, t_new, past * nkv), lambda h: (h, 0, 0))],
        out_shape=[jax.ShapeDtypeStruct((nh, 2, Q_TILE, Q_TILE), F32),
                   jax.ShapeDtypeStruct((nh, t_new, lpad), F32),
                   jax.ShapeDtypeStruct((nh, t_new, past * nkv), F32)],
        compiler_params=_cp(("arbitrary",)),
        name="t5_bias_tiles",
    )(table)


def _top_lanes(g, nvalid, k):
    lane = _iota(g.shape, 1).astype(F32)
    gg = jnp.where(lane < nvalid, g, NEG_INF)
    sel = jnp.zeros(g.shape, F32)
    for _ in range(k):
        mx = jnp.max(gg, axis=-1, keepdims=True)
        first = jnp.min(jnp.where(gg == mx, lane, float(g.shape[1])), axis=-1, keepdims=True)
        pick = (lane == first) & (mx > NEG_INF)
        sel = jnp.where(pick, 1.0, sel)
        gg = jnp.where(pick, NEG_INF, gg)
    return sel


def _moba_prompt_kernel(tab_ref, q_ref, k_ref, v_ref, qg_ref, kg_ref, bt_ref, y_ref, kn_ref, *, s):
    hp = pl.program_id(0)
    nb = s // MOBA_BLOCK
    dh = MOBA_HEAD_DIM
    scale = dh ** -0.5
    lane = _iota((1, LANES), 1)
    low = lane < dh

    def headnorm(x, g):
        x2 = x * x
        s0 = jnp.sum(jnp.where(low, x2, 0.0), axis=-1, keepdims=True)
        s1 = jnp.sum(jnp.where(low, 0.0, x2), axis=-1, keepdims=True)
        ms = jnp.where(low, s0, s1) * (1.0 / dh)
        return x * lax.rsqrt(ms + EPS) * g

    qn = headnorm(q_ref[0], qg_ref[...])
    kn = headnorm(k_ref[0], kg_ref[...])
    kn_ref[0] = kn
    kb = kn.astype(BF16)
    vb = v_ref[0].astype(BF16)
    blk_row = _iota((LANES, LANES), 0)
    kmat = jnp.zeros((LANES, LANES), F32)
    for j in range(nb):
        kmean = jnp.mean(kn[j * MOBA_BLOCK:(j + 1) * MOBA_BLOCK], axis=0, keepdims=True)
        kmat = jnp.where(blk_row == j, kmean, kmat)
    qi = _iota((Q_TILE, Q_TILE), 0)
    kj = _iota((Q_TILE, Q_TILE), 1)
    causal = qi >= kj

    near = []
    for hh in range(2):
        t31 = tab_ref[N_BUCKETS - 1, 2 * hp + hh]
        near.append((jnp.where(causal, bt_ref[hh, 0] - t31, NEG_INF), bt_ref[hh, 1] - t31))

    for c in range(nb):
        qc = qn[c * Q_TILE:(c + 1) * Q_TILE]
        nk = (c + 1) * MOBA_BLOCK
        outs = []
        for hh in range(2):
            hmask = low if hh == 0 else jnp.logical_not(low)
            qm = jnp.where(hmask, qc, 0.0)
            sc = _dot_nt((qm * scale).astype(BF16), kb[0:nk])
            if c > MOBA_TOPK:
                sel = _top_lanes(_dot3(_dot_nt, qm, kmat), c, MOBA_TOPK)
            pieces = []
            for j in range(c + 1):
                sj = sc[:, j * MOBA_BLOCK:(j + 1) * MOBA_BLOCK]
                if j == c:
                    sj = sj + near[hh][0]
                else:
                    if j == c - 1:
                        sj = sj + near[hh][1]
                    if c > MOBA_TOPK:
                        sj = jnp.where(sel[:, j:j + 1] > 0.0, sj, NEG_INF)
                pieces.append(sj)
            srow = jnp.concatenate(pieces, axis=1) if c else pieces[0]
            mx = jnp.max(srow, axis=-1, keepdims=True)
            p = jnp.exp(srow - mx)
            l = jnp.sum(p, axis=-1, keepdims=True)
            outs.append(_dot_nn(p.astype(BF16), vb[0:nk]) / l)
        y_ref[0, c * Q_TILE:(c + 1) * Q_TILE, :] = jnp.where(low, outs[0], outs[1]).astype(y_ref.dtype)


def moba_prompt(z, table, bt, q_gain, k_gain, w):
    b, s, _ = z.shape
    assert s % MOBA_BLOCK == 0 and Q_TILE == MOBA_BLOCK and s // MOBA_BLOCK <= LANES
    nhp = w // LANES
    g2 = lambda g: jnp.tile(g, LANES // MOBA_HEAD_DIM).reshape(1, LANES)
    col = lambda off: pl.BlockSpec((1, s, LANES), lambda hp, i: (i, 0, off * nhp + hp))
    gspec = pl.BlockSpec((1, LANES), lambda hp, i: (0, 0))
    return pl.pallas_call(
        functools.partial(_moba_prompt_kernel, s=s),
        grid=(nhp, b),
        in_specs=[pl.BlockSpec(memory_space=pltpu.SMEM), col(2), col(3), col(4), gspec, gspec,
                  pl.BlockSpec((2, 2, Q_TILE, Q_TILE), lambda hp, i: (hp, 0, 0, 0))],
        out_specs=[pl.BlockSpec((1, s, LANES), lambda hp, i: (i, 0, hp)),
                   pl.BlockSpec((1, s, LANES), lambda hp, i: (i, 0, hp))],
        out_shape=[jax.ShapeDtypeStruct((b, s, w), BF16), jax.ShapeDtypeStruct((b, s, w), F32)],
        compiler_params=_cp(("parallel", "parallel")),
        name="moba_prompt",
    )(table, z, z, z, g2(q_gain), g2(k_gain), bt)


def _group_onehot(ngroups_pad, width, gsize):
    return jnp.where(_iota((ngroups_pad, width), 0) == _iota((ngroups_pad, width), 1) // gsize,
                     1.0, 0.0).astype(BF16)


def _group_rmsnorm(x, gain, gsize):
    onehot = _group_onehot(LANES, x.shape[1], gsize)
    ssq = _dot_exact01(_dot_nt, x * x, onehot)
    rinv = lax.rsqrt(ssq * (1.0 / gsize) + EPS)
    return x * _dot_exact01(_dot_nn, rinv, onehot) * gain


def _moba_sample_kernel(pt_ref, q_ref, k_ref, v_ref, qg_ref, kg_ref, bs_ref, *rest, npages, t):
    kp_refs = rest[:npages]
    vp_refs = rest[npages:2 * npages]
    y_ref, kn_ref = rest[2 * npages:]
    w = q_ref.shape[-1]
    dh = MOBA_HEAD_DIM
    nh = w // dh
    rows = nh * t
    ppb = MOBA_BLOCK // PAGE_SIZE
    nbp = npages // ppb
    past = npages * PAGE_SIZE
    scale = dh ** -0.5

    qn = _group_rmsnorm(q_ref[0], qg_ref[...], dh)
    kn = _group_rmsnorm(k_ref[0], kg_ref[...], dh)
    kn_ref[0] = kn
    own = _iota((rows, w), 0) // t == _iota((rows, w), 1) // dh
    qbd = jnp.where(own, jnp.concatenate([qn] * nh, axis=0), 0.0)
    qbb = qbd.astype(BF16)
    pad_rows = lambda x: jnp.concatenate([x, jnp.zeros((LANES - t, w), x.dtype)], axis=0)

    pieces, sums = [], []
    for p in range(npages):
        kpt = kp_refs[p][0]
        sums.append(jnp.sum(kpt, axis=1, keepdims=True))
        pieces.append(_dot_nn(qbb, kpt.astype(BF16)))
    pieces.append(_dot_nt(qbb, pad_rows(kn).astype(BF16)))
    lpad = past + LANES
    srow = jnp.concatenate(pieces, axis=1) * scale + bs_ref[...]

    blk_lane = _iota((w, LANES), 1)
    kmat = jnp.zeros((w, LANES), F32)
    for j in range(nbp):
        kmat = jnp.where(blk_lane == j, sum(sums[j * ppb:(j + 1) * ppb]) * (1.0 / MOBA_BLOCK), kmat)
    sel = _top_lanes(_dot3(_dot_nn, qbd, kmat), nbp, MOBA_TOPK)
    col = _iota((rows, lpad), 1)
    jrow = _iota((rows, lpad), 0) % t
    keep = (col >= past) & (col - past <= jrow)
    for j in range(nbp):
        inblk = (col >= j * MOBA_BLOCK) & (col < (j + 1) * MOBA_BLOCK)
        keep = keep | (inblk & (sel[:, j:j + 1] > 0.0))
    srow = jnp.where(keep, srow, NEG_INF)
    mx = jnp.max(srow, axis=-1, keepdims=True)
    pr = jnp.exp(srow - mx)
    l = jnp.sum(pr, axis=-1, keepdims=True)
    pb = pr.astype(BF16)
    acc = _dot_nn(pb[:, past:], pad_rows(v_ref[0]).astype(BF16))
    for p in range(npages):
        acc = acc + _dot_nt(pb[:, p * PAGE_SIZE:(p + 1) * PAGE_SIZE], vp_refs[p][0].astype(BF16))
    acc = acc / l
    lane_head = _iota((t, w), 1) // dh
    y = jnp.zeros((t, w), F32)
    for h in range(nh):
        y = jnp.where(lane_head == h, acc[h * t:(h + 1) * t, :], y)
    y_ref[0] = y.astype(y_ref.dtype)


def moba_sample(z, pool_k, pool_v, page_table, bs, q_gain, k_gain, w):
    db, t, _ = z.shape
    npages = page_table.shape[1]
    nh = w // MOBA_HEAD_DIM
    assert t == 8 and nh * t == LANES and npages % (MOBA_BLOCK // PAGE_SIZE) == 0
    lpad = npages * PAGE_SIZE + LANES
    gt = lambda g: jnp.tile(g, nh).reshape(1, w)
    col = lambda off: pl.BlockSpec((1, t, w), lambda i, pt: (i, 0, off))
    gspec = pl.BlockSpec((1, w), lambda i, pt: (0, 0))
    page = lambda p: pl.BlockSpec((1, w, PAGE_SIZE), lambda i, pt: (pt[i, p], 0, 0))
    grid_spec = pltpu.PrefetchScalarGridSpec(
        num_scalar_prefetch=1,
        grid=(db,),
        in_specs=[col(2), col(3), col(4), gspec, gspec,
                  pl.BlockSpec((nh * t, lpad), lambda i, pt: (0, 0))]
                 + [page(p) for p in range(npages)] * 2,
        out_specs=[pl.BlockSpec((1, t, w), lambda i, pt: (i, 0, 0)),
                   pl.BlockSpec((1, t, w), lambda i, pt: (i, 0, 0))],
    )
    return pl.pallas_call(
        functools.partial(_moba_sample_kernel, npages=npages, t=t),
        grid_spec=grid_spec,
        out_shape=[jax.ShapeDtypeStruct((db, t, w), BF16), jax.ShapeDtypeStruct((db, t, w), F32)],
        compiler_params=_cp(("arbitrary",), vmem_mib=56),
        name="moba_sample",
    )(page_table, z, z, z, gt(q_gain), gt(k_gain), bs.reshape(nh * t, lpad),
      *([pool_k] * npages), *([pool_v] * npages))


def _kv_prep_kernel(k_ref, v_ref, g_ref, kn_ref, knb_ref, vb_ref):
    x = k_ref[...]
    ms = jnp.mean(x * x, axis=-1, keepdims=True)
    kn = x * lax.rsqrt(ms + EPS) * g_ref[...]
    kn_ref[...] = kn
    knb_ref[...] = kn.astype(BF16)
    vb_ref[...] = v_ref[...].astype(BF16)


def kv_prep(z, gain, k_blk0, v_blk0, nheads):
    m = z.shape[0]
    tm = min(m, 1024)
    assert m % tm == 0
    out = pl.BlockSpec((tm, LANES), lambda i, h: (i, h))
    return pl.pallas_call(
        _kv_prep_kernel,
        grid=(m // tm, nheads),
        in_specs=[pl.BlockSpec((tm, LANES), lambda i, h: (i, k_blk0 + h)),
                  pl.BlockSpec((tm, LANES), lambda i, h: (i, v_blk0 + h)),
                  pl.BlockSpec((1, LANES), lambda i, h: (0, 0))],
        out_specs=[out, out, out],
        out_shape=[jax.ShapeDtypeStruct((m, nheads * LANES), F32),
                   jax.ShapeDtypeStruct((m, nheads * LANES), BF16),
                   jax.ShapeDtypeStruct((m, nheads * LANES), BF16)],
        compiler_params=_cp(("parallel", "parallel")),
        name="kv_prep",
    )(z, z, gain.reshape(1, LANES))


def _index_prompt_kernel(qi_ref, qw_ref, kk_ref, o_ref, *, s):
    c = pl.program_id(1)
    lane = _iota((1, LANES), 1)
    low = lane < IDX_DIM
    qw = qw_ref[0]

    def tile(cc):
        nk = (cc + 1) * Q_TILE
        kk = jnp.where(low, kk_ref[0, 0:nk, :], 0.0)
        kk_hi = pltpu.roll(kk, IDX_DIM, 1)
        acc = jnp.zeros((Q_TILE, nk), F32)
        for ih in range(IDX_HEADS):
            blk = qi_ref[0, :, (ih // 2) * LANES:(ih // 2 + 1) * LANES]
            half_low = ih % 2 == 0
            qm = jnp.where(low if half_low else jnp.logical_not(low), blk, 0.0)
            sc = _dot3(_dot_nt, qm, kk if half_low else kk_hi)
            wcol = qw[:, IDX_DIM + ih:IDX_DIM + ih + 1] * (IDX_HEADS ** -0.5)
            acc = acc + wcol * jnp.maximum(sc * (IDX_DIM ** -0.5), 0.0)
        qpos = cc * Q_TILE + _iota((Q_TILE, nk), 0)
        o_ref[0, :, 0:nk] = jnp.where(_iota((Q_TILE, nk), 1) <= qpos, acc, NEG_INF)
        if nk < s:
            o_ref[0, :, nk:] = jnp.full((Q_TILE, s - nk), NEG_INF, F32)

    for cc in range(s // Q_TILE):
        pl.when(c == cc)(functools.partial(tile, cc))


def index_prompt(z, qi_blk, kw_blk):
    b, s, _ = z.shape
    assert s % Q_TILE == 0
    return pl.pallas_call(
        functools.partial(_index_prompt_kernel, s=s),
        grid=(b, s // Q_TILE),
        in_specs=[pl.BlockSpec((1, Q_TILE, IDX_HEADS * IDX_DIM), lambda i, c: (i, c, qi_blk)),
                  pl.BlockSpec((1, Q_TILE, LANES), lambda i, c: (i, c, kw_blk)),
                  pl.BlockSpec((1, s, LANES), lambda i, c: (i, 0, kw_blk))],
        out_specs=pl.BlockSpec((1, Q_TILE, s), lambda i, c: (i, c, 0)),
        out_shape=jax.ShapeDtypeStruct((b, s, s), F32),
        compiler_params=_cp(("parallel", "parallel")),
        name="index_prompt",
    )(z, z, z)


def _index_sample_kernel(pt_ref, qi_ref, qw_ref, *rest, npages, t):
    kp_refs = rest[:npages]
    o_ref = rest[npages]
    qi = qi_ref[0]
    qw = qw_ref[0]
    q2 = jnp.concatenate([qi[:, ih * IDX_DIM:(ih + 1) * IDX_DIM] for ih in range(IDX_HEADS)], axis=0)
    wcol = jnp.concatenate([qw[:, IDX_DIM + ih:IDX_DIM + ih + 1] for ih in range(IDX_HEADS)], axis=0)
    wcol = wcol * (IDX_HEADS ** -0.5)
    knew = jnp.concatenate([qw[:, 0:IDX_DIM], jnp.zeros((LANES - t, IDX_DIM), F32)], axis=0)

    def scores(dot, keys):
        sc = wcol * jnp.maximum(_dot3(dot, q2, keys) * (IDX_DIM ** -0.5), 0.0)
        return sum(sc[ih * t:(ih + 1) * t, :] for ih in range(IDX_HEADS))

    pieces = [scores(_dot_nn, kp_refs[p][0]) for p in range(npages)]
    new = scores(_dot_nt, knew)
    jj = _iota((t, LANES), 0)
    cc = _iota((t, LANES), 1)
    pieces.append(jnp.where(cc <= jj, new, NEG_INF))
    o_ref[0] = jnp.concatenate(pieces, axis=1)


def index_sample(z, pool_ik, page_table, qi_blk, kw_blk):
    db, t, _ = z.shape
    npages = page_table.shape[1]
    lpad = npages * PAGE_SIZE + LANES
    grid_spec = pltpu.PrefetchScalarGridSpec(
        num_scalar_prefetch=1,
        grid=(db,),
        in_specs=[pl.BlockSpec((1, t, IDX_HEADS * IDX_DIM), lambda i, pt: (i, 0, qi_blk)),
                  pl.BlockSpec((1, t, LANES), lambda i, pt: (i, 0, kw_blk))]
                 + [pl.BlockSpec((1, IDX_DIM, PAGE_SIZE), (lambda i, pt, p=p: (pt[i, p], 0, 0)))
                    for p in range(npages)],
        out_specs=pl.BlockSpec((1, t, lpad), lambda i, pt: (i, 0, 0)),
    )
    return pl.pallas_call(
        functools.partial(_index_sample_kernel, npages=npages, t=t),
        grid_spec=grid_spec,
        out_shape=jax.ShapeDtypeStruct((db, t, lpad), F32),
        compiler_params=_cp(("arbitrary",)),
        name="index_sample",
    )(page_table, z, z, *([pool_ik] * npages))


def _topk_mask_kernel(x_ref, o_ref, *, topk, period, base, tr, additive, ncausal):
    l = x_ref.shape[1]
    on, off = (0.0, NEG_INF) if additive else (1.0, 0.0)

    def body(nk):
        rowg = pl.program_id(0) * tr + _iota((tr, nk), 0)
        valid = _iota((tr, nk), 1) <= base + rowg % period
        bits = pltpu.bitcast(x_ref[:, 0:nk], I32)
        key = jnp.where(bits < 0, bits ^ 0x7FFFFFFF, bits)
        key = jnp.where(valid, key, INT_MIN)

        def step(i, lo):
            cand = lo + lax.shift_left(jnp.int32(1), 31 - i)
            cnt = jnp.sum(jnp.where(key >= cand, 1.0, 0.0), axis=-1, keepdims=True)
            return jnp.where(cnt >= topk, cand, lo)

        thr = lax.fori_loop(0, 32, step, jnp.full((tr, 1), INT_MIN, I32))
        gt = key > thr
        need = topk - jnp.sum(jnp.where(gt, 1.0, 0.0), axis=-1, keepdims=True)
        eq = jnp.where(key == thr, 1.0, 0.0)
        upper = jnp.where(_iota((LANES, LANES), 0) < _iota((LANES, LANES), 1), 1.0, 0.0).astype(BF16)
        before = jnp.zeros((tr, 1), F32)
        for j in range(nk // LANES):
            e = eq[:, j * LANES:(j + 1) * LANES]
            rank = before + _dot_nn(e.astype(BF16), upper)
            pick = gt[:, j * LANES:(j + 1) * LANES] | ((e > 0.0) & (rank < need))
            pick = pick & valid[:, j * LANES:(j + 1) * LANES]
            o_ref[:, j * LANES:(j + 1) * LANES] = jnp.where(pick, on, off)
            before = before + jnp.sum(e, axis=-1, keepdims=True)
        if nk < l:
            o_ref[:, nk:] = jnp.full((tr, l - nk), off, F32)

    if ncausal > 1:
        c = pl.program_id(0) % ncausal
        for cc in range(ncausal):
            pl.when(c == cc)(functools.partial(body, (cc + 1) * tr))
    else:
        body(l)


def topk_mask(scores, topk, period, base, additive):
    r, l = scores.shape
    tr = min(r, 256)
    assert r % tr == 0 and l % LANES == 0
    ncausal = period // tr if (base == 0 and period == l and period % tr == 0) else 1
    return pl.pallas_call(
        functools.partial(_topk_mask_kernel, topk=topk, period=period, base=base, tr=tr, additive=additive,
                          ncausal=ncausal),
        grid=(r // tr,),
        in_specs=[pl.BlockSpec((tr, l), lambda i: (i, 0))],
        out_specs=pl.BlockSpec((tr, l), lambda i: (i, 0)),
        out_shape=jax.ShapeDtypeStruct((r, l), F32),
        compiler_params=_cp(("parallel",)),
        name="topk_mask",
    )(scores)


def _dsa_prompt_kernel(tab_ref, q_ref, k_ref, v_ref, m_ref, bt_ref, qg_ref, y_ref, *, s):
    c = pl.program_id(1)
    g = pl.program_id(2)
    dh = DSA_HEAD_DIM
    grp = q_ref.shape[-1] // dh
    qbs, t31s = [], []
    for hh in range(grp):
        q = q_ref[0, :, hh * dh:(hh + 1) * dh]
        qn = q * lax.rsqrt(jnp.mean(q * q, axis=-1, keepdims=True) + EPS) * qg_ref[...]
        qbs.append((qn * (dh ** -0.5)).astype(BF16))
        t31s.append(tab_ref[N_BUCKETS - 1, g * grp + hh])

    def tile(cc):
        nk = (cc + 1) * Q_TILE
        madd = m_ref[0, :, 0:nk]
        for hh in range(grp):
            sc = _dot_nt(qbs[hh], k_ref[0, 0:nk, :]) + madd
            pieces = [sc[:, 0:(cc - 1) * Q_TILE]] if cc > 1 else []
            if cc > 0:
                pieces.append(sc[:, (cc - 1) * Q_TILE:cc * Q_TILE] + (bt_ref[hh, 1] - t31s[hh]))
            pieces.append(sc[:, cc * Q_TILE:nk] + (bt_ref[hh, 0] - t31s[hh]))
            srow = jnp.concatenate(pieces, axis=1) if cc else pieces[0]
            mx = jnp.max(srow, axis=-1, keepdims=True)
            p = jnp.exp(srow - mx)
            l = jnp.sum(p, axis=-1, keepdims=True)
            y_ref[0, :, hh * dh:(hh + 1) * dh] = (_dot_nn(p.astype(BF16), v_ref[0, 0:nk, :]) / l
                                                  ).astype(y_ref.dtype)

    for cc in range(s // Q_TILE):
        pl.when(c == cc)(functools.partial(tile, cc))


def dsa_prompt(z, knb, vb, madd, table, bt, q_gain, nq, nkv):
    b, s, _ = z.shape
    grp = nq // nkv
    return pl.pallas_call(
        functools.partial(_dsa_prompt_kernel, s=s),
        grid=(b, s // Q_TILE, nkv),
        in_specs=[pl.BlockSpec(memory_space=pltpu.SMEM),
                  pl.BlockSpec((1, Q_TILE, grp * LANES), lambda i, c, g: (i, c, g)),
                  pl.BlockSpec((1, s, LANES), lambda i, c, g: (i, 0, g)),
                  pl.BlockSpec((1, s, LANES), lambda i, c, g: (i, 0, g)),
                  pl.BlockSpec((1, Q_TILE, s), lambda i, c, g: (i, c, 0)),
                  pl.BlockSpec((grp, 2, Q_TILE, Q_TILE), lambda i, c, g: (g, 0, 0, 0)),
                  pl.BlockSpec((1, LANES), lambda i, c, g: (0, 0))],
        out_specs=pl.BlockSpec((1, Q_TILE, grp * LANES), lambda i, c, g: (i, c, g)),
        out_shape=jax.ShapeDtypeStruct((b, s, nq * LANES), BF16),
        compiler_params=_cp(("parallel", "parallel", "arbitrary")),
        name="dsa_prompt",
    )(table, z, knb, vb, madd, bt, q_gain.reshape(1, LANES))


def _dsa_sample_kernel(pt_ref, q_ref, kn_ref, v_ref, m_ref, bs_ref, bx_ref, qg_ref, *rest,
                       npages, t, nq, nkv):
    kp_refs = rest[:npages]
    vp_refs = rest[npages:2 * npages]
    y_ref = rest[2 * npages]
    dh = DSA_HEAD_DIM
    grp = nq // nkv
    wkv = nkv * dh
    past = npages * PAGE_SIZE
    pcols = PAGE_SIZE * nkv
    scale = dh ** -0.5
    q = q_ref[0]
    g = qg_ref[...]
    zero = jnp.zeros((t, dh), F32)
    qrows, qstruct = [], []
    for h in range(nq):
        qh = q[:, h * dh:(h + 1) * dh]
        qh = qh * lax.rsqrt(jnp.mean(qh * qh, axis=-1, keepdims=True) + EPS) * g
        qrows.append(qh)
        qstruct.append(jnp.concatenate([qh if kv == h // grp else zero for kv in range(nkv)], axis=1))
    qall = jnp.concatenate(qrows, axis=0).astype(BF16)
    qsb = jnp.concatenate(qstruct, axis=0).astype(BF16)
    pad_rows = lambda x: jnp.concatenate([x, jnp.zeros((LANES - t, wkv), x.dtype)], axis=0)
    keep = jnp.concatenate([m_ref[0]] * nq, axis=0)
    keepb = keep.astype(BF16)
    spread = jnp.where(_iota((PAGE_SIZE, pcols), 0) == _iota((PAGE_SIZE, pcols), 1) // nkv,
                       1.0, 0.0).astype(BF16)

    pieces = []
    for p in range(npages):
        sp = _dot_nt(qall, kp_refs[p][...].astype(BF16)) * scale + bx_ref[:, p * pcols:(p + 1) * pcols]
        kexp = _dot_nn(keepb[:, p * PAGE_SIZE:(p + 1) * PAGE_SIZE], spread)
        pieces.append(jnp.where(kexp > 0.5, sp, NEG_INF))
    loc = _dot_nt(qsb, pad_rows(kn_ref[0]).astype(BF16)) * scale + bs_ref[:, past:]
    pieces.append(jnp.where(keep[:, past:] > 0.5, loc, NEG_INF))
    srow = jnp.concatenate(pieces, axis=1)
    mx = jnp.max(srow, axis=-1, keepdims=True)
    pr = jnp.exp(srow - mx)
    l = jnp.sum(pr, axis=-1, keepdims=True)
    pb = pr.astype(BF16)
    accl = _dot_nn(pb[:, npages * pcols:], pad_rows(v_ref[0]).astype(BF16))
    row_grp = _iota((nq * t, dh), 0) // (t * grp)
    acc = jnp.zeros((nq * t, dh), F32)
    for kv in range(nkv):
        acc = jnp.where(row_grp == kv, accl[:, kv * dh:(kv + 1) * dh], acc)
    for p in range(npages):
        acc = acc + _dot_nn(pb[:, p * pcols:(p + 1) * pcols], vp_refs[p][...].astype(BF16))
    acc = acc / l
    y_ref[0] = jnp.concatenate([acc[h * t:(h + 1) * t, :] for h in range(nq)], axis=1).astype(y_ref.dtype)


def dsa_sample(z, kn, mask, pool_k, pool_v, page_table, bs, bx, q_gain, nq, nkv, v_blk):
    db, t, _ = z.shape
    npages = page_table.shape[1]
    wkv = nkv * DSA_HEAD_DIM
    past = npages * PAGE_SIZE
    lpad = past + LANES
    assert nq * t == LANES
    page = lambda p: pl.BlockSpec((PAGE_SIZE * nkv, DSA_HEAD_DIM), lambda i, pt: (pt[i, p], 0))
    grid_spec = pltpu.PrefetchScalarGridSpec(
        num_scalar_prefetch=1,
        grid=(db,),
        in_specs=[pl.BlockSpec((1, t, nq * DSA_HEAD_DIM), lambda i, pt: (i, 0, 0)),
                  pl.BlockSpec((1, t, wkv), lambda i, pt: (i, 0, 0)),
                  pl.BlockSpec((1, t, wkv), lambda i, pt: (i, 0, v_blk)),
                  pl.BlockSpec((1, t, lpad), lambda i, pt: (i, 0, 0)),
                  pl.BlockSpec((nq * t, lpad), lambda i, pt: (0, 0)),
                  pl.BlockSpec((nq * t, past * nkv), lambda i, pt: (0, 0)),
                  pl.BlockSpec((1, LANES), lambda i, pt: (0, 0))]
                 + [page(p) for p in range(npages)] * 2,
        out_specs=pl.BlockSpec((1, t, nq * DSA_HEAD_DIM), lambda i, pt: (i, 0, 0)),
    )
    return pl.pallas_call(
        functools.partial(_dsa_sample_kernel, npages=npages, t=t, nq=nq, nkv=nkv),
        grid_spec=grid_spec,
        out_shape=jax.ShapeDtypeStruct((db, t, nq * DSA_HEAD_DIM), BF16),
        compiler_params=_cp(("arbitrary",)),
        name="dsa_sample",
    )(page_table, z, kn, z, mask, bs.reshape(nq * t, lpad), bx.reshape(nq * t, past * nkv),
      q_gain.reshape(1, LANES), *([pool_k] * npages), *([pool_v] * npages))


def _conv_ffn(x2, shape3, state, norm_g, w_up, conv_w, conv_b, w_down):
    b, t, d = shape3
    keep = conv_w.shape[0] - 1
    tn = FFN_COL_TILE
    w_up_p = _pair_tiles(w_up, tn)
    mid = ffn_up(x2, t, _pair_tiles(state, tn), norm_g, w_up_p, _pair_tiles(conv_w, tn),
                 _pair_tiles(conv_b, tn), tn)
    y = matmul_residual(mid, w_down, x2)
    tail = x2.reshape(b, t, d)[:, t - keep:, :].reshape(b * keep, d)
    return y, _unpair_tiles(norm_matmul(tail, norm_g, w_up_p), tn).reshape(b, keep, -1)


def kernel(x_prompt, x_sample, state_rglru_h, state_rglru_conv, cache_moba_k, cache_moba_v, state_ffn0_conv, cache_dsa_k, cache_dsa_v, cache_dsa_idx_k, state_ffn1_conv, page_table, rel_bias_table, l0_norm_g, l0_w_in, l0_conv_w, l0_conv_b, l0_gate_a_w, l0_gate_a_b, l0_gate_i_w, l0_gate_i_b, l0_lambda, l0_q_norm_g, l0_k_norm_g, l0_w_out, ffn0_norm_g, ffn0_w_up, ffn0_conv_w, ffn0_conv_b, ffn0_w_down, l1_norm_g, l1_w_in, l1_q_norm_g, l1_k_norm_g, l1_w_out, ffn1_norm_g, ffn1_w_up, ffn1_conv_w, ffn1_conv_b, ffn1_w_down):
    bp, s, d = x_prompt.shape
    db, t, _ = x_sample.shape
    w = state_rglru_h.shape[-1]
    f2 = ffn0_w_up.shape[1]
    n_phys = cache_moba_k.shape[0]
    npages = page_table.shape[1]
    past = npages * PAGE_SIZE
    lpad = past + LANES
    nmh = cache_moba_k.shape[2]
    nkv = cache_dsa_k.shape[2]
    nq = l1_w_out.shape[0] // DSA_HEAD_DIM
    assert cache_moba_k.shape[3] == MOBA_HEAD_DIM and cache_dsa_k.shape[3] == DSA_HEAD_DIM
    assert cache_moba_k.shape[1] == PAGE_SIZE and nmh * MOBA_HEAD_DIM == w
    assert rel_bias_table.shape == (N_BUCKETS, nmh) and nq == nmh
    assert l0_w_in.shape[1] == 5 * w

    xp = x_prompt.reshape(bp * s, d)
    xs = x_sample.reshape(db * t, d)
    bt, bsamp, bsx = bias_tiles(rel_bias_table, past, t, lpad, nkv)
    l0_w_in, l0_w_out, l1_w_out = (x.astype(BF16) for x in (l0_w_in, l0_w_out, l1_w_out))
    ffn0_w_up, ffn0_w_down, ffn1_w_up, ffn1_w_down = (
        x.astype(BF16) for x in (ffn0_w_up, ffn0_w_down, ffn1_w_up, ffn1_w_down))

    wa_bd = _regroup_blockdiag(l0_gate_a_w)
    wi_bd = _regroup_blockdiag(l0_gate_i_w)
    pool_mk = cache_moba_k.transpose(0, 2, 3, 1).reshape(n_phys, w, PAGE_SIZE)
    pool_mv = cache_moba_v.transpose(0, 2, 3, 1).reshape(n_phys, w, PAGE_SIZE)

    def even(x2, b, tt, h0, conv_state, attend):
        z = norm_matmul(x2, l0_norm_g, l0_w_in).reshape(b, tt, 5 * w)
        ya, h_last = rglru_branch(z, h0, conv_state, l0_conv_w, l0_conv_b, wa_bd, l0_gate_a_b,
                                  wi_bd, l0_gate_i_b, l0_lambda)
        yb, kn = attend(z)
        mix = jnp.concatenate([ya, yb], axis=-1).reshape(b * tt, 2 * w)
        out = matmul_residual(mix, l0_w_out, x2)
        new_conv = z[:, tt - (l0_conv_w.shape[0] - 1):, :w]
        v_rows = z[:, :, 4 * w:].reshape(b, tt, nmh, MOBA_HEAD_DIM)
        return out, h_last, new_conv, kn.reshape(b, tt, nmh, MOBA_HEAD_DIM), v_rows

    xp, h_p, conv_p, mk_p, mv_p = even(
        xp, bp, s, jnp.zeros((bp, w), F32), jnp.zeros((bp, l0_conv_w.shape[0] - 1, w), F32),
        lambda z: moba_prompt(z, rel_bias_table, bt, l0_q_norm_g, l0_k_norm_g, w))
    xs, h_s, conv_s, mk_s, mv_s = even(
        xs, db, t, state_rglru_h, state_rglru_conv,
        lambda z: moba_sample(z, pool_mk, pool_mv, page_table, bsamp, l0_q_norm_g, l0_k_norm_g, w))

    xp, f0_p = _conv_ffn(xp, (bp, s, d), jnp.zeros((bp, ffn0_conv_w.shape[0] - 1, f2), F32),
                         ffn0_norm_g, ffn0_w_up, ffn0_conv_w, ffn0_conv_b, ffn0_w_down)
    xs, f0_s = _conv_ffn(xs, (db, t, d), state_ffn0_conv,
                         ffn0_norm_g, ffn0_w_up, ffn0_conv_w, ffn0_conv_b, ffn0_w_down)

    wq = nq * DSA_HEAD_DIM
    wkv = nkv * DSA_HEAD_DIM
    wqi = IDX_HEADS * IDX_DIM
    n_in = l1_w_in.shape[1]
    assert n_in == wq + 2 * wkv + wqi + IDX_DIM + IDX_HEADS
    n_pad = -(-n_in // 512) * 512
    w_in1 = jnp.pad(l1_w_in.astype(BF16), ((0, 0), (0, n_pad - n_in)))
    k_blk0 = wq // LANES
    v_blk0 = (wq + wkv) // LANES
    qi_off = wq + 2 * wkv
    assert qi_off % wqi == 0 and (qi_off + wqi) % LANES == 0 and (wq + wkv) % wkv == 0
    qi_blk = qi_off // wqi
    kw_blk = (qi_off + wqi) // LANES
    pool_dk = cache_dsa_k.reshape(n_phys * PAGE_SIZE * nkv, DSA_HEAD_DIM)
    pool_dv = cache_dsa_v.reshape(n_phys * PAGE_SIZE * nkv, DSA_HEAD_DIM)
    pool_ik = cache_dsa_idx_k.transpose(0, 2, 1)

    def odd(x2, b, tt, attend):
        z2 = norm_matmul(x2, l1_norm_g, w_in1)
        kn, knb, vb = kv_prep(z2, l1_k_norm_g, k_blk0, v_blk0, nkv)
        z = z2.reshape(b, tt, n_pad)
        y = attend(z, kn.reshape(b, tt, wkv), knb.reshape(b, tt, wkv), vb.reshape(b, tt, wkv))
        out = matmul_residual(y.reshape(b * tt, wq), l1_w_out, x2)
        v_rows = z[:, :, wq + wkv:wq + 2 * wkv].reshape(b, tt, nkv, DSA_HEAD_DIM)
        ki_rows = z[:, :, qi_off + wqi:qi_off + wqi + IDX_DIM]
        return out, kn.reshape(b, tt, nkv, DSA_HEAD_DIM), v_rows, ki_rows

    def attend_prompt(z, kn, knb, vb):
        isc = index_prompt(z, qi_blk, kw_blk)
        mask = topk_mask(isc.reshape(bp * s, s), min(DSA_TOPK, s // 4), s, 0, True).reshape(bp, s, s)
        return dsa_prompt(z, knb, vb, mask, rel_bias_table, bt, l1_q_norm_g, nq, nkv)

    def attend_sample(z, kn, knb, vb):
        isc = index_sample(z, pool_ik, page_table, qi_blk, kw_blk)
        mask = topk_mask(isc.reshape(db * t, lpad), min(DSA_TOPK, (past + t) // 4), t, past, False)
        return dsa_sample(z, kn, mask.reshape(db, t, lpad), pool_dk, pool_dv, page_table, bsamp, bsx,
                          l1_q_norm_g, nq, nkv, (wq + wkv) // wkv)

    xp, dk_p, dv_p, di_p = odd(xp, bp, s, attend_prompt)
    xs, dk_s, dv_s, di_s = odd(xs, db, t, attend_sample)

    xp, f1_p = _conv_ffn(xp, (bp, s, d), jnp.zeros((bp, ffn1_conv_w.shape[0] - 1, f2), F32),
                         ffn1_norm_g, ffn1_w_up, ffn1_conv_w, ffn1_conv_b, ffn1_w_down)
    xs, f1_s = _conv_ffn(xs, (db, t, d), state_ffn1_conv,
                         ffn1_norm_g, ffn1_w_up, ffn1_conv_w, ffn1_conv_b, ffn1_w_down)

    return (xp.reshape(bp, s, d), xs.reshape(db, t, d), h_p, h_s, conv_p, conv_s, mk_p, mk_s, mv_p, mv_s,
            f0_p, f0_s, dk_p, dk_s, dv_p, dv_s, di_p, di_s, f1_p, f1_s)
```

```python
import functools
import math

import numpy as np
import jax
import jax.numpy as jnp
from jax import lax
from jax.experimental import pallas as pl
from jax.experimental.pallas import tpu as pltpu

F32 = jnp.float32
BF16 = jnp.bfloat16
I32 = jnp.int32
NEG_INF = float("-inf")
INT_MIN = -(2 ** 31)

EPS = 1e-6
LRU_C = 8.0
LRU_GROUP = 256
MOBA_BLOCK = 256
MOBA_TOPK = 3
MOBA_HEAD_DIM = 64
DSA_HEAD_DIM = 128
DSA_TOPK = 256
IDX_DIM = 64
IDX_HEADS = 8
PAGE_SIZE = 128
N_BUCKETS = 32
MAX_DISTANCE = 128
Q_TILE = 256
LANES = 128
MIB = 1024 * 1024


def _t5_thresholds():
    exact = N_BUCKETS // 2
    n = np.arange(0, 4 * MAX_DISTANCE)
    nf = np.maximum(n, 1).astype(np.float32)
    large = exact + (np.log(nf / np.float32(exact)) / np.float32(math.log(MAX_DISTANCE / exact))
                     * np.float32(N_BUCKETS - exact)).astype(np.int32)
    bucket = np.where(n < exact, n, np.minimum(large, N_BUCKETS - 1))
    assert np.all(np.diff(bucket) >= 0) and bucket[-1] == N_BUCKETS - 1
    return [int(np.argmax(bucket >= b)) for b in range(N_BUCKETS)]


T5_THR = _t5_thresholds()


def _cp(sem, vmem_mib=48):
    return pltpu.CompilerParams(dimension_semantics=sem, vmem_limit_bytes=vmem_mib * MIB)


def _iota(shape, dim):
    return lax.broadcasted_iota(I32, shape, dim)


def _dot_nt(a, b):
    return lax.dot_general(a, b, (((1,), (1,)), ((), ())), preferred_element_type=F32)


def _dot_nn(a, b):
    return jnp.dot(a, b, preferred_element_type=F32)


def _split2(x):
    hi = x.astype(BF16)
    lo = (x - hi.astype(F32)).astype(BF16)
    return hi, lo


def _split3(x):
    hi = x.astype(BF16)
    r = x - hi.astype(F32)
    mid = r.astype(BF16)
    lo = (r - mid.astype(F32)).astype(BF16)
    return hi, mid, lo


def _dot3(dot, a, b):
    ah, al = _split2(a)
    bh, bl = _split2(b)
    return dot(ah, bh) + (dot(ah, bl) + dot(al, bh))


def _dot_exact01(dot, a, b01):
    h, m, l = _split3(a)
    return dot(h, b01) + (dot(m, b01) + dot(l, b01))


def _t5_bias(dist, tab_ref, h):
    acc = jnp.full(dist.shape, tab_ref[0, h], F32)
    for b in range(1, N_BUCKETS):
        acc = jnp.where(dist >= T5_THR[b], tab_ref[b, h], acc)
    return acc


def _softplus(x):
    return jnp.maximum(x, 0.0) + jnp.log1p(jnp.exp(-jnp.abs(x)))


def _norm_mm_kernel(x_ref, g_ref, w_ref, o_ref, *rest, emit):
    xn_ref = rest[-1]

    @pl.when(pl.program_id(1) == 0)
    def _():
        x = x_ref[...]
        ms = jnp.mean(x * x, axis=-1, keepdims=True)
        xn_ref[...] = (x * lax.rsqrt(ms + EPS) * g_ref[...]).astype(BF16)

    wb = w_ref[...].astype(BF16)
    if emit:
        rest[0][...] = wb
    o_ref[...] = _dot_nn(xn_ref[...], wb)


def norm_matmul(x, g, w, emit=False, tn=512):
    m, d = x.shape
    n = w.shape[1]
    tm = min(m, 1024)
    assert m % tm == 0 and n % tn == 0 and (not emit or m == tm)
    out_specs = [pl.BlockSpec((tm, tn), lambda i, j: (i, j))]
    out_shape = [jax.ShapeDtypeStruct((m, n), F32)]
    if emit:
        out_specs.append(pl.BlockSpec((d, tn), lambda i, j: (0, j)))
        out_shape.append(jax.ShapeDtypeStruct((d, n), BF16))
    res = pl.pallas_call(
        functools.partial(_norm_mm_kernel, emit=emit),
        grid=(m // tm, n // tn),
        in_specs=[pl.BlockSpec((tm, d), lambda i, j: (i, 0)),
                  pl.BlockSpec((1, d), lambda i, j: (0, 0)),
                  pl.BlockSpec((d, tn), lambda i, j: (0, j))],
        out_specs=out_specs,
        out_shape=out_shape,
        scratch_shapes=[pltpu.VMEM((tm, d), BF16)],
        compiler_params=_cp(("parallel", "arbitrary")),
        name="norm_matmul",
    )(x, g.reshape(1, d), w)
    return res if emit else res[0]


def _mm_res_kernel(a_ref, w_ref, r_ref, o_ref, *rest, emit):
    wb = w_ref[...].astype(BF16)
    if emit:
        rest[0][...] = wb
    o_ref[...] = r_ref[...] + _dot_nn(a_ref[...], wb)


def matmul_residual(a, w, res, emit=False):
    m, kdim = a.shape
    n = w.shape[1]
    tm = min(m, 1024)
    tn = 256 if emit else 512
    assert m % tm == 0 and n % tn == 0 and (not emit or m == tm)
    out_specs = [pl.BlockSpec((tm, tn), lambda i, j: (i, j))]
    out_shape = [jax.ShapeDtypeStruct((m, n), F32)]
    if emit:
        out_specs.append(pl.BlockSpec((kdim, tn), lambda i, j: (0, j)))
        out_shape.append(jax.ShapeDtypeStruct((kdim, n), BF16))
    out = pl.pallas_call(
        functools.partial(_mm_res_kernel, emit=emit),
        grid=(m // tm, n // tn),
        in_specs=[pl.BlockSpec((tm, kdim), lambda i, j: (i, 0)),
                  pl.BlockSpec((kdim, tn), lambda i, j: (0, j)),
                  pl.BlockSpec((tm, tn), lambda i, j: (i, j))],
        out_specs=out_specs,
        out_shape=out_shape,
        compiler_params=_cp(("parallel", "arbitrary"), vmem_mib=56),
        name="matmul_residual",
    )(a, w, res)
    return out if emit else out[0]


def _ffn_up_kernel(x_ref, xp_ref, g_ref, wv_ref, wg_ref, sv_ref, sg_ref, cwv_ref, cwg_ref, cbv_ref, cbg_ref,
                   o_ref, nsv_ref, nsg_ref, *rest, nb, tc, chunks, emit):
    xn_ref, xnp_ref, ev_ref, eg_ref = rest[-4:]
    i = pl.program_id(0)
    tm, tn = o_ref.shape

    def norm(x):
        return x * lax.rsqrt(jnp.mean(x * x, axis=-1, keepdims=True) + EPS) * g_ref[...]

    @pl.when(pl.program_id(1) == 0)
    def _():
        xn_ref[...] = norm(x_ref[...]).astype(BF16)
        if chunks > 1:
            xnp_ref[...] = norm(xp_ref[...])

    def half(w_ref, wb_ref, s_ref, cw_ref, cb_ref, e_ref, ns_ref):
        wb = w_ref[...].astype(BF16)
        if emit:
            wb_ref[...] = wb
        u = _dot_nn(xn_ref[...], wb)
        cw = cw_ref[...]
        if nb > 1:
            e_ref[:, 8:8 + tc, :] = u.reshape(nb, tc, tn)
            e_ref[:, 6:8, :] = s_ref[...]
            ns_ref[...] = e_ref[:, 6 + tc:8 + tc, :]
            return (cb_ref[...] + e_ref[:, 6:6 + tc, :] * cw[0:1] + e_ref[:, 7:7 + tc, :] * cw[1:2]
                    + e_ref[:, 8:8 + tc, :] * cw[2:3]).reshape(tm, tn)
        ns_ref[0] = u[tm - 2:tm, :]
        prev = s_ref[0]
        if chunks > 1:
            up = _dot_nn(xnp_ref[...].astype(BF16), wb)
            prev = jnp.where(i % chunks == 0, prev, up[6:8, :])
        u1 = pltpu.roll(u, 1, 0)
        u2 = pltpu.roll(u, 2, 0)
        c = cb_ref[...] + u2 * cw[0:1] + u1 * cw[1:2] + u * cw[2:3]
        row = _iota((8, tn), 0)
        t1 = jnp.where(row == 0, prev[1:2, :], u1[0:8, :])
        t2 = jnp.where(row == 0, prev[0:1, :], jnp.where(row == 1, prev[1:2, :], u2[0:8, :]))
        top = cb_ref[...] + t2 * cw[0:1] + t1 * cw[1:2] + u[0:8, :] * cw[2:3]
        return jnp.concatenate([top, c[8:, :]], axis=0)

    wbv_ref, wbg_ref = (rest[0], rest[1]) if emit else (None, None)
    val = half(wv_ref, wbv_ref, sv_ref, cwv_ref, cbv_ref, ev_ref, nsv_ref)
    gate = half(wg_ref, wbg_ref, sg_ref, cwg_ref, cbg_ref, eg_ref, nsg_ref)
    o_ref[...] = (jax.nn.gelu(gate) * val).astype(o_ref.dtype)


def ffn_up(x, seq_len, state, norm_g, w_val, w_gate, gate_col, conv_w, conv_b, emit=False):
    m, d = x.shape
    f2 = conv_w.shape[1]
    f = f2 // 2
    tn = 256 if emit else 512
    tm = min(m, 1024)
    tc = min(seq_len, tm)
    nb = tm // tc
    chunks = seq_len // tc
    assert m % tm == 0 and tm % tc == 0 and seq_len % tc == 0 and tc % 8 == 0 and (nb == 1 or chunks == 1)
    assert f % tn == 0 and gate_col % tn == 0 and conv_w.shape[0] == 3 and (not emit or m == tm)
    ncb = f // tn
    goff = gate_col // tn
    r8 = tm // 8
    nseq = m // seq_len

    def wspec(rows, off):
        return pl.BlockSpec((rows, tn), lambda i, j: (0, j + off))

    def st(off):
        return pl.BlockSpec((nb, 2, tn), lambda i, j: (i // chunks, 0, j + off))

    tail = pl.BlockSpec((nb, 2, tn), lambda i, j: (i, 0, j))
    out_specs = [pl.BlockSpec((tm, tn), lambda i, j: (i, j)), tail, tail]
    out_shape = [jax.ShapeDtypeStruct((m, f), BF16), jax.ShapeDtypeStruct((nseq * chunks, 2, f), F32),
                 jax.ShapeDtypeStruct((nseq * chunks, 2, f), F32)]
    if emit:
        out_specs += [wspec(d, 0), wspec(d, 0)]
        out_shape += [jax.ShapeDtypeStruct((d, f), BF16), jax.ShapeDtypeStruct((d, f), BF16)]
    return pl.pallas_call(
        functools.partial(_ffn_up_kernel, nb=nb, tc=tc, chunks=chunks, emit=emit),
        grid=(m // tm, ncb),
        in_specs=[pl.BlockSpec((tm, d), lambda i, j: (i, 0)),
                  pl.BlockSpec((8, d), lambda i, j: (jnp.maximum(i * r8 - 1, 0), 0)),
                  pl.BlockSpec((1, d), lambda i, j: (0, 0)),
                  wspec(d, 0), wspec(d, goff), st(0), st(ncb),
                  wspec(3, 0), wspec(3, ncb), wspec(1, 0), wspec(1, ncb)],
        out_specs=out_specs,
        out_shape=out_shape,
        scratch_shapes=[pltpu.VMEM((tm, d), BF16), pltpu.VMEM((8, d), F32),
                        pltpu.VMEM((nb, tc + 8, tn), F32), pltpu.VMEM((nb, tc + 8, tn), F32)],
        compiler_params=_cp(("parallel", "arbitrary"), vmem_mib=56),
        name="ffn_up",
    )(x, x, norm_g.reshape(1, d), w_val, w_gate, state, state, conv_w, conv_w,
      conv_b.reshape(1, f2), conv_b.reshape(1, f2))


def _rglru_kernel(xa_ref, ga_ref, pv_ref, cs_ref, h0_ref, cw_ref, cb_ref, wa_ref, ba_ref,
                  wi_ref, bi_ref, lam_ref, ya_ref, hl_ref, e_ref, hc_ref, *, tc):
    t = pl.program_id(1)
    w = xa_ref.shape[-1]

    @pl.when(t == 0)
    def _():
        hc_ref[...] = h0_ref[0]
        e_ref[5:8, :] = cs_ref[0]

    @pl.when(t > 0)
    def _():
        e_ref[5:8, :] = pv_ref[0, 5:8, :]

    e_ref[8:8 + tc, :] = xa_ref[0]
    cw = cw_ref[...]
    xc = (cb_ref[...] + e_ref[5:5 + tc, :] * cw[0:1] + e_ref[6:6 + tc, :] * cw[1:2]
          + e_ref[7:7 + tc, :] * cw[2:3] + e_ref[8:8 + tc, :] * cw[3:4])
    xb = xc.astype(BF16)

    def gate(w_ref, b_ref):
        parts = [_dot_nn(xb[:, g * LRU_GROUP:(g + 1) * LRU_GROUP], w_ref[g])
                 for g in range(w // LRU_GROUP)]
        return jax.nn.sigmoid(jnp.concatenate(parts, axis=1) + b_ref[...])

    r = gate(wa_ref, ba_ref)
    i = gate(wi_ref, bi_ref)
    log_a = -LRU_C * r * _softplus(-lam_ref[...])
    a = jnp.exp(log_a)
    u = jnp.sqrt(-jnp.tanh(log_a) * (1.0 + a * a)) * (i * xc)

    row = _iota((tc, w), 0)
    s = 1
    while s < tc:
        a_sh = jnp.where(row >= s, pltpu.roll(a, s, 0), 1.0)
        u_sh = jnp.where(row >= s, pltpu.roll(u, s, 0), 0.0)
        u = u + a * u_sh
        a = a * a_sh
        s *= 2
    h = u + a * hc_ref[...]
    hlast = h[tc - 1:tc, :]
    hc_ref[...] = hlast
    hl_ref[0] = hlast
    ya_ref[0] = (h * jax.nn.gelu(ga_ref[0])).astype(ya_ref.dtype)


def rglru_branch(z, h0, conv_state, conv_w, conv_b, wa_bd, ba, wi_bd, bi, lam):
    b, t, _ = z.shape
    w = h0.shape[-1]
    tc = min(t, 256)
    assert t % tc == 0 and tc % 8 == 0 and conv_w.shape[0] == 4 and w % LRU_GROUP == 0
    r8 = tc // 8
    ng = w // LRU_GROUP
    vec = lambda: pl.BlockSpec((1, w), lambda i, j: (0, 0))
    ya, hl = pl.pallas_call(
        functools.partial(_rglru_kernel, tc=tc),
        grid=(b, t // tc),
        in_specs=[pl.BlockSpec((1, tc, w), lambda i, j: (i, j, 0)),
                  pl.BlockSpec((1, tc, w), lambda i, j: (i, j, 1)),
                  pl.BlockSpec((1, 8, w), lambda i, j: (i, jnp.maximum(j * r8 - 1, 0), 0)),
                  pl.BlockSpec((1, 3, w), lambda i, j: (i, 0, 0)),
                  pl.BlockSpec((1, 1, w), lambda i, j: (i, 0, 0)),
                  pl.BlockSpec((4, w), lambda i, j: (0, 0)),
                  vec(),
                  pl.BlockSpec((ng, LRU_GROUP, LRU_GROUP), lambda i, j: (0, 0, 0)),
                  vec(),
                  pl.BlockSpec((ng, LRU_GROUP, LRU_GROUP), lambda i, j: (0, 0, 0)),
                  vec(), vec()],
        out_specs=[pl.BlockSpec((1, tc, w), lambda i, j: (i, j, 0)),
                   pl.BlockSpec((1, 1, w), lambda i, j: (i, 0, 0))],
        out_shape=[jax.ShapeDtypeStruct((b, t, w), BF16), jax.ShapeDtypeStruct((b, 1, w), F32)],
        scratch_shapes=[pltpu.VMEM((tc + 8, w), F32), pltpu.VMEM((1, w), F32)],
        compiler_params=_cp(("parallel", "arbitrary")),
        name="rglru",
    )(z, z, z, conv_state, h0.reshape(b, 1, w), conv_w, conv_b.reshape(1, w), wa_bd,
      ba.reshape(1, w), wi_bd, bi.reshape(1, w), lam.reshape(1, w))
    return ya, hl.reshape(b, w)


def _regroup_blockdiag(wb):
    n, c, _ = wb.shape
    per = LRU_GROUP // c
    g = n // per
    eye = jnp.eye(per, dtype=wb.dtype)
    out = jnp.einsum("gacd,ab->gacbd", wb.reshape(g, per, c, c), eye)
    return out.reshape(g, LRU_GROUP, LRU_GROUP).astype(BF16)


def _bias_kernel(tab_ref, bt_ref, bs_ref, bx_ref, *, past, lpad, nkv, grp):
    h = pl.program_id(0)
    d0 = _iota((Q_TILE, Q_TILE), 0) - _iota((Q_TILE, Q_TILE), 1)
    bt_ref[0, 0] = _t5_bias(d0, tab_ref, h)
    bt_ref[0, 1] = _t5_bias(d0 + Q_TILE, tab_ref, h)
    t = bs_ref.shape[1]
    ds = past + _iota((t, lpad), 0) - _iota((t, lpad), 1)
    bs_ref[0] = _t5_bias(ds, tab_ref, h)
    col = _iota((t, past * nkv), 1)
    dx = past + _iota((t, past * nkv), 0) - col // nkv
    bx_ref[0] = jnp.where(col % nkv == h // grp, _t5_bias(dx, tab_ref, h), NEG_INF)


def bias_tiles(table, past, t_new, lpad, nkv):
    nh = table.shape[1]
    return pl.pallas_call(
        functools.partial(_bias_kernel, past=past, lpad=lpad, nkv=nkv, grp=nh // nkv),
        grid=(nh,),
        in_specs=[pl.BlockSpec(memory_space=pltpu.SMEM)],
        out_specs=[pl.BlockSpec((1, 2, Q_TILE, Q_TILE), lambda h: (h, 0, 0, 0)),
                   pl.BlockSpec((1, t_new, lpad), lambda h: (h, 0, 0)),
                   pl.BlockSpec((1, t_new, past * nkv), lambda h: (h, 0, 0))],
        out_shape=[jax.ShapeDtypeStruct((nh, 2, Q_TILE, Q_TILE), F32),
                   jax.ShapeDtypeStruct((nh, t_new, lpad), F32),
                   jax.ShapeDtypeStruct((nh, t_new, past * nkv), F32)],
        compiler_params=_cp(("arbitrary",)),
        name="t5_bias_tiles",
    )(table)


def _top_lanes(g, nvalid, k):
    lane = _iota(g.shape, 1).astype(F32)
    gg = jnp.where(lane < nvalid, g, NEG_INF)
    sel = jnp.zeros(g.shape, F32)
    for _ in range(k):
        mx = jnp.max(gg, axis=-1, keepdims=True)
        first = jnp.min(jnp.where(gg == mx, lane, float(g.shape[1])), axis=-1, keepdims=True)
        pick = (lane == first) & (mx > NEG_INF)
        sel = jnp.where(pick, 1.0, sel)
        gg = jnp.where(pick, NEG_INF, gg)
    return sel


def _moba_prompt_kernel(tab_ref, q_ref, k_ref, v_ref, qg_ref, kg_ref, bt_ref, y_ref, kn_ref, *, s):
    hp = pl.program_id(0)
    nb = s // MOBA_BLOCK
    dh = MOBA_HEAD_DIM
    scale = dh ** -0.5
    lane = _iota((1, LANES), 1)
    low = lane < dh

    def headnorm(x, g):
        x2 = x * x
        s0 = jnp.sum(jnp.where(low, x2, 0.0), axis=-1, keepdims=True)
        s1 = jnp.sum(jnp.where(low, 0.0, x2), axis=-1, keepdims=True)
        ms = jnp.where(low, s0, s1) * (1.0 / dh)
        return x * lax.rsqrt(ms + EPS) * g

    qn = headnorm(q_ref[0], qg_ref[...])
    kn = headnorm(k_ref[0], kg_ref[...])
    kn_ref[0] = kn
    kb = kn.astype(BF16)
    vb = v_ref[0].astype(BF16)
    blk_row = _iota((LANES, LANES), 0)
    kmat = jnp.zeros((LANES, LANES), F32)
    for j in range(nb):
        kmean = jnp.mean(kn[j * MOBA_BLOCK:(j + 1) * MOBA_BLOCK], axis=0, keepdims=True)
        kmat = jnp.where(blk_row == j, kmean, kmat)
    qi = _iota((Q_TILE, Q_TILE), 0)
    kj = _iota((Q_TILE, Q_TILE), 1)
    causal = qi >= kj

    near = []
    for hh in range(2):
        t31 = tab_ref[N_BUCKETS - 1, 2 * hp + hh]
        near.append((jnp.where(causal, bt_ref[hh, 0] - t31, NEG_INF), bt_ref[hh, 1] - t31))

    for c in range(nb):
        qc = qn[c * Q_TILE:(c + 1) * Q_TILE]
        nk = (c + 1) * MOBA_BLOCK
        outs = []
        for hh in range(2):
            hmask = low if hh == 0 else jnp.logical_not(low)
            qm = jnp.where(hmask, qc, 0.0)
            sc = _dot_nt((qm * scale).astype(BF16), kb[0:nk])
            if c > MOBA_TOPK:
                sel = _top_lanes(_dot3(_dot_nt, qm, kmat), c, MOBA_TOPK)
            pieces = []
            for j in range(c + 1):
                sj = sc[:, j * MOBA_BLOCK:(j + 1) * MOBA_BLOCK]
                if j == c:
                    sj = sj + near[hh][0]
                else:
                    if j == c - 1:
                        sj = sj + near[hh][1]
                    if c > MOBA_TOPK:
                        sj = jnp.where(sel[:, j:j + 1] > 0.0, sj, NEG_INF)
                pieces.append(sj)
            srow = jnp.concatenate(pieces, axis=1) if c else pieces[0]
            mx = jnp.max(srow, axis=-1, keepdims=True)
            p = jnp.exp(srow - mx)
            l = jnp.sum(p, axis=-1, keepdims=True)
            outs.append(_dot_nn(p.astype(BF16), vb[0:nk]) / l)
        y_ref[0, c * Q_TILE:(c + 1) * Q_TILE, :] = jnp.where(low, outs[0], outs[1]).astype(y_ref.dtype)


def moba_prompt(z, table, bt, q_gain, k_gain, w):
    b, s, _ = z.shape
    assert s % MOBA_BLOCK == 0 and Q_TILE == MOBA_BLOCK and s // MOBA_BLOCK <= LANES
    nhp = w // LANES
    g2 = lambda g: jnp.tile(g, LANES // MOBA_HEAD_DIM).reshape(1, LANES)
    col = lambda off: pl.BlockSpec((1, s, LANES), lambda hp, i: (i, 0, off * nhp + hp))
    gspec = pl.BlockSpec((1, LANES), lambda hp, i: (0, 0))
    return pl.pallas_call(
        functools.partial(_moba_prompt_kernel, s=s),
        grid=(nhp, b),
        in_specs=[pl.BlockSpec(memory_space=pltpu.SMEM), col(2), col(3), col(4), gspec, gspec,
                  pl.BlockSpec((2, 2, Q_TILE, Q_TILE), lambda hp, i: (hp, 0, 0, 0))],
        out_specs=[pl.BlockSpec((1, s, LANES), lambda hp, i: (i, 0, hp)),
                   pl.BlockSpec((1, s, LANES), lambda hp, i: (i, 0, hp))],
        out_shape=[jax.ShapeDtypeStruct((b, s, w), BF16), jax.ShapeDtypeStruct((b, s, w), F32)],
        compiler_params=_cp(("parallel", "parallel")),
        name="moba_prompt",
    )(table, z, z, z, g2(q_gain), g2(k_gain), bt)


def _group_onehot(ngroups_pad, width, gsize):
    return jnp.where(_iota((ngroups_pad, width), 0) == _iota((ngroups_pad, width), 1) // gsize,
                     1.0, 0.0).astype(BF16)


def _group_rmsnorm(x, gain, gsize):
    onehot = _group_onehot(LANES, x.shape[1], gsize)
    ssq = _dot_exact01(_dot_nt, x * x, onehot)
    rinv = lax.rsqrt(ssq * (1.0 / gsize) + EPS)
    return x * _dot_exact01(_dot_nn, rinv, onehot) * gain


def _moba_sample_kernel(pt_ref, q_ref, k_ref, v_ref, qg_ref, kg_ref, bs_ref, *rest, npages, t):
    kp_refs = rest[:npages]
    vp_refs = rest[npages:2 * npages]
    y_ref, kn_ref = rest[2 * npages:]
    w = q_ref.shape[-1]
    dh = MOBA_HEAD_DIM
    nh = w // dh
    rows = nh * t
    ppb = MOBA_BLOCK // PAGE_SIZE
    nbp = npages // ppb
    past = npages * PAGE_SIZE
    scale = dh ** -0.5

    qn = _group_rmsnorm(q_ref[0], qg_ref[...], dh)
    kn = _group_rmsnorm(k_ref[0], kg_ref[...], dh)
    kn_ref[0] = kn
    own = _iota((rows, w), 0) // t == _iota((rows, w), 1) // dh
    qbd = jnp.where(own, jnp.concatenate([qn] * nh, axis=0), 0.0)
    qbb = qbd.astype(BF16)
    pad_rows = lambda x: jnp.concatenate([x, jnp.zeros((LANES - t, w), x.dtype)], axis=0)

    pieces, sums = [], []
    for p in range(npages):
        kpt = kp_refs[p][0]
        sums.append(jnp.sum(kpt, axis=1, keepdims=True))
        pieces.append(_dot_nn(qbb, kpt.astype(BF16)))
    pieces.append(_dot_nt(qbb, pad_rows(kn).astype(BF16)))
    lpad = past + LANES
    srow = jnp.concatenate(pieces, axis=1) * scale + bs_ref[...]

    blk_lane = _iota((w, LANES), 1)
    kmat = jnp.zeros((w, LANES), F32)
    for j in range(nbp):
        kmat = jnp.where(blk_lane == j, sum(sums[j * ppb:(j + 1) * ppb]) * (1.0 / MOBA_BLOCK), kmat)
    sel = _top_lanes(_dot3(_dot_nn, qbd, kmat), nbp, MOBA_TOPK)
    col = _iota((rows, lpad), 1)
    jrow = _iota((rows, lpad), 0) % t
    keep = (col >= past) & (col - past <= jrow)
    for j in range(nbp):
        inblk = (col >= j * MOBA_BLOCK) & (col < (j + 1) * MOBA_BLOCK)
        keep = keep | (inblk & (sel[:, j:j + 1] > 0.0))
    srow = jnp.where(keep, srow, NEG_INF)
    mx = jnp.max(srow, axis=-1, keepdims=True)
    pr = jnp.exp(srow - mx)
    l = jnp.sum(pr, axis=-1, keepdims=True)
    pb = pr.astype(BF16)
    acc = _dot_nn(pb[:, past:], pad_rows(v_ref[0]).astype(BF16))
    for p in range(npages):
        acc = acc + _dot_nt(pb[:, p * PAGE_SIZE:(p + 1) * PAGE_SIZE], vp_refs[p][0].astype(BF16))
    acc = acc / l
    lane_head = _iota((t, w), 1) // dh
    y = jnp.zeros((t, w), F32)
    for h in range(nh):
        y = jnp.where(lane_head == h, acc[h * t:(h + 1) * t, :], y)
    y_ref[0] = y.astype(y_ref.dtype)


def moba_sample(z, pool_k, pool_v, page_table, bs, q_gain, k_gain, w):
    db, t, _ = z.shape
    npages = page_table.shape[1]
    nh = w // MOBA_HEAD_DIM
    assert t == 8 and nh * t == LANES and npages % (MOBA_BLOCK // PAGE_SIZE) == 0
    lpad = npages * PAGE_SIZE + LANES
    gt = lambda g: jnp.tile(g, nh).reshape(1, w)
    col = lambda off: pl.BlockSpec((1, t, w), lambda i, pt: (i, 0, off))
    gspec = pl.BlockSpec((1, w), lambda i, pt: (0, 0))
    page = lambda p: pl.BlockSpec((1, w, PAGE_SIZE), lambda i, pt: (pt[i, p], 0, 0))
    grid_spec = pltpu.PrefetchScalarGridSpec(
        num_scalar_prefetch=1,
        grid=(db,),
        in_specs=[col(2), col(3), col(4), gspec, gspec,
                  pl.BlockSpec((nh * t, lpad), lambda i, pt: (0, 0))]
                 + [page(p) for p in range(npages)] * 2,
        out_specs=[pl.BlockSpec((1, t, w), lambda i, pt: (i, 0, 0)),
                   pl.BlockSpec((1, t, w), lambda i, pt: (i, 0, 0))],
    )
    return pl.pallas_call(
        functools.partial(_moba_sample_kernel, npages=npages, t=t),
        grid_spec=grid_spec,
        out_shape=[jax.ShapeDtypeStruct((db, t, w), BF16), jax.ShapeDtypeStruct((db, t, w), F32)],
        compiler_params=_cp(("arbitrary",), vmem_mib=56),
        name="moba_sample",
    )(page_table, z, z, z, gt(q_gain), gt(k_gain), bs.reshape(nh * t, lpad),
      *([pool_k] * npages), *([pool_v] * npages))


def _kv_prep_kernel(k_ref, v_ref, g_ref, kn_ref, knb_ref, vb_ref):
    x = k_ref[...]
    ms = jnp.mean(x * x, axis=-1, keepdims=True)
    kn = x * lax.rsqrt(ms + EPS) * g_ref[...]
    kn_ref[...] = kn
    knb_ref[...] = kn.astype(BF16)
    vb_ref[...] = v_ref[...].astype(BF16)


def kv_prep(z, gain, k_blk0, v_blk0, nheads):
    m = z.shape[0]
    tm = min(m, 1024)
    assert m % tm == 0
    out = pl.BlockSpec((tm, LANES), lambda i, h: (i, h))
    return pl.pallas_call(
        _kv_prep_kernel,
        grid=(m // tm, nheads),
        in_specs=[pl.BlockSpec((tm, LANES), lambda i, h: (i, k_blk0 + h)),
                  pl.BlockSpec((tm, LANES), lambda i, h: (i, v_blk0 + h)),
                  pl.BlockSpec((1, LANES), lambda i, h: (0, 0))],
        out_specs=[out, out, out],
        out_shape=[jax.ShapeDtypeStruct((m, nheads * LANES), F32),
                   jax.ShapeDtypeStruct((m, nheads * LANES), BF16),
                   jax.ShapeDtypeStruct((m, nheads * LANES), BF16)],
        compiler_params=_cp(("parallel", "parallel")),
        name="kv_prep",
    )(z, z, gain.reshape(1, LANES))


def _index_prompt_kernel(qi_ref, qw_ref, kk_ref, o_ref, *, s):
    c = pl.program_id(1)
    lane = _iota((1, LANES), 1)
    low = lane < IDX_DIM
    qw = qw_ref[0]

    def tile(cc):
        nk = (cc + 1) * Q_TILE
        kk = jnp.where(low, kk_ref[0, 0:nk, :], 0.0)
        kk_hi = pltpu.roll(kk, IDX_DIM, 1)
        acc = jnp.zeros((Q_TILE, nk), F32)
        for ih in range(IDX_HEADS):
            blk = qi_ref[0, :, (ih // 2) * LANES:(ih // 2 + 1) * LANES]
            half_low = ih % 2 == 0
            qm = jnp.where(low if half_low else jnp.logical_not(low), blk, 0.0)
            sc = _dot3(_dot_nt, qm, kk if half_low else kk_hi)
            wcol = qw[:, IDX_DIM + ih:IDX_DIM + ih + 1] * (IDX_HEADS ** -0.5)
            acc = acc + wcol * jnp.maximum(sc * (IDX_DIM ** -0.5), 0.0)
        qpos = cc * Q_TILE + _iota((Q_TILE, nk), 0)
        o_ref[0, :, 0:nk] = jnp.where(_iota((Q_TILE, nk), 1) <= qpos, acc, NEG_INF)
        if nk < s:
            o_ref[0, :, nk:] = jnp.full((Q_TILE, s - nk), NEG_INF, F32)

    for cc in range(s // Q_TILE):
        pl.when(c == cc)(functools.partial(tile, cc))


def index_prompt(z, qi_blk, kw_blk):
    b, s, _ = z.shape
    assert s % Q_TILE == 0
    return pl.pallas_call(
        functools.partial(_index_prompt_kernel, s=s),
        grid=(b, s // Q_TILE),
        in_specs=[pl.BlockSpec((1, Q_TILE, IDX_HEADS * IDX_DIM), lambda i, c: (i, c, qi_blk)),
                  pl.BlockSpec((1, Q_TILE, LANES), lambda i, c: (i, c, kw_blk)),
                  pl.BlockSpec((1, s, LANES), lambda i, c: (i, 0, kw_blk))],
        out_specs=pl.BlockSpec((1, Q_TILE, s), lambda i, c: (i, c, 0)),
        out_shape=jax.ShapeDtypeStruct((b, s, s), F32),
        compiler_params=_cp(("parallel", "parallel")),
        name="index_prompt",
    )(z, z, z)


def _index_sample_kernel(pt_ref, qi_ref, qw_ref, *rest, npages, t):
    kp_refs = rest[:npages]
    o_ref = rest[npages]
    qi = qi_ref[0]
    qw = qw_ref[0]
    q2 = jnp.concatenate([qi[:, ih * IDX_DIM:(ih + 1) * IDX_DIM] for ih in range(IDX_HEADS)], axis=0)
    wcol = jnp.concatenate([qw[:, IDX_DIM + ih:IDX_DIM + ih + 1] for ih in range(IDX_HEADS)], axis=0)
    wcol = wcol * (IDX_HEADS ** -0.5)
    knew = jnp.concatenate([qw[:, 0:IDX_DIM], jnp.zeros((LANES - t, IDX_DIM), F32)], axis=0)

    def scores(dot, keys):
        sc = wcol * jnp.maximum(_dot3(dot, q2, keys) * (IDX_DIM ** -0.5), 0.0)
        return sum(sc[ih * t:(ih + 1) * t, :] for ih in range(IDX_HEADS))

    pieces = [scores(_dot_nn, kp_refs[p][0]) for p in range(npages)]
    new = scores(_dot_nt, knew)
    jj = _iota((t, LANES), 0)
    cc = _iota((t, LANES), 1)
    pieces.append(jnp.where(cc <= jj, new, NEG_INF))
    o_ref[0] = jnp.concatenate(pieces, axis=1)


def index_sample(z, pool_ik, page_table, qi_blk, kw_blk):
    db, t, _ = z.shape
    npages = page_table.shape[1]
    lpad = npages * PAGE_SIZE + LANES
    grid_spec = pltpu.PrefetchScalarGridSpec(
        num_scalar_prefetch=1,
        grid=(db,),
        in_specs=[pl.BlockSpec((1, t, IDX_HEADS * IDX_DIM), lambda i, pt: (i, 0, qi_blk)),
                  pl.BlockSpec((1, t, LANES), lambda i, pt: (i, 0, kw_blk))]
                 + [pl.BlockSpec((1, IDX_DIM, PAGE_SIZE), (lambda i, pt, p=p: (pt[i, p], 0, 0)))
                    for p in range(npages)],
        out_specs=pl.BlockSpec((1, t, lpad), lambda i, pt: (i, 0, 0)),
    )
    return pl.pallas_call(
        functools.partial(_index_sample_kernel, npages=npages, t=t),
        grid_spec=grid_spec,
        out_shape=jax.ShapeDtypeStruct((db, t, lpad), F32),
        compiler_params=_cp(("arbitrary",)),
        name="index_sample",
    )(page_table, z, z, *([pool_ik] * npages))


def _topk_mask_kernel(x_ref, o_ref, *, topk, period, base, tr, additive, ncausal):
    l = x_ref.shape[1]
    on, off = (0.0, NEG_INF) if additive else (1.0, 0.0)

    def body(nk):
        rowg = pl.program_id(0) * tr + _iota((tr, nk), 0)
        valid = _iota((tr, nk), 1) <= base + rowg % period
        bits = pltpu.bitcast(x_ref[:, 0:nk], I32)
        key = jnp.where(bits < 0, bits ^ 0x7FFFFFFF, bits)
        key = jnp.where(valid, key, INT_MIN)

        def step(i, lo):
            cand = lo + lax.shift_left(jnp.int32(1), 31 - i)
            cnt = jnp.sum(jnp.where(key >= cand, 1.0, 0.0), axis=-1, keepdims=True)
            return jnp.where(cnt >= topk, cand, lo)

        thr = lax.fori_loop(0, 32, step, jnp.full((tr, 1), INT_MIN, I32))
        gt = key > thr
        need = topk - jnp.sum(jnp.where(gt, 1.0, 0.0), axis=-1, keepdims=True)
        eq = jnp.where(key == thr, 1.0, 0.0)
        upper = jnp.where(_iota((LANES, LANES), 0) < _iota((LANES, LANES), 1), 1.0, 0.0).astype(BF16)
        before = jnp.zeros((tr, 1), F32)
        for j in range(nk // LANES):
            e = eq[:, j * LANES:(j + 1) * LANES]
            rank = before + _dot_nn(e.astype(BF16), upper)
            pick = gt[:, j * LANES:(j + 1) * LANES] | ((e > 0.0) & (rank < need))
            pick = pick & valid[:, j * LANES:(j + 1) * LANES]
            o_ref[:, j * LANES:(j + 1) * LANES] = jnp.where(pick, on, off)
            before = before + jnp.sum(e, axis=-1, keepdims=True)
        if nk < l:
            o_ref[:, nk:] = jnp.full((tr, l - nk), off, F32)

    if ncausal > 1:
        c = pl.program_id(0) % ncausal
        for cc in range(ncausal):
            pl.when(c == cc)(functools.partial(body, (cc + 1) * tr))
    else:
        body(l)


def topk_mask(scores, topk, period, base, additive):
    r, l = scores.shape
    tr = min(r, 256)
    assert r % tr == 0 and l % LANES == 0
    ncausal = period // tr if (base == 0 and period == l and period % tr == 0) else 1
    return pl.pallas_call(
        functools.partial(_topk_mask_kernel, topk=topk, period=period, base=base, tr=tr, additive=additive,
                          ncausal=ncausal),
        grid=(r // tr,),
        in_specs=[pl.BlockSpec((tr, l), lambda i: (i, 0))],
        out_specs=pl.BlockSpec((tr, l), lambda i: (i, 0)),
        out_shape=jax.ShapeDtypeStruct((r, l), F32),
        compiler_params=_cp(("parallel",)),
        name="topk_mask",
    )(scores)


def _dsa_prompt_kernel(tab_ref, q_ref, k_ref, v_ref, m_ref, bt_ref, qg_ref, y_ref, *, s):
    c = pl.program_id(1)
    g = pl.program_id(2)
    dh = DSA_HEAD_DIM
    grp = q_ref.shape[-1] // dh
    qbs, t31s = [], []
    for hh in range(grp):
        q = q_ref[0, :, hh * dh:(hh + 1) * dh]
        qn = q * lax.rsqrt(jnp.mean(q * q, axis=-1, keepdims=True) + EPS) * qg_ref[...]
        qbs.append((qn * (dh ** -0.5)).astype(BF16))
        t31s.append(tab_ref[N_BUCKETS - 1, g * grp + hh])

    def tile(cc):
        nk = (cc + 1) * Q_TILE
        madd = m_ref[0, :, 0:nk]
        for hh in range(grp):
            sc = _dot_nt(qbs[hh], k_ref[0, 0:nk, :]) + madd
            pieces = [sc[:, 0:(cc - 1) * Q_TILE]] if cc > 1 else []
            if cc > 0:
                pieces.append(sc[:, (cc - 1) * Q_TILE:cc * Q_TILE] + (bt_ref[hh, 1] - t31s[hh]))
            pieces.append(sc[:, cc * Q_TILE:nk] + (bt_ref[hh, 0] - t31s[hh]))
            srow = jnp.concatenate(pieces, axis=1) if cc else pieces[0]
            mx = jnp.max(srow, axis=-1, keepdims=True)
            p = jnp.exp(srow - mx)
            l = jnp.sum(p, axis=-1, keepdims=True)
            y_ref[0, :, hh * dh:(hh + 1) * dh] = (_dot_nn(p.astype(BF16), v_ref[0, 0:nk, :]) / l
                                                  ).astype(y_ref.dtype)

    for cc in range(s // Q_TILE):
        pl.when(c == cc)(functools.partial(tile, cc))


def dsa_prompt(z, knb, vb, madd, table, bt, q_gain, nq, nkv):
    b, s, _ = z.shape
    grp = nq // nkv
    return pl.pallas_call(
        functools.partial(_dsa_prompt_kernel, s=s),
        grid=(b, s // Q_TILE, nkv),
        in_specs=[pl.BlockSpec(memory_space=pltpu.SMEM),
                  pl.BlockSpec((1, Q_TILE, grp * LANES), lambda i, c, g: (i, c, g)),
                  pl.BlockSpec((1, s, LANES), lambda i, c, g: (i, 0, g)),
                  pl.BlockSpec((1, s, LANES), lambda i, c, g: (i, 0, g)),
                  pl.BlockSpec((1, Q_TILE, s), lambda i, c, g: (i, c, 0)),
                  pl.BlockSpec((grp, 2, Q_TILE, Q_TILE), lambda i, c, g: (g, 0, 0, 0)),
                  pl.BlockSpec((1, LANES), lambda i, c, g: (0, 0))],
        out_specs=pl.BlockSpec((1, Q_TILE, grp * LANES), lambda i, c, g: (i, c, g)),
        out_shape=jax.ShapeDtypeStruct((b, s, nq * LANES), BF16),
        compiler_params=_cp(("parallel", "parallel", "arbitrary")),
        name="dsa_prompt",
    )(table, z, knb, vb, madd, bt, q_gain.reshape(1, LANES))


def _dsa_sample_kernel(pt_ref, q_ref, kn_ref, v_ref, m_ref, bs_ref, bx_ref, qg_ref, *rest,
                       npages, t, nq, nkv):
    kp_refs = rest[:npages]
    vp_refs = rest[npages:2 * npages]
    y_ref = rest[2 * npages]
    dh = DSA_HEAD_DIM
    grp = nq // nkv
    wkv = nkv * dh
    past = npages * PAGE_SIZE
    pcols = PAGE_SIZE * nkv
    scale = dh ** -0.5
    q = q_ref[0]
    g = qg_ref[...]
    zero = jnp.zeros((t, dh), F32)
    qrows, qstruct = [], []
    for h in range(nq):
        qh = q[:, h * dh:(h + 1) * dh]
        qh = qh * lax.rsqrt(jnp.mean(qh * qh, axis=-1, keepdims=True) + EPS) * g
        qrows.append(qh)
        qstruct.append(jnp.concatenate([qh if kv == h // grp else zero for kv in range(nkv)], axis=1))
    qall = jnp.concatenate(qrows, axis=0).astype(BF16)
    qsb = jnp.concatenate(qstruct, axis=0).astype(BF16)
    pad_rows = lambda x: jnp.concatenate([x, jnp.zeros((LANES - t, wkv), x.dtype)], axis=0)
    keep = jnp.concatenate([m_ref[0]] * nq, axis=0)
    keepb = keep.astype(BF16)
    spread = jnp.where(_iota((PAGE_SIZE, pcols), 0) == _iota((PAGE_SIZE, pcols), 1) // nkv,
                       1.0, 0.0).astype(BF16)

    pieces = []
    for p in range(npages):
        sp = _dot_nt(qall, kp_refs[p][...].astype(BF16)) * scale + bx_ref[:, p * pcols:(p + 1) * pcols]
        kexp = _dot_nn(keepb[:, p * PAGE_SIZE:(p + 1) * PAGE_SIZE], spread)
        pieces.append(jnp.where(kexp > 0.5, sp, NEG_INF))
    loc = _dot_nt(qsb, pad_rows(kn_ref[0]).astype(BF16)) * scale + bs_ref[:, past:]
    pieces.append(jnp.where(keep[:, past:] > 0.5, loc, NEG_INF))
    srow = jnp.concatenate(pieces, axis=1)
    mx = jnp.max(srow, axis=-1, keepdims=True)
    pr = jnp.exp(srow - mx)
    l = jnp.sum(pr, axis=-1, keepdims=True)
    pb = pr.astype(BF16)
    accl = _dot_nn(pb[:, npages * pcols:], pad_rows(v_ref[0]).astype(BF16))
    row_grp = _iota((nq * t, dh), 0) // (t * grp)
    acc = jnp.zeros((nq * t, dh), F32)
    for kv in range(nkv):
        acc = jnp.where(row_grp == kv, accl[:, kv * dh:(kv + 1) * dh], acc)
    for p in range(npages):
        acc = acc + _dot_nn(pb[:, p * pcols:(p + 1) * pcols], vp_refs[p][...].astype(BF16))
    acc = acc / l
    y_ref[0] = jnp.concatenate([acc[h * t:(h + 1) * t, :] for h in range(nq)], axis=1).astype(y_ref.dtype)


def dsa_sample(z, kn, mask, pool_k, pool_v, page_table, bs, bx, q_gain, nq, nkv, v_blk):
    db, t, _ = z.shape
    npages = page_table.shape[1]
    wkv = nkv * DSA_HEAD_DIM
    past = npages * PAGE_SIZE
    lpad = past + LANES
    assert nq * t == LANES
    page = lambda p: pl.BlockSpec((PAGE_SIZE * nkv, DSA_HEAD_DIM), lambda i, pt: (pt[i, p], 0))
    grid_spec = pltpu.PrefetchScalarGridSpec(
        num_scalar_prefetch=1,
        grid=(db,),
        in_specs=[pl.BlockSpec((1, t, nq * DSA_HEAD_DIM), lambda i, pt: (i, 0, 0)),
                  pl.BlockSpec((1, t, wkv), lambda i, pt: (i, 0, 0)),
                  pl.BlockSpec((1, t, wkv), lambda i, pt: (i, 0, v_blk)),
                  pl.BlockSpec((1, t, lpad), lambda i, pt: (i, 0, 0)),
                  pl.BlockSpec((nq * t, lpad), lambda i, pt: (0, 0)),
                  pl.BlockSpec((nq * t, past * nkv), lambda i, pt: (0, 0)),
                  pl.BlockSpec((1, LANES), lambda i, pt: (0, 0))]
                 + [page(p) for p in range(npages)] * 2,
        out_specs=pl.BlockSpec((1, t, nq * DSA_HEAD_DIM), lambda i, pt: (i, 0, 0)),
    )
    return pl.pallas_call(
        functools.partial(_dsa_sample_kernel, npages=npages, t=t, nq=nq, nkv=nkv),
        grid_spec=grid_spec,
        out_shape=jax.ShapeDtypeStruct((db, t, nq * DSA_HEAD_DIM), BF16),
        compiler_params=_cp(("arbitrary",)),
        name="dsa_sample",
    )(page_table, z, kn, z, mask, bs.reshape(nq * t, lpad), bx.reshape(nq * t, past * nkv),
      q_gain.reshape(1, LANES), *([pool_k] * npages), *([pool_v] * npages))


def _conv_ffn(x2, seq_len, state, norm_g, weights, conv_w, conv_b):
    emit = len(weights) == 2
    if emit:
        w_val = w_gate = weights[0]
        gate_col = weights[0].shape[1] // 2
    else:
        w_val, w_gate = weights[:2]
        gate_col = 0
    mid, ns_val, ns_gate, *wb = ffn_up(x2, seq_len, state, norm_g, w_val, w_gate, gate_col, conv_w, conv_b, emit)
    y = matmul_residual(mid, weights[-1], x2, emit)
    if emit:
        y, w_down_b = y
        wb.append(w_down_b)
    step = ns_val.shape[0] * seq_len // x2.shape[0]
    return y, jnp.concatenate([ns_val[step - 1::step], ns_gate[step - 1::step]], axis=-1), tuple(wb)


def kernel(x_prompt, x_sample, state_rglru_h, state_rglru_conv, cache_moba_k, cache_moba_v, state_ffn0_conv, cache_dsa_k, cache_dsa_v, cache_dsa_idx_k, state_ffn1_conv, page_table, rel_bias_table, l0_norm_g, l0_w_in, l0_conv_w, l0_conv_b, l0_gate_a_w, l0_gate_a_b, l0_gate_i_w, l0_gate_i_b, l0_lambda, l0_q_norm_g, l0_k_norm_g, l0_w_out, ffn0_norm_g, ffn0_w_up, ffn0_conv_w, ffn0_conv_b, ffn0_w_down, l1_norm_g, l1_w_in, l1_q_norm_g, l1_k_norm_g, l1_w_out, ffn1_norm_g, ffn1_w_up, ffn1_conv_w, ffn1_conv_b, ffn1_w_down):
    bp, s, d = x_prompt.shape
    db, t, _ = x_sample.shape
    w = state_rglru_h.shape[-1]
    f2 = ffn0_w_up.shape[1]
    n_phys = cache_moba_k.shape[0]
    npages = page_table.shape[1]
    past = npages * PAGE_SIZE
    lpad = past + LANES
    nmh = cache_moba_k.shape[2]
    nkv = cache_dsa_k.shape[2]
    nq = l1_w_out.shape[0] // DSA_HEAD_DIM
    assert cache_moba_k.shape[3] == MOBA_HEAD_DIM and cache_dsa_k.shape[3] == DSA_HEAD_DIM
    assert cache_moba_k.shape[1] == PAGE_SIZE and nmh * MOBA_HEAD_DIM == w
    assert rel_bias_table.shape == (N_BUCKETS, nmh) and nq == nmh
    assert l0_w_in.shape[1] == 5 * w

    xp = x_prompt.reshape(bp * s, d)
    xs = x_sample.reshape(db * t, d)
    bt, bsamp, bsx = bias_tiles(rel_bias_table, past, t, lpad, nkv)

    wa_bd = _regroup_blockdiag(l0_gate_a_w)
    wi_bd = _regroup_blockdiag(l0_gate_i_w)
    pool_mk = cache_moba_k.transpose(0, 2, 3, 1).reshape(n_phys, w, PAGE_SIZE)
    pool_mv = cache_moba_v.transpose(0, 2, 3, 1).reshape(n_phys, w, PAGE_SIZE)

    def even(x2, b, tt, h0, conv_state, attend, w_in, w_out, emit):
        z = norm_matmul(x2, l0_norm_g, w_in, emit)
        z, w_in_b = z if emit else (z, None)
        z = z.reshape(b, tt, 5 * w)
        ya, h_last = rglru_branch(z, h0, conv_state, l0_conv_w, l0_conv_b, wa_bd, l0_gate_a_b,
                                  wi_bd, l0_gate_i_b, l0_lambda)
        yb, kn = attend(z)
        mix = jnp.concatenate([ya, yb], axis=-1).reshape(b * tt, 2 * w)
        out = matmul_residual(mix, w_out, x2, emit)
        out, w_out_b = out if emit else (out, None)
        new_conv = z[:, tt - (l0_conv_w.shape[0] - 1):, :w]
        v_rows = z[:, :, 4 * w:].reshape(b, tt, nmh, MOBA_HEAD_DIM)
        return (out, h_last, new_conv, kn.reshape(b, tt, nmh, MOBA_HEAD_DIM), v_rows), (w_in_b, w_out_b)

    (xs, h_s, conv_s, mk_s, mv_s), (w_in0_b, w_out0_b) = even(
        xs, db, t, state_rglru_h, state_rglru_conv,
        lambda z: moba_sample(z, pool_mk, pool_mv, page_table, bsamp, l0_q_norm_g, l0_k_norm_g, w),
        l0_w_in, l0_w_out, True)
    (xp, h_p, conv_p, mk_p, mv_p), _ = even(
        xp, bp, s, jnp.zeros((bp, w), F32), jnp.zeros((bp, l0_conv_w.shape[0] - 1, w), F32),
        lambda z: moba_prompt(z, rel_bias_table, bt, l0_q_norm_g, l0_k_norm_g, w),
        w_in0_b, w_out0_b, False)

    xs, f0_s, ffn0_wb = _conv_ffn(xs, t, state_ffn0_conv, ffn0_norm_g, (ffn0_w_up, ffn0_w_down),
                                  ffn0_conv_w, ffn0_conv_b)
    xp, f0_p, _ = _conv_ffn(xp, s, jnp.zeros((bp, ffn0_conv_w.shape[0] - 1, f2), F32), ffn0_norm_g, ffn0_wb,
                            ffn0_conv_w, ffn0_conv_b)

    wq = nq * DSA_HEAD_DIM
    wkv = nkv * DSA_HEAD_DIM
    wqi = IDX_HEADS * IDX_DIM
    n_in = l1_w_in.shape[1]
    assert n_in == wq + 2 * wkv + wqi + IDX_DIM + IDX_HEADS
    n_pad = -(-n_in // 512) * 512
    w_in1 = jnp.pad(l1_w_in, ((0, 0), (0, n_pad - n_in)))
    k_blk0 = wq // LANES
    v_blk0 = (wq + wkv) // LANES
    qi_off = wq + 2 * wkv
    assert qi_off % wqi == 0 and (qi_off + wqi) % LANES == 0 and (wq + wkv) % wkv == 0
    qi_blk = qi_off // wqi
    kw_blk = (qi_off + wqi) // LANES
    pool_dk = cache_dsa_k.reshape(n_phys * PAGE_SIZE * nkv, DSA_HEAD_DIM)
    pool_dv = cache_dsa_v.reshape(n_phys * PAGE_SIZE * nkv, DSA_HEAD_DIM)
    pool_ik = cache_dsa_idx_k.transpose(0, 2, 1)

    def odd(x2, b, tt, attend, w_in, w_out, emit):
        z2 = norm_matmul(x2, l1_norm_g, w_in, emit)
        z2, w_in_b = z2 if emit else (z2, None)
        kn, knb, vb = kv_prep(z2, l1_k_norm_g, k_blk0, v_blk0, nkv)
        z = z2.reshape(b, tt, n_pad)
        y = attend(z, kn.reshape(b, tt, wkv), knb.reshape(b, tt, wkv), vb.reshape(b, tt, wkv))
        out = matmul_residual(y.reshape(b * tt, wq), w_out, x2, emit)
        out, w_out_b = out if emit else (out, None)
        v_rows = z[:, :, wq + wkv:wq + 2 * wkv].reshape(b, tt, nkv, DSA_HEAD_DIM)
        ki_rows = z[:, :, qi_off + wqi:qi_off + wqi + IDX_DIM]
        return (out, kn.reshape(b, tt, nkv, DSA_HEAD_DIM), v_rows, ki_rows), (w_in_b, w_out_b)

    def attend_prompt(z, kn, knb, vb):
        isc = index_prompt(z, qi_blk, kw_blk)
        mask = topk_mask(isc.reshape(bp * s, s), min(DSA_TOPK, s // 4), s, 0, True).reshape(bp, s, s)
        return dsa_prompt(z, knb, vb, mask, rel_bias_table, bt, l1_q_norm_g, nq, nkv)

    def attend_sample(z, kn, knb, vb):
        isc = index_sample(z, pool_ik, page_table, qi_blk, kw_blk)
        mask = topk_mask(isc.reshape(db * t, lpad), min(DSA_TOPK, (past + t) // 4), t, past, False)
        return dsa_sample(z, kn, mask.reshape(db, t, lpad), pool_dk, pool_dv, page_table, bsamp, bsx,
                          l1_q_norm_g, nq, nkv, (wq + wkv) // wkv)

    (xs, dk_s, dv_s, di_s), (w_in1_b, w_out1_b) = odd(xs, db, t, attend_sample, w_in1, l1_w_out, True)
    (xp, dk_p, dv_p, di_p), _ = odd(xp, bp, s, attend_prompt, w_in1_b, w_out1_b, False)

    xs, f1_s, ffn1_wb = _conv_ffn(xs, t, state_ffn1_conv, ffn1_norm_g, (ffn1_w_up, ffn1_w_down),
                                  ffn1_conv_w, ffn1_conv_b)
    xp, f1_p, _ = _conv_ffn(xp, s, jnp.zeros((bp, ffn1_conv_w.shape[0] - 1, f2), F32), ffn1_norm_g, ffn1_wb,
                            ffn1_conv_w, ffn1_conv_b)

    return (xp.reshape(bp, s, d), xs.reshape(db, t, d), h_p, h_s, conv_p, conv_s, mk_p, mk_s, mv_p, mv_s,
            f0_p, f0_s, dk_p, dk_s, dv_p, dv_s, di_p, di_s, f1_p, f1_s)
```

```python
import functools
import math

import numpy as np
import jax
import jax.numpy as jnp
from jax import lax
from jax.experimental import pallas as pl
from jax.experimental.pallas import tpu as pltpu

F32 = jnp.float32
BF16 = jnp.bfloat16
I32 = jnp.int32
NEG_INF = float("-inf")
INT_MIN = -(2 ** 31)

EPS = 1e-6
LRU_C = 8.0
LRU_GROUP = 256
MOBA_BLOCK = 256
MOBA_TOPK = 3
MOBA_HEAD_DIM = 64
DSA_HEAD_DIM = 128
DSA_TOPK = 256
IDX_DIM = 64
IDX_HEADS = 8
PAGE_SIZE = 128
N_BUCKETS = 32
MAX_DISTANCE = 128
Q_TILE = 256
LANES = 128
MIB = 1024 * 1024


def _t5_thresholds():
    exact = N_BUCKETS // 2
    n = np.arange(0, 4 * MAX_DISTANCE)
    nf = np.maximum(n, 1).astype(np.float32)
    large = exact + (np.log(nf / np.float32(exact)) / np.float32(math.log(MAX_DISTANCE / exact))
                     * np.float32(N_BUCKETS - exact)).astype(np.int32)
    bucket = np.where(n < exact, n, np.minimum(large, N_BUCKETS - 1))
    assert np.all(np.diff(bucket) >= 0) and bucket[-1] == N_BUCKETS - 1
    return [int(np.argmax(bucket >= b)) for b in range(N_BUCKETS)]


T5_THR = _t5_thresholds()


def _cp(sem, vmem_mib=48):
    return pltpu.CompilerParams(dimension_semantics=sem, vmem_limit_bytes=vmem_mib * MIB)


def _iota(shape, dim):
    return lax.broadcasted_iota(I32, shape, dim)


def _dot_nt(a, b):
    return lax.dot_general(a, b, (((1,), (1,)), ((), ())), preferred_element_type=F32)


def _dot_nn(a, b):
    return jnp.dot(a, b, preferred_element_type=F32)


def _split2(x):
    hi = x.astype(BF16)
    lo = (x - hi.astype(F32)).astype(BF16)
    return hi, lo


def _split3(x):
    hi = x.astype(BF16)
    r = x - hi.astype(F32)
    mid = r.astype(BF16)
    lo = (r - mid.astype(F32)).astype(BF16)
    return hi, mid, lo


def _dot3(dot, a, b):
    ah, al = _split2(a)
    bh, bl = _split2(b)
    return dot(ah, bh) + (dot(ah, bl) + dot(al, bh))


def _dot_exact01(dot, a, b01):
    h, m, l = _split3(a)
    return dot(h, b01) + (dot(m, b01) + dot(l, b01))


def _t5_bias(dist, tab_ref, h):
    acc = jnp.full(dist.shape, tab_ref[0, h], F32)
    for b in range(1, N_BUCKETS):
        acc = jnp.where(dist >= T5_THR[b], tab_ref[b, h], acc)
    return acc


def _softplus(x):
    return jnp.maximum(x, 0.0) + jnp.log1p(jnp.exp(-jnp.abs(x)))


def _norm_mm_kernel(x_ref, g_ref, w_ref, o_ref, *rest, emit):
    xn_ref = rest[-1]

    @pl.when(pl.program_id(1) == 0)
    def _():
        x = x_ref[...]
        ms = jnp.mean(x * x, axis=-1, keepdims=True)
        xn_ref[...] = (x * lax.rsqrt(ms + EPS) * g_ref[...]).astype(BF16)

    wb = w_ref[...].astype(BF16)
    if emit:
        rest[0][...] = wb
    o_ref[...] = _dot_nn(xn_ref[...], wb)


def norm_matmul(x, g, w, emit=False):
    m, d = x.shape
    n = w.shape[1]
    tm = min(m, 1024)
    tn = 1024 if (not emit and n % 1024 == 0) else 512
    assert m % tm == 0 and n % tn == 0 and (not emit or m == tm)
    out_specs = [pl.BlockSpec((tm, tn), lambda i, j: (i, j))]
    out_shape = [jax.ShapeDtypeStruct((m, n), F32)]
    if emit:
        out_specs.append(pl.BlockSpec((d, tn), lambda i, j: (0, j)))
        out_shape.append(jax.ShapeDtypeStruct((d, n), BF16))
    res = pl.pallas_call(
        functools.partial(_norm_mm_kernel, emit=emit),
        grid=(m // tm, n // tn),
        in_specs=[pl.BlockSpec((tm, d), lambda i, j: (i, 0)),
                  pl.BlockSpec((1, d), lambda i, j: (0, 0)),
                  pl.BlockSpec((d, tn), lambda i, j: (0, j))],
        out_specs=out_specs,
        out_shape=out_shape,
        scratch_shapes=[pltpu.VMEM((tm, d), BF16)],
        compiler_params=_cp(("parallel", "arbitrary")),
        name="norm_matmul",
    )(x, g.reshape(1, d), w)
    return res if emit else res[0]


def _mm_res_kernel(*refs, emit, na):
    a_refs = refs[:na]
    w_ref, r_ref, o_ref = refs[na:na + 3]
    wb = w_ref[...].astype(BF16)
    if emit:
        refs[na + 3][...] = wb
    acc = r_ref[...]
    k0 = 0
    for a_ref in a_refs:
        k1 = k0 + a_ref.shape[1]
        acc = acc + _dot_nn(a_ref[...], wb[k0:k1, :])
        k0 = k1
    o_ref[...] = acc


def matmul_residual(a_parts, w, res, emit=False):
    m = a_parts[0].shape[0]
    kdim, n = w.shape
    tm = min(m, 1024)
    tn = 256 if emit else (1024 if kdim <= 2048 and n % 1024 == 0 else 512)
    assert m % tm == 0 and n % tn == 0 and (not emit or m == tm) and sum(a.shape[1] for a in a_parts) == kdim
    out_specs = [pl.BlockSpec((tm, tn), lambda i, j: (i, j))]
    out_shape = [jax.ShapeDtypeStruct((m, n), F32)]
    if emit:
        out_specs.append(pl.BlockSpec((kdim, tn), lambda i, j: (0, j)))
        out_shape.append(jax.ShapeDtypeStruct((kdim, n), BF16))
    out = pl.pallas_call(
        functools.partial(_mm_res_kernel, emit=emit, na=len(a_parts)),
        grid=(m // tm, n // tn),
        in_specs=[pl.BlockSpec((tm, a.shape[1]), lambda i, j: (i, 0)) for a in a_parts]
                 + [pl.BlockSpec((kdim, tn), lambda i, j: (0, j)),
                    pl.BlockSpec((tm, tn), lambda i, j: (i, j))],
        out_specs=out_specs,
        out_shape=out_shape,
        compiler_params=_cp(("parallel", "arbitrary"), vmem_mib=56),
        name="matmul_residual",
    )(*a_parts, w, res)
    return out if emit else out[0]


def _ffn_up_kernel(x_ref, xp_ref, g_ref, wv_ref, wg_ref, sv_ref, sg_ref, cwv_ref, cwg_ref, cbv_ref, cbg_ref,
                   o_ref, nsv_ref, nsg_ref, *rest, nb, tc, chunks, emit):
    xn_ref, xnp_ref, ev_ref, eg_ref = rest[-4:]
    i = pl.program_id(0)
    tm, tn = o_ref.shape

    def norm(x):
        return x * lax.rsqrt(jnp.mean(x * x, axis=-1, keepdims=True) + EPS) * g_ref[...]

    @pl.when(pl.program_id(1) == 0)
    def _():
        xn_ref[...] = norm(x_ref[...]).astype(BF16)
        if chunks > 1:
            xnp_ref[...] = norm(xp_ref[...])

    def half(w_ref, wb_ref, s_ref, cw_ref, cb_ref, e_ref, ns_ref):
        wb = w_ref[...].astype(BF16)
        if emit:
            wb_ref[...] = wb
        u = _dot_nn(xn_ref[...], wb)
        cw = cw_ref[...]
        if nb > 1:
            e_ref[:, 8:8 + tc, :] = u.reshape(nb, tc, tn)
            e_ref[:, 6:8, :] = s_ref[...]
            ns_ref[...] = e_ref[:, 6 + tc:8 + tc, :]
            return (cb_ref[...] + e_ref[:, 6:6 + tc, :] * cw[0:1] + e_ref[:, 7:7 + tc, :] * cw[1:2]
                    + e_ref[:, 8:8 + tc, :] * cw[2:3]).reshape(tm, tn)
        ns_ref[0] = u[tm - 2:tm, :]
        prev = s_ref[0]
        if chunks > 1:
            up = _dot_nn(xnp_ref[...].astype(BF16), wb)
            prev = jnp.where(i % chunks == 0, prev, up[6:8, :])
        u1 = pltpu.roll(u, 1, 0)
        u2 = pltpu.roll(u, 2, 0)
        c = cb_ref[...] + u2 * cw[0:1] + u1 * cw[1:2] + u * cw[2:3]
        row = _iota((8, tn), 0)
        t1 = jnp.where(row == 0, prev[1:2, :], u1[0:8, :])
        t2 = jnp.where(row == 0, prev[0:1, :], jnp.where(row == 1, prev[1:2, :], u2[0:8, :]))
        top = cb_ref[...] + t2 * cw[0:1] + t1 * cw[1:2] + u[0:8, :] * cw[2:3]
        return jnp.concatenate([top, c[8:, :]], axis=0)

    wbv_ref, wbg_ref = (rest[0], rest[1]) if emit else (None, None)
    val = half(wv_ref, wbv_ref, sv_ref, cwv_ref, cbv_ref, ev_ref, nsv_ref)
    gate = half(wg_ref, wbg_ref, sg_ref, cwg_ref, cbg_ref, eg_ref, nsg_ref)
    o_ref[...] = (jax.nn.gelu(gate) * val).astype(o_ref.dtype)


def ffn_up(x, seq_len, state, norm_g, w_val, w_gate, gate_col, conv_w, conv_b, emit=False):
    m, d = x.shape
    f2 = conv_w.shape[1]
    f = f2 // 2
    tn = 256 if emit else 512
    tm = min(m, 1024)
    tc = min(seq_len, tm)
    nb = tm // tc
    chunks = seq_len // tc
    assert m % tm == 0 and tm % tc == 0 and seq_len % tc == 0 and tc % 8 == 0 and (nb == 1 or chunks == 1)
    assert f % tn == 0 and gate_col % tn == 0 and conv_w.shape[0] == 3 and (not emit or m == tm)
    ncb = f // tn
    goff = gate_col // tn
    r8 = tm // 8
    nseq = m // seq_len

    def wspec(rows, off):
        return pl.BlockSpec((rows, tn), lambda i, j: (0, j + off))

    def st(off):
        return pl.BlockSpec((nb, 2, tn), lambda i, j: (i // chunks, 0, j + off))

    tail = pl.BlockSpec((nb, 2, tn), lambda i, j: (i, 0, j))
    out_specs = [pl.BlockSpec((tm, tn), lambda i, j: (i, j)), tail, tail]
    out_shape = [jax.ShapeDtypeStruct((m, f), BF16), jax.ShapeDtypeStruct((nseq * chunks, 2, f), F32),
                 jax.ShapeDtypeStruct((nseq * chunks, 2, f), F32)]
    if emit:
        out_specs += [wspec(d, 0), wspec(d, 0)]
        out_shape += [jax.ShapeDtypeStruct((d, f), BF16), jax.ShapeDtypeStruct((d, f), BF16)]
    return pl.pallas_call(
        functools.partial(_ffn_up_kernel, nb=nb, tc=tc, chunks=chunks, emit=emit),
        grid=(m // tm, ncb),
        in_specs=[pl.BlockSpec((tm, d), lambda i, j: (i, 0)),
                  pl.BlockSpec((8, d), lambda i, j: (jnp.maximum(i * r8 - 1, 0), 0)),
                  pl.BlockSpec((1, d), lambda i, j: (0, 0)),
                  wspec(d, 0), wspec(d, goff), st(0), st(ncb),
                  wspec(3, 0), wspec(3, ncb), wspec(1, 0), wspec(1, ncb)],
        out_specs=out_specs,
        out_shape=out_shape,
        scratch_shapes=[pltpu.VMEM((tm, d), BF16), pltpu.VMEM((8, d), F32),
                        pltpu.VMEM((nb, tc + 8, tn), F32), pltpu.VMEM((nb, tc + 8, tn), F32)],
        compiler_params=_cp(("parallel", "arbitrary"), vmem_mib=56),
        name="ffn_up",
    )(x, x, norm_g.reshape(1, d), w_val, w_gate, state, state, conv_w, conv_w,
      conv_b.reshape(1, f2), conv_b.reshape(1, f2))


def _rglru_kernel(xa_ref, ga_ref, pv_ref, cs_ref, h0_ref, cw_ref, cb_ref, wa_ref, ba_ref,
                  wi_ref, bi_ref, lam_ref, ya_ref, hl_ref, e_ref, hc_ref, *, tc, bb):
    t = pl.program_id(1)
    w = xa_ref.shape[-1]
    rows = bb * tc

    @pl.when(t == 0)
    def _():
        hc_ref[...] = h0_ref[...]
        e_ref[:, 5:8, :] = cs_ref[...]

    @pl.when(t > 0)
    def _():
        e_ref[:, 5:8, :] = pv_ref[:, 5:8, :]

    e_ref[:, 8:8 + tc, :] = xa_ref[...]
    cw = cw_ref[...]
    xc = (cb_ref[...] + e_ref[:, 5:5 + tc, :] * cw[0:1] + e_ref[:, 6:6 + tc, :] * cw[1:2]
          + e_ref[:, 7:7 + tc, :] * cw[2:3] + e_ref[:, 8:8 + tc, :] * cw[3:4]).reshape(rows, w)
    xb = xc.astype(BF16)

    def gate(w_ref, b_ref):
        parts = [_dot_nn(xb[:, g * LRU_GROUP:(g + 1) * LRU_GROUP], w_ref[g])
                 for g in range(w // LRU_GROUP)]
        return jax.nn.sigmoid(jnp.concatenate(parts, axis=1) + b_ref[...])

    r = gate(wa_ref, ba_ref)
    i = gate(wi_ref, bi_ref)
    log_a = -LRU_C * r * _softplus(-lam_ref[...])
    a = jnp.exp(log_a)
    u = jnp.sqrt(-jnp.tanh(log_a) * (1.0 + a * a)) * (i * xc)

    step = _iota((rows, w), 0) % tc
    s = 1
    while s < tc:
        a_sh = jnp.where(step >= s, pltpu.roll(a, s, 0), 1.0)
        u_sh = jnp.where(step >= s, pltpu.roll(u, s, 0), 0.0)
        u = u + a * u_sh
        a = a * a_sh
        s *= 2
    h = u.reshape(bb, tc, w) + a.reshape(bb, tc, w) * hc_ref[...]
    hlast = h[:, tc - 1:tc, :]
    hc_ref[...] = hlast
    hl_ref[...] = hlast
    ya_ref[...] = (h * jax.nn.gelu(ga_ref[...])).astype(ya_ref.dtype)


def rglru_branch(z, h0, conv_state, conv_w, conv_b, wa_bd, ba, wi_bd, bi, lam):
    b, t, _ = z.shape
    w = h0.shape[-1]
    tc = min(t, 256)
    bb = 8 if (tc == t and tc == 8 and b % 8 == 0) else 1
    assert t % tc == 0 and tc % 8 == 0 and conv_w.shape[0] == 4 and w % LRU_GROUP == 0
    r8 = tc // 8
    ng = w // LRU_GROUP
    vec = lambda: pl.BlockSpec((1, w), lambda i, j: (0, 0))
    ya, hl = pl.pallas_call(
        functools.partial(_rglru_kernel, tc=tc, bb=bb),
        grid=(b // bb, t // tc),
        in_specs=[pl.BlockSpec((bb, tc, w), lambda i, j: (i, j, 0)),
                  pl.BlockSpec((bb, tc, w), lambda i, j: (i, j, 1)),
                  pl.BlockSpec((bb, 8, w), lambda i, j: (i, jnp.maximum(j * r8 - 1, 0), 0)),
                  pl.BlockSpec((bb, 3, w), lambda i, j: (i, 0, 0)),
                  pl.BlockSpec((bb, 1, w), lambda i, j: (i, 0, 0)),
                  pl.BlockSpec((4, w), lambda i, j: (0, 0)),
                  vec(),
                  pl.BlockSpec((ng, LRU_GROUP, LRU_GROUP), lambda i, j: (0, 0, 0)),
                  vec(),
                  pl.BlockSpec((ng, LRU_GROUP, LRU_GROUP), lambda i, j: (0, 0, 0)),
                  vec(), vec()],
        out_specs=[pl.BlockSpec((bb, tc, w), lambda i, j: (i, j, 0)),
                   pl.BlockSpec((bb, 1, w), lambda i, j: (i, 0, 0))],
        out_shape=[jax.ShapeDtypeStruct((b, t, w), BF16), jax.ShapeDtypeStruct((b, 1, w), F32)],
        scratch_shapes=[pltpu.VMEM((bb, tc + 8, w), F32), pltpu.VMEM((bb, 1, w), F32)],
        compiler_params=_cp(("parallel", "arbitrary")),
        name="rglru",
    )(z, z, z, conv_state, h0.reshape(b, 1, w), conv_w, conv_b.reshape(1, w), wa_bd,
      ba.reshape(1, w), wi_bd, bi.reshape(1, w), lam.reshape(1, w))
    return ya, hl.reshape(b, w)


def _regroup_blockdiag(wb):
    n, c, _ = wb.shape
    per = LRU_GROUP // c
    g = n // per
    eye = jnp.eye(per, dtype=wb.dtype)
    out = jnp.einsum("gacd,ab->gacbd", wb.reshape(g, per, c, c), eye)
    return out.reshape(g, LRU_GROUP, LRU_GROUP).astype(BF16)


def _bias_kernel(tab_ref, bt_ref, bs_ref, bx_ref, *, past, lpad, nkv, grp):
    h = pl.program_id(0)
    d0 = _iota((Q_TILE, Q_TILE), 0) - _iota((Q_TILE, Q_TILE), 1)
    bt_ref[0, 0] = _t5_bias(d0, tab_ref, h)
    bt_ref[0, 1] = _t5_bias(d0 + Q_TILE, tab_ref, h)
    t = bs_ref.shape[1]
    ds = past + _iota((t, lpad), 0) - _iota((t, lpad), 1)
    bs_ref[0] = _t5_bias(ds, tab_ref, h)
    col = _iota((t, past * nkv), 1)
    dx = past + _iota((t, past * nkv), 0) - col // nkv
    bx_ref[0] = jnp.where(col % nkv == h // grp, _t5_bias(dx, tab_ref, h), NEG_INF)


def bias_tiles(table, past, t_new, lpad, nkv):
    nh = table.shape[1]
    return pl.pallas_call(
        functools.partial(_bias_kernel, past=past, lpad=lpad, nkv=nkv, grp=nh // nkv),
        grid=(nh,),
        in_specs=[pl.BlockSpec(memory_space=pltpu.SMEM)],
        out_specs=[pl.BlockSpec((1, 2, Q_TILE, Q_TILE), lambda h: (h, 0, 0, 0)),
                   pl.BlockSpec((1, t_new, lpad), lambda h: (h, 0, 0)),
                   pl.BlockSpec((1, t_new, past * nkv), lambda h: (h, 0, 0))],
        out_shape=[jax.ShapeDtypeStruct((nh, 2, Q_TILE, Q_TILE), F32),
                   jax.ShapeDtypeStruct((nh, t_new, lpad), F32),
                   jax.ShapeDtypeStruct((nh, t_new, past * nkv), F32)],
        compiler_params=_cp(("arbitrary",)),
        name="t5_bias_tiles",
    )(table)


def _top_lanes(g, nvalid, k):
    lane = _iota(g.shape, 1).astype(F32)
    gg = jnp.where(lane < nvalid, g, NEG_INF)
    sel = jnp.zeros(g.shape, F32)
    for _ in range(k):
        mx = jnp.max(gg, axis=-1, keepdims=True)
        first = jnp.min(jnp.where(gg == mx, lane, float(g.shape[1])), axis=-1, keepdims=True)
        pick = (lane == first) & (mx > NEG_INF)
        sel = jnp.where(pick, 1.0, sel)
        gg = jnp.where(pick, NEG_INF, gg)
    return sel


def _moba_prompt_kernel(tab_ref, q_ref, k_ref, v_ref, qg_ref, kg_ref, bt_ref, y_ref, kn_ref, *, s):
    hp = pl.program_id(0)
    nb = s // MOBA_BLOCK
    dh = MOBA_HEAD_DIM
    scale = dh ** -0.5
    lane = _iota((1, LANES), 1)
    low = lane < dh

    def headnorm(x, g):
        x2 = x * x
        s0 = jnp.sum(jnp.where(low, x2, 0.0), axis=-1, keepdims=True)
        s1 = jnp.sum(jnp.where(low, 0.0, x2), axis=-1, keepdims=True)
        ms = jnp.where(low, s0, s1) * (1.0 / dh)
        return x * lax.rsqrt(ms + EPS) * g

    qn = headnorm(q_ref[0], qg_ref[...])
    kn = headnorm(k_ref[0], kg_ref[...])
    kn_ref[0] = kn
    kb = kn.astype(BF16)
    vb = v_ref[0].astype(BF16)
    blk_row = _iota((LANES, LANES), 0)
    kmat = jnp.zeros((LANES, LANES), F32)
    for j in range(nb):
        kmean = jnp.mean(kn[j * MOBA_BLOCK:(j + 1) * MOBA_BLOCK], axis=0, keepdims=True)
        kmat = jnp.where(blk_row == j, kmean, kmat)
    qi = _iota((Q_TILE, Q_TILE), 0)
    kj = _iota((Q_TILE, Q_TILE), 1)
    causal = qi >= kj

    near = []
    for hh in range(2):
        t31 = tab_ref[N_BUCKETS - 1, 2 * hp + hh]
        near.append((jnp.where(causal, bt_ref[hh, 0] - t31, NEG_INF), bt_ref[hh, 1] - t31))

    for c in range(nb):
        qc = qn[c * Q_TILE:(c + 1) * Q_TILE]
        nk = (c + 1) * MOBA_BLOCK
        outs = []
        for hh in range(2):
            hmask = low if hh == 0 else jnp.logical_not(low)
            qm = jnp.where(hmask, qc, 0.0)
            sc = _dot_nt((qm * scale).astype(BF16), kb[0:nk])
            if c > MOBA_TOPK:
                sel = _top_lanes(_dot3(_dot_nt, qm, kmat), c, MOBA_TOPK)
            pieces = []
            for j in range(c + 1):
                sj = sc[:, j * MOBA_BLOCK:(j + 1) * MOBA_BLOCK]
                if j == c:
                    sj = sj + near[hh][0]
                else:
                    if j == c - 1:
                        sj = sj + near[hh][1]
                    if c > MOBA_TOPK:
                        sj = jnp.where(sel[:, j:j + 1] > 0.0, sj, NEG_INF)
                pieces.append(sj)
            srow = jnp.concatenate(pieces, axis=1) if c else pieces[0]
            mx = jnp.max(srow, axis=-1, keepdims=True)
            p = jnp.exp(srow - mx)
            l = jnp.sum(p, axis=-1, keepdims=True)
            outs.append(_dot_nn(p.astype(BF16), vb[0:nk]) / l)
        y_ref[0, c * Q_TILE:(c + 1) * Q_TILE, :] = jnp.where(low, outs[0], outs[1]).astype(y_ref.dtype)


def moba_prompt(z, table, bt, q_gain, k_gain, w):
    b, s, _ = z.shape
    assert s % MOBA_BLOCK == 0 and Q_TILE == MOBA_BLOCK and s // MOBA_BLOCK <= LANES
    nhp = w // LANES
    g2 = lambda g: jnp.tile(g, LANES // MOBA_HEAD_DIM).reshape(1, LANES)
    col = lambda off: pl.BlockSpec((1, s, LANES), lambda hp, i: (i, 0, off * nhp + hp))
    gspec = pl.BlockSpec((1, LANES), lambda hp, i: (0, 0))
    return pl.pallas_call(
        functools.partial(_moba_prompt_kernel, s=s),
        grid=(nhp, b),
        in_specs=[pl.BlockSpec(memory_space=pltpu.SMEM), col(2), col(3), col(4), gspec, gspec,
                  pl.BlockSpec((2, 2, Q_TILE, Q_TILE), lambda hp, i: (hp, 0, 0, 0))],
        out_specs=[pl.BlockSpec((1, s, LANES), lambda hp, i: (i, 0, hp)),
                   pl.BlockSpec((1, s, LANES), lambda hp, i: (i, 0, hp))],
        out_shape=[jax.ShapeDtypeStruct((b, s, w), BF16), jax.ShapeDtypeStruct((b, s, w), F32)],
        compiler_params=_cp(("parallel", "parallel")),
        name="moba_prompt",
    )(table, z, z, z, g2(q_gain), g2(k_gain), bt)


def _group_onehot(ngroups_pad, width, gsize):
    return jnp.where(_iota((ngroups_pad, width), 0) == _iota((ngroups_pad, width), 1) // gsize,
                     1.0, 0.0).astype(BF16)


def _group_rmsnorm(x, gain, gsize):
    onehot = _group_onehot(LANES, x.shape[1], gsize)
    ssq = _dot_exact01(_dot_nt, x * x, onehot)
    rinv = lax.rsqrt(ssq * (1.0 / gsize) + EPS)
    return x * _dot_exact01(_dot_nn, rinv, onehot) * gain


def _moba_sample_kernel(pt_ref, q_ref, k_ref, v_ref, qg_ref, kg_ref, bs_ref, *rest, npages, t):
    kp_refs = rest[:npages]
    vp_refs = rest[npages:2 * npages]
    y_ref, kn_ref = rest[2 * npages:]
    w = q_ref.shape[-1]
    dh = MOBA_HEAD_DIM
    nh = w // dh
    rows = nh * t
    ppb = MOBA_BLOCK // PAGE_SIZE
    nbp = npages // ppb
    past = npages * PAGE_SIZE
    scale = dh ** -0.5

    qn = _group_rmsnorm(q_ref[0], qg_ref[...], dh)
    kn = _group_rmsnorm(k_ref[0], kg_ref[...], dh)
    kn_ref[0] = kn
    own = _iota((rows, w), 0) // t == _iota((rows, w), 1) // dh
    qbd = jnp.where(own, jnp.concatenate([qn] * nh, axis=0), 0.0)
    qbb = qbd.astype(BF16)
    pad_rows = lambda x: jnp.concatenate([x, jnp.zeros((LANES - t, w), x.dtype)], axis=0)

    pieces, sums = [], []
    for p in range(npages):
        kpt = kp_refs[p][0]
        sums.append(jnp.sum(kpt, axis=1, keepdims=True))
        pieces.append(_dot_nn(qbb, kpt.astype(BF16)))
    pieces.append(_dot_nt(qbb, pad_rows(kn).astype(BF16)))
    lpad = past + LANES
    srow = jnp.concatenate(pieces, axis=1) * scale + bs_ref[...]

    blk_lane = _iota((w, LANES), 1)
    kmat = jnp.zeros((w, LANES), F32)
    for j in range(nbp):
        kmat = jnp.where(blk_lane == j, sum(sums[j * ppb:(j + 1) * ppb]) * (1.0 / MOBA_BLOCK), kmat)
    sel = _top_lanes(_dot3(_dot_nn, qbd, kmat), nbp, MOBA_TOPK)
    col = _iota((rows, lpad), 1)
    jrow = _iota((rows, lpad), 0) % t
    keep = (col >= past) & (col - past <= jrow)
    for j in range(nbp):
        inblk = (col >= j * MOBA_BLOCK) & (col < (j + 1) * MOBA_BLOCK)
        keep = keep | (inblk & (sel[:, j:j + 1] > 0.0))
    srow = jnp.where(keep, srow, NEG_INF)
    mx = jnp.max(srow, axis=-1, keepdims=True)
    pr = jnp.exp(srow - mx)
    l = jnp.sum(pr, axis=-1, keepdims=True)
    pb = pr.astype(BF16)
    acc = _dot_nn(pb[:, past:], pad_rows(v_ref[0]).astype(BF16))
    for p in range(npages):
        acc = acc + _dot_nt(pb[:, p * PAGE_SIZE:(p + 1) * PAGE_SIZE], vp_refs[p][0].astype(BF16))
    acc = acc / l
    lane_head = _iota((t, w), 1) // dh
    y = jnp.zeros((t, w), F32)
    for h in range(nh):
        y = jnp.where(lane_head == h, acc[h * t:(h + 1) * t, :], y)
    y_ref[0] = y.astype(y_ref.dtype)


def moba_sample(z, pool_k, pool_v, page_table, bs, q_gain, k_gain, w):
    db, t, _ = z.shape
    npages = page_table.shape[1]
    nh = w // MOBA_HEAD_DIM
    assert t == 8 and nh * t == LANES and npages % (MOBA_BLOCK // PAGE_SIZE) == 0
    lpad = npages * PAGE_SIZE + LANES
    gt = lambda g: jnp.tile(g, nh).reshape(1, w)
    col = lambda off: pl.BlockSpec((1, t, w), lambda i, pt: (i, 0, off))
    gspec = pl.BlockSpec((1, w), lambda i, pt: (0, 0))
    page = lambda p: pl.BlockSpec((1, w, PAGE_SIZE), lambda i, pt: (pt[i, p], 0, 0))
    grid_spec = pltpu.PrefetchScalarGridSpec(
        num_scalar_prefetch=1,
        grid=(db,),
        in_specs=[col(2), col(3), col(4), gspec, gspec,
                  pl.BlockSpec((nh * t, lpad), lambda i, pt: (0, 0))]
                 + [page(p) for p in range(npages)] * 2,
        out_specs=[pl.BlockSpec((1, t, w), lambda i, pt: (i, 0, 0)),
                   pl.BlockSpec((1, t, w), lambda i, pt: (i, 0, 0))],
    )
    return pl.pallas_call(
        functools.partial(_moba_sample_kernel, npages=npages, t=t),
        grid_spec=grid_spec,
        out_shape=[jax.ShapeDtypeStruct((db, t, w), BF16), jax.ShapeDtypeStruct((db, t, w), F32)],
        compiler_params=_cp(("arbitrary",), vmem_mib=56),
        name="moba_sample",
    )(page_table, z, z, z, gt(q_gain), gt(k_gain), bs.reshape(nh * t, lpad),
      *([pool_k] * npages), *([pool_v] * npages))


def _kv_prep_kernel(k_ref, v_ref, g_ref, kn_ref, knb_ref, vb_ref):
    x = k_ref[...]
    ms = jnp.mean(x * x, axis=-1, keepdims=True)
    kn = x * lax.rsqrt(ms + EPS) * g_ref[...]
    kn_ref[...] = kn
    knb_ref[...] = kn.astype(BF16)
    vb_ref[...] = v_ref[...].astype(BF16)


def kv_prep(z, gain, k_blk0, v_blk0, nheads):
    m = z.shape[0]
    tm = min(m, 1024)
    assert m % tm == 0
    out = pl.BlockSpec((tm, LANES), lambda i, h: (i, h))
    return pl.pallas_call(
        _kv_prep_kernel,
        grid=(m // tm, nheads),
        in_specs=[pl.BlockSpec((tm, LANES), lambda i, h: (i, k_blk0 + h)),
                  pl.BlockSpec((tm, LANES), lambda i, h: (i, v_blk0 + h)),
                  pl.BlockSpec((1, LANES), lambda i, h: (0, 0))],
        out_specs=[out, out, out],
        out_shape=[jax.ShapeDtypeStruct((m, nheads * LANES), F32),
                   jax.ShapeDtypeStruct((m, nheads * LANES), BF16),
                   jax.ShapeDtypeStruct((m, nheads * LANES), BF16)],
        compiler_params=_cp(("parallel", "parallel")),
        name="kv_prep",
    )(z, z, gain.reshape(1, LANES))


def _index_prompt_kernel(qi_ref, qw_ref, kk_ref, o_ref, *, s):
    c = pl.program_id(1)
    lane = _iota((1, LANES), 1)
    low = lane < IDX_DIM
    qw = qw_ref[0]

    def tile(cc):
        nk = (cc + 1) * Q_TILE
        kk = jnp.where(low, kk_ref[0, 0:nk, :], 0.0)
        kk_hi = pltpu.roll(kk, IDX_DIM, 1)
        acc = jnp.zeros((Q_TILE, nk), F32)
        for ih in range(IDX_HEADS):
            blk = qi_ref[0, :, (ih // 2) * LANES:(ih // 2 + 1) * LANES]
            half_low = ih % 2 == 0
            qm = jnp.where(low if half_low else jnp.logical_not(low), blk, 0.0)
            sc = _dot3(_dot_nt, qm, kk if half_low else kk_hi)
            wcol = qw[:, IDX_DIM + ih:IDX_DIM + ih + 1] * (IDX_HEADS ** -0.5)
            acc = acc + wcol * jnp.maximum(sc * (IDX_DIM ** -0.5), 0.0)
        qpos = cc * Q_TILE + _iota((Q_TILE, nk), 0)
        o_ref[0, :, 0:nk] = jnp.where(_iota((Q_TILE, nk), 1) <= qpos, acc, NEG_INF)
        if nk < s:
            o_ref[0, :, nk:] = jnp.full((Q_TILE, s - nk), NEG_INF, F32)

    for cc in range(s // Q_TILE):
        pl.when(c == cc)(functools.partial(tile, cc))


def index_prompt(z, qi_blk, kw_blk):
    b, s, _ = z.shape
    assert s % Q_TILE == 0
    return pl.pallas_call(
        functools.partial(_index_prompt_kernel, s=s),
        grid=(b, s // Q_TILE),
        in_specs=[pl.BlockSpec((1, Q_TILE, IDX_HEADS * IDX_DIM), lambda i, c: (i, c, qi_blk)),
                  pl.BlockSpec((1, Q_TILE, LANES), lambda i, c: (i, c, kw_blk)),
                  pl.BlockSpec((1, s, LANES), lambda i, c: (i, 0, kw_blk))],
        out_specs=pl.BlockSpec((1, Q_TILE, s), lambda i, c: (i, c, 0)),
        out_shape=jax.ShapeDtypeStruct((b, s, s), F32),
        compiler_params=_cp(("parallel", "parallel")),
        name="index_prompt",
    )(z, z, z)


def _index_sample_kernel(pt_ref, qi_ref, qw_ref, *rest, npages, t):
    kp_refs = rest[:npages]
    o_ref = rest[npages]
    qi = qi_ref[0]
    qw = qw_ref[0]
    q2 = jnp.concatenate([qi[:, ih * IDX_DIM:(ih + 1) * IDX_DIM] for ih in range(IDX_HEADS)], axis=0)
    wcol = jnp.concatenate([qw[:, IDX_DIM + ih:IDX_DIM + ih + 1] for ih in range(IDX_HEADS)], axis=0)
    wcol = wcol * (IDX_HEADS ** -0.5)
    knew = jnp.concatenate([qw[:, 0:IDX_DIM], jnp.zeros((LANES - t, IDX_DIM), F32)], axis=0)

    def scores(dot, keys):
        sc = wcol * jnp.maximum(_dot3(dot, q2, keys) * (IDX_DIM ** -0.5), 0.0)
        return sum(sc[ih * t:(ih + 1) * t, :] for ih in range(IDX_HEADS))

    pieces = [scores(_dot_nn, kp_refs[p][0]) for p in range(npages)]
    new = scores(_dot_nt, knew)
    jj = _iota((t, LANES), 0)
    cc = _iota((t, LANES), 1)
    pieces.append(jnp.where(cc <= jj, new, NEG_INF))
    o_ref[0] = jnp.concatenate(pieces, axis=1)


def index_sample(z, pool_ik, page_table, qi_blk, kw_blk):
    db, t, _ = z.shape
    npages = page_table.shape[1]
    lpad = npages * PAGE_SIZE + LANES
    grid_spec = pltpu.PrefetchScalarGridSpec(
        num_scalar_prefetch=1,
        grid=(db,),
        in_specs=[pl.BlockSpec((1, t, IDX_HEADS * IDX_DIM), lambda i, pt: (i, 0, qi_blk)),
                  pl.BlockSpec((1, t, LANES), lambda i, pt: (i, 0, kw_blk))]
                 + [pl.BlockSpec((1, IDX_DIM, PAGE_SIZE), (lambda i, pt, p=p: (pt[i, p], 0, 0)))
                    for p in range(npages)],
        out_specs=pl.BlockSpec((1, t, lpad), lambda i, pt: (i, 0, 0)),
    )
    return pl.pallas_call(
        functools.partial(_index_sample_kernel, npages=npages, t=t),
        grid_spec=grid_spec,
        out_shape=jax.ShapeDtypeStruct((db, t, lpad), F32),
        compiler_params=_cp(("arbitrary",)),
        name="index_sample",
    )(page_table, z, z, *([pool_ik] * npages))


def _topk_mask_kernel(x_ref, o_ref, *, topk, period, base, tr, additive, ncausal):
    l = x_ref.shape[1]
    on, off = (0.0, NEG_INF) if additive else (1.0, 0.0)

    def body(nk):
        rowg = pl.program_id(0) * tr + _iota((tr, nk), 0)
        valid = _iota((tr, nk), 1) <= base + rowg % period
        bits = pltpu.bitcast(x_ref[:, 0:nk], I32)
        key = jnp.where(bits < 0, bits ^ 0x7FFFFFFF, bits)
        key = jnp.where(valid, key, INT_MIN)

        def step(i, lo):
            cand = lo + lax.shift_left(jnp.int32(1), 31 - i)
            cnt = jnp.sum(jnp.where(key >= cand, 1.0, 0.0), axis=-1, keepdims=True)
            return jnp.where(cnt >= topk, cand, lo)

        thr = lax.fori_loop(0, 32, step, jnp.full((tr, 1), INT_MIN, I32))
        gt = key > thr
        need = topk - jnp.sum(jnp.where(gt, 1.0, 0.0), axis=-1, keepdims=True)
        eq = jnp.where(key == thr, 1.0, 0.0)
        upper = jnp.where(_iota((LANES, LANES), 0) < _iota((LANES, LANES), 1), 1.0, 0.0).astype(BF16)
        before = jnp.zeros((tr, 1), F32)
        for j in range(nk // LANES):
            e = eq[:, j * LANES:(j + 1) * LANES]
            rank = before + _dot_nn(e.astype(BF16), upper)
            pick = gt[:, j * LANES:(j + 1) * LANES] | ((e > 0.0) & (rank < need))
            pick = pick & valid[:, j * LANES:(j + 1) * LANES]
            o_ref[:, j * LANES:(j + 1) * LANES] = jnp.where(pick, on, off)
            before = before + jnp.sum(e, axis=-1, keepdims=True)
        if nk < l:
            o_ref[:, nk:] = jnp.full((tr, l - nk), off, F32)

    if ncausal > 1:
        c = pl.program_id(0) % ncausal
        for cc in range(ncausal):
            pl.when(c == cc)(functools.partial(body, (cc + 1) * tr))
    else:
        body(l)


def topk_mask(scores, topk, period, base, additive):
    r, l = scores.shape
    tr = min(r, 256)
    assert r % tr == 0 and l % LANES == 0
    ncausal = period // tr if (base == 0 and period == l and period % tr == 0) else 1
    return pl.pallas_call(
        functools.partial(_topk_mask_kernel, topk=topk, period=period, base=base, tr=tr, additive=additive,
                          ncausal=ncausal),
        grid=(r // tr,),
        in_specs=[pl.BlockSpec((tr, l), lambda i: (i, 0))],
        out_specs=pl.BlockSpec((tr, l), lambda i: (i, 0)),
        out_shape=jax.ShapeDtypeStruct((r, l), F32),
        compiler_params=_cp(("parallel",)),
        name="topk_mask",
    )(scores)


def _dsa_prompt_kernel(tab_ref, q_ref, k_ref, v_ref, m_ref, bt_ref, qg_ref, y_ref, *, s):
    c = pl.program_id(1)
    g = pl.program_id(2)
    dh = DSA_HEAD_DIM
    grp = q_ref.shape[-1] // dh
    qbs, t31s = [], []
    for hh in range(grp):
        q = q_ref[0, :, hh * dh:(hh + 1) * dh]
        qn = q * lax.rsqrt(jnp.mean(q * q, axis=-1, keepdims=True) + EPS) * qg_ref[...]
        qbs.append((qn * (dh ** -0.5)).astype(BF16))
        t31s.append(tab_ref[N_BUCKETS - 1, g * grp + hh])

    def tile(cc):
        nk = (cc + 1) * Q_TILE
        madd = m_ref[0, :, 0:nk]
        for hh in range(grp):
            sc = _dot_nt(qbs[hh], k_ref[0, 0:nk, :]) + madd
            pieces = [sc[:, 0:(cc - 1) * Q_TILE]] if cc > 1 else []
            if cc > 0:
                pieces.append(sc[:, (cc - 1) * Q_TILE:cc * Q_TILE] + (bt_ref[hh, 1] - t31s[hh]))
            pieces.append(sc[:, cc * Q_TILE:nk] + (bt_ref[hh, 0] - t31s[hh]))
            srow = jnp.concatenate(pieces, axis=1) if cc else pieces[0]
            mx = jnp.max(srow, axis=-1, keepdims=True)
            p = jnp.exp(srow - mx)
            l = jnp.sum(p, axis=-1, keepdims=True)
            y_ref[0, :, hh * dh:(hh + 1) * dh] = (_dot_nn(p.astype(BF16), v_ref[0, 0:nk, :]) / l
                                                  ).astype(y_ref.dtype)

    for cc in range(s // Q_TILE):
        pl.when(c == cc)(functools.partial(tile, cc))


def dsa_prompt(z, knb, vb, madd, table, bt, q_gain, nq, nkv):
    b, s, _ = z.shape
    grp = nq // nkv
    return pl.pallas_call(
        functools.partial(_dsa_prompt_kernel, s=s),
        grid=(b, s // Q_TILE, nkv),
        in_specs=[pl.BlockSpec(memory_space=pltpu.SMEM),
                  pl.BlockSpec((1, Q_TILE, grp * LANES), lambda i, c, g: (i, c, g)),
                  pl.BlockSpec((1, s, LANES), lambda i, c, g: (i, 0, g)),
                  pl.BlockSpec((1, s, LANES), lambda i, c, g: (i, 0, g)),
                  pl.BlockSpec((1, Q_TILE, s), lambda i, c, g: (i, c, 0)),
                  pl.BlockSpec((grp, 2, Q_TILE, Q_TILE), lambda i, c, g: (g, 0, 0, 0)),
                  pl.BlockSpec((1, LANES), lambda i, c, g: (0, 0))],
        out_specs=pl.BlockSpec((1, Q_TILE, grp * LANES), lambda i, c, g: (i, c, g)),
        out_shape=jax.ShapeDtypeStruct((b, s, nq * LANES), BF16),
        compiler_params=_cp(("parallel", "parallel", "arbitrary")),
        name="dsa_prompt",
    )(table, z, knb, vb, madd, bt, q_gain.reshape(1, LANES))


def _dsa_sample_kernel(pt_ref, q_ref, kn_ref, v_ref, m_ref, bs_ref, bx_ref, qg_ref, *rest,
                       npages, t, nq, nkv):
    kp_refs = rest[:npages]
    vp_refs = rest[npages:2 * npages]
    y_ref = rest[2 * npages]
    dh = DSA_HEAD_DIM
    grp = nq // nkv
    wkv = nkv * dh
    past = npages * PAGE_SIZE
    pcols = PAGE_SIZE * nkv
    scale = dh ** -0.5
    q = q_ref[0]
    g = qg_ref[...]
    zero = jnp.zeros((t, dh), F32)
    qrows, qstruct = [], []
    for h in range(nq):
        qh = q[:, h * dh:(h + 1) * dh]
        qh = qh * lax.rsqrt(jnp.mean(qh * qh, axis=-1, keepdims=True) + EPS) * g
        qrows.append(qh)
        qstruct.append(jnp.concatenate([qh if kv == h // grp else zero for kv in range(nkv)], axis=1))
    qall = jnp.concatenate(qrows, axis=0).astype(BF16)
    qsb = jnp.concatenate(qstruct, axis=0).astype(BF16)
    pad_rows = lambda x: jnp.concatenate([x, jnp.zeros((LANES - t, wkv), x.dtype)], axis=0)
    keep = jnp.concatenate([m_ref[0]] * nq, axis=0)
    keepb = keep.astype(BF16)
    spread = jnp.where(_iota((PAGE_SIZE, pcols), 0) == _iota((PAGE_SIZE, pcols), 1) // nkv,
                       1.0, 0.0).astype(BF16)

    pieces = []
    for p in range(npages):
        sp = _dot_nt(qall, kp_refs[p][...].astype(BF16)) * scale + bx_ref[:, p * pcols:(p + 1) * pcols]
        kexp = _dot_nn(keepb[:, p * PAGE_SIZE:(p + 1) * PAGE_SIZE], spread)
        pieces.append(jnp.where(kexp > 0.5, sp, NEG_INF))
    loc = _dot_nt(qsb, pad_rows(kn_ref[0]).astype(BF16)) * scale + bs_ref[:, past:]
    pieces.append(jnp.where(keep[:, past:] > 0.5, loc, NEG_INF))
    srow = jnp.concatenate(pieces, axis=1)
    mx = jnp.max(srow, axis=-1, keepdims=True)
    pr = jnp.exp(srow - mx)
    l = jnp.sum(pr, axis=-1, keepdims=True)
    pb = pr.astype(BF16)
    accl = _dot_nn(pb[:, npages * pcols:], pad_rows(v_ref[0]).astype(BF16))
    row_grp = _iota((nq * t, dh), 0) // (t * grp)
    acc = jnp.zeros((nq * t, dh), F32)
    for kv in range(nkv):
        acc = jnp.where(row_grp == kv, accl[:, kv * dh:(kv + 1) * dh], acc)
    for p in range(npages):
        acc = acc + _dot_nn(pb[:, p * pcols:(p + 1) * pcols], vp_refs[p][...].astype(BF16))
    acc = acc / l
    y_ref[0] = jnp.concatenate([acc[h * t:(h + 1) * t, :] for h in range(nq)], axis=1).astype(y_ref.dtype)


def dsa_sample(z, kn, mask, pool_k, pool_v, page_table, bs, bx, q_gain, nq, nkv, v_blk):
    db, t, _ = z.shape
    npages = page_table.shape[1]
    wkv = nkv * DSA_HEAD_DIM
    past = npages * PAGE_SIZE
    lpad = past + LANES
    assert nq * t == LANES
    page = lambda p: pl.BlockSpec((PAGE_SIZE * nkv, DSA_HEAD_DIM), lambda i, pt: (pt[i, p], 0))
    grid_spec = pltpu.PrefetchScalarGridSpec(
        num_scalar_prefetch=1,
        grid=(db,),
        in_specs=[pl.BlockSpec((1, t, nq * DSA_HEAD_DIM), lambda i, pt: (i, 0, 0)),
                  pl.BlockSpec((1, t, wkv), lambda i, pt: (i, 0, 0)),
                  pl.BlockSpec((1, t, wkv), lambda i, pt: (i, 0, v_blk)),
                  pl.BlockSpec((1, t, lpad), lambda i, pt: (i, 0, 0)),
                  pl.BlockSpec((nq * t, lpad), lambda i, pt: (0, 0)),
                  pl.BlockSpec((nq * t, past * nkv), lambda i, pt: (0, 0)),
                  pl.BlockSpec((1, LANES), lambda i, pt: (0, 0))]
                 + [page(p) for p in range(npages)] * 2,
        out_specs=pl.BlockSpec((1, t, nq * DSA_HEAD_DIM), lambda i, pt: (i, 0, 0)),
    )
    return pl.pallas_call(
        functools.partial(_dsa_sample_kernel, npages=npages, t=t, nq=nq, nkv=nkv),
        grid_spec=grid_spec,
        out_shape=jax.ShapeDtypeStruct((db, t, nq * DSA_HEAD_DIM), BF16),
        compiler_params=_cp(("arbitrary",)),
        name="dsa_sample",
    )(page_table, z, kn, z, mask, bs.reshape(nq * t, lpad), bx.reshape(nq * t, past * nkv),
      q_gain.reshape(1, LANES), *([pool_k] * npages), *([pool_v] * npages))


def _conv_ffn(x2, seq_len, state, norm_g, weights, conv_w, conv_b):
    emit = len(weights) == 2
    if emit:
        w_val = w_gate = weights[0]
        gate_col = weights[0].shape[1] // 2
    else:
        w_val, w_gate = weights[:2]
        gate_col = 0
    mid, ns_val, ns_gate, *wb = ffn_up(x2, seq_len, state, norm_g, w_val, w_gate, gate_col, conv_w, conv_b, emit)
    y = matmul_residual((mid,), weights[-1], x2, emit)
    if emit:
        y, w_down_b = y
        wb.append(w_down_b)
    step = ns_val.shape[0] * seq_len // x2.shape[0]
    return y, jnp.concatenate([ns_val[step - 1::step], ns_gate[step - 1::step]], axis=-1), tuple(wb)


def kernel(x_prompt, x_sample, state_rglru_h, state_rglru_conv, cache_moba_k, cache_moba_v, state_ffn0_conv, cache_dsa_k, cache_dsa_v, cache_dsa_idx_k, state_ffn1_conv, page_table, rel_bias_table, l0_norm_g, l0_w_in, l0_conv_w, l0_conv_b, l0_gate_a_w, l0_gate_a_b, l0_gate_i_w, l0_gate_i_b, l0_lambda, l0_q_norm_g, l0_k_norm_g, l0_w_out, ffn0_norm_g, ffn0_w_up, ffn0_conv_w, ffn0_conv_b, ffn0_w_down, l1_norm_g, l1_w_in, l1_q_norm_g, l1_k_norm_g, l1_w_out, ffn1_norm_g, ffn1_w_up, ffn1_conv_w, ffn1_conv_b, ffn1_w_down):
    bp, s, d = x_prompt.shape
    db, t, _ = x_sample.shape
    w = state_rglru_h.shape[-1]
    f2 = ffn0_w_up.shape[1]
    n_phys = cache_moba_k.shape[0]
    npages = page_table.shape[1]
    past = npages * PAGE_SIZE
    lpad = past + LANES
    nmh = cache_moba_k.shape[2]
    nkv = cache_dsa_k.shape[2]
    nq = l1_w_out.shape[0] // DSA_HEAD_DIM
    assert cache_moba_k.shape[3] == MOBA_HEAD_DIM and cache_dsa_k.shape[3] == DSA_HEAD_DIM
    assert cache_moba_k.shape[1] == PAGE_SIZE and nmh * MOBA_HEAD_DIM == w
    assert rel_bias_table.shape == (N_BUCKETS, nmh) and nq == nmh
    assert l0_w_in.shape[1] == 5 * w

    xp = x_prompt.reshape(bp * s, d)
    xs = x_sample.reshape(db * t, d)
    bt, bsamp, bsx = bias_tiles(rel_bias_table, past, t, lpad, nkv)

    wa_bd = _regroup_blockdiag(l0_gate_a_w)
    wi_bd = _regroup_blockdiag(l0_gate_i_w)
    pool_mk = cache_moba_k.transpose(0, 2, 3, 1).reshape(n_phys, w, PAGE_SIZE)
    pool_mv = cache_moba_v.transpose(0, 2, 3, 1).reshape(n_phys, w, PAGE_SIZE)

    def even(x2, b, tt, h0, conv_state, attend, w_in, w_out, emit):
        z = norm_matmul(x2, l0_norm_g, w_in, emit)
        z, w_in_b = z if emit else (z, None)
        z = z.reshape(b, tt, 5 * w)
        ya, h_last = rglru_branch(z, h0, conv_state, l0_conv_w, l0_conv_b, wa_bd, l0_gate_a_b,
                                  wi_bd, l0_gate_i_b, l0_lambda)
        yb, kn = attend(z)
        out = matmul_residual((ya.reshape(b * tt, w), yb.reshape(b * tt, w)), w_out, x2, emit)
        out, w_out_b = out if emit else (out, None)
        new_conv = z[:, tt - (l0_conv_w.shape[0] - 1):, :w]
        v_rows = z[:, :, 4 * w:].reshape(b, tt, nmh, MOBA_HEAD_DIM)
        return (out, h_last, new_conv, kn.reshape(b, tt, nmh, MOBA_HEAD_DIM), v_rows), (w_in_b, w_out_b)

    (xs, h_s, conv_s, mk_s, mv_s), (w_in0_b, w_out0_b) = even(
        xs, db, t, state_rglru_h, state_rglru_conv,
        lambda z: moba_sample(z, pool_mk, pool_mv, page_table, bsamp, l0_q_norm_g, l0_k_norm_g, w),
        l0_w_in, l0_w_out, True)
    (xp, h_p, conv_p, mk_p, mv_p), _ = even(
        xp, bp, s, jnp.zeros((bp, w), F32), jnp.zeros((bp, l0_conv_w.shape[0] - 1, w), F32),
        lambda z: moba_prompt(z, rel_bias_table, bt, l0_q_norm_g, l0_k_norm_g, w),
        w_in0_b, w_out0_b, False)

    xs, f0_s, ffn0_wb = _conv_ffn(xs, t, state_ffn0_conv, ffn0_norm_g, (ffn0_w_up, ffn0_w_down),
                                  ffn0_conv_w, ffn0_conv_b)
    xp, f0_p, _ = _conv_ffn(xp, s, jnp.zeros((bp, ffn0_conv_w.shape[0] - 1, f2), F32), ffn0_norm_g, ffn0_wb,
                            ffn0_conv_w, ffn0_conv_b)

    wq = nq * DSA_HEAD_DIM
    wkv = nkv * DSA_HEAD_DIM
    wqi = IDX_HEADS * IDX_DIM
    n_in = l1_w_in.shape[1]
    assert n_in == wq + 2 * wkv + wqi + IDX_DIM + IDX_HEADS
    n_pad = -(-n_in // 512) * 512
    w_in1 = jnp.pad(l1_w_in, ((0, 0), (0, n_pad - n_in)))
    k_blk0 = wq // LANES
    v_blk0 = (wq + wkv) // LANES
    qi_off = wq + 2 * wkv
    assert qi_off % wqi == 0 and (qi_off + wqi) % LANES == 0 and (wq + wkv) % wkv == 0
    qi_blk = qi_off // wqi
    kw_blk = (qi_off + wqi) // LANES
    pool_dk = cache_dsa_k.reshape(n_phys * PAGE_SIZE * nkv, DSA_HEAD_DIM)
    pool_dv = cache_dsa_v.reshape(n_phys * PAGE_SIZE * nkv, DSA_HEAD_DIM)
    pool_ik = cache_dsa_idx_k.transpose(0, 2, 1)

    def odd(x2, b, tt, attend, w_in, w_out, emit):
        z2 = norm_matmul(x2, l1_norm_g, w_in, emit)
        z2, w_in_b = z2 if emit else (z2, None)
        kn, knb, vb = kv_prep(z2, l1_k_norm_g, k_blk0, v_blk0, nkv)
        z = z2.reshape(b, tt, n_pad)
        y = attend(z, kn.reshape(b, tt, wkv), knb.reshape(b, tt, wkv), vb.reshape(b, tt, wkv))
        out = matmul_residual((y.reshape(b * tt, wq),), w_out, x2, emit)
        out, w_out_b = out if emit else (out, None)
        v_rows = z[:, :, wq + wkv:wq + 2 * wkv].reshape(b, tt, nkv, DSA_HEAD_DIM)
        ki_rows = z[:, :, qi_off + wqi:qi_off + wqi + IDX_DIM]
        return (out, kn.reshape(b, tt, nkv, DSA_HEAD_DIM), v_rows, ki_rows), (w_in_b, w_out_b)

    def attend_prompt(z, kn, knb, vb):
        isc = index_prompt(z, qi_blk, kw_blk)
        mask = topk_mask(isc.reshape(bp * s, s), min(DSA_TOPK, s // 4), s, 0, True).reshape(bp, s, s)
        return dsa_prompt(z, knb, vb, mask, rel_bias_table, bt, l1_q_norm_g, nq, nkv)

    def attend_sample(z, kn, knb, vb):
        isc = index_sample(z, pool_ik, page_table, qi_blk, kw_blk)
        mask = topk_mask(isc.reshape(db * t, lpad), min(DSA_TOPK, (past + t) // 4), t, past, False)
        return dsa_sample(z, kn, mask.reshape(db, t, lpad), pool_dk, pool_dv, page_table, bsamp, bsx,
                          l1_q_norm_g, nq, nkv, (wq + wkv) // wkv)

    (xs, dk_s, dv_s, di_s), (w_in1_b, w_out1_b) = odd(xs, db, t, attend_sample, w_in1, l1_w_out, True)
    (xp, dk_p, dv_p, di_p), _ = odd(xp, bp, s, attend_prompt, w_in1_b, w_out1_b, False)

    xs, f1_s, ffn1_wb = _conv_ffn(xs, t, state_ffn1_conv, ffn1_norm_g, (ffn1_w_up, ffn1_w_down),
                                  ffn1_conv_w, ffn1_conv_b)
    xp, f1_p, _ = _conv_ffn(xp, s, jnp.zeros((bp, ffn1_conv_w.shape[0] - 1, f2), F32), ffn1_norm_g, ffn1_wb,
                            ffn1_conv_w, ffn1_conv_b)

    return (xp.reshape(bp, s, d), xs.reshape(db, t, d), h_p, h_s, conv_p, conv_s, mk_p, mk_s, mv_p, mv_s,
            f0_p, f0_s, dk_p, dk_s, dv_p, dv_s, di_p, di_s, f1_p, f1_s)
```

```python
import functools
import math

import numpy as np
import jax
import jax.numpy as jnp
from jax import lax
from jax.experimental import pallas as pl
from jax.experimental.pallas import tpu as pltpu

F32 = jnp.float32
BF16 = jnp.bfloat16
I32 = jnp.int32
NEG_INF = float("-inf")
INT_MIN = -(2 ** 31)

EPS = 1e-6
LRU_C = 8.0
LRU_GROUP = 256
MOBA_BLOCK = 256
MOBA_TOPK = 3
MOBA_HEAD_DIM = 64
DSA_HEAD_DIM = 128
DSA_TOPK = 256
IDX_DIM = 64
IDX_HEADS = 8
PAGE_SIZE = 128
N_BUCKETS = 32
MAX_DISTANCE = 128
Q_TILE = 256
MOBA_ROUND_TILES = 2
LANES = 128
MIB = 1024 * 1024


def _t5_thresholds():
    exact = N_BUCKETS // 2
    n = np.arange(0, 4 * MAX_DISTANCE)
    nf = np.maximum(n, 1).astype(np.float32)
    large = exact + (np.log(nf / np.float32(exact)) / np.float32(math.log(MAX_DISTANCE / exact))
                     * np.float32(N_BUCKETS - exact)).astype(np.int32)
    bucket = np.where(n < exact, n, np.minimum(large, N_BUCKETS - 1))
    assert np.all(np.diff(bucket) >= 0) and bucket[-1] == N_BUCKETS - 1
    return [int(np.argmax(bucket >= b)) for b in range(N_BUCKETS)]


T5_THR = _t5_thresholds()


def _cp(sem, vmem_mib=48):
    return pltpu.CompilerParams(dimension_semantics=sem, vmem_limit_bytes=vmem_mib * MIB)


def _iota(shape, dim):
    return lax.broadcasted_iota(I32, shape, dim)


def _dot_nt(a, b):
    return lax.dot_general(a, b, (((1,), (1,)), ((), ())), preferred_element_type=F32)


def _dot_nn(a, b):
    return jnp.dot(a, b, preferred_element_type=F32)


def _split2(x):
    hi = x.astype(BF16)
    lo = (x - hi.astype(F32)).astype(BF16)
    return hi, lo


def _split3(x):
    hi = x.astype(BF16)
    r = x - hi.astype(F32)
    mid = r.astype(BF16)
    lo = (r - mid.astype(F32)).astype(BF16)
    return hi, mid, lo


def _dot3(dot, a, b):
    ah, al = _split2(a)
    bh, bl = _split2(b)
    return dot(ah, bh) + (dot(ah, bl) + dot(al, bh))


def _dot_exact01(dot, a, b01):
    h, m, l = _split3(a)
    return dot(h, b01) + (dot(m, b01) + dot(l, b01))


def _t5_bias(dist, tab_ref, h):
    acc = jnp.full(dist.shape, tab_ref[0, h], F32)
    for b in range(1, N_BUCKETS):
        acc = jnp.where(dist >= T5_THR[b], tab_ref[b, h], acc)
    return acc


def _softplus(x):
    return jnp.maximum(x, 0.0) + jnp.log1p(jnp.exp(-jnp.abs(x)))


def _norm_mm_kernel(x_ref, g_ref, w_ref, o_ref, *rest, emit):
    xn_ref = rest[-1]

    @pl.when(pl.program_id(1) == 0)
    def _():
        x = x_ref[...]
        ms = jnp.mean(x * x, axis=-1, keepdims=True)
        xn_ref[...] = (x * lax.rsqrt(ms + EPS) * g_ref[...]).astype(BF16)

    wb = w_ref[...].astype(BF16)
    if emit:
        rest[0][...] = wb
    o_ref[...] = _dot_nn(xn_ref[...], wb)


def norm_matmul(x, g, w, emit=False):
    m, d = x.shape
    n = w.shape[1]
    tm = min(m, 1024)
    tn = 1024 if (not emit and n % 1024 == 0) else 512
    assert m % tm == 0 and n % tn == 0 and (not emit or m == tm)
    out_specs = [pl.BlockSpec((tm, tn), lambda i, j: (i, j))]
    out_shape = [jax.ShapeDtypeStruct((m, n), F32)]
    if emit:
        out_specs.append(pl.BlockSpec((d, tn), lambda i, j: (0, j)))
        out_shape.append(jax.ShapeDtypeStruct((d, n), BF16))
    res = pl.pallas_call(
        functools.partial(_norm_mm_kernel, emit=emit),
        grid=(m // tm, n // tn),
        in_specs=[pl.BlockSpec((tm, d), lambda i, j: (i, 0)),
                  pl.BlockSpec((1, d), lambda i, j: (0, 0)),
                  pl.BlockSpec((d, tn), lambda i, j: (0, j))],
        out_specs=out_specs,
        out_shape=out_shape,
        scratch_shapes=[pltpu.VMEM((tm, d), BF16)],
        compiler_params=_cp(("parallel", "arbitrary")),
        name="norm_matmul",
    )(x, g.reshape(1, d), w)
    return res if emit else res[0]


def _mm_res_kernel(*refs, emit, na):
    a_refs = refs[:na]
    w_ref, r_ref, o_ref = refs[na:na + 3]
    wb = w_ref[...].astype(BF16)
    if emit:
        refs[na + 3][...] = wb
    acc = r_ref[...]
    k0 = 0
    for a_ref in a_refs:
        k1 = k0 + a_ref.shape[1]
        acc = acc + _dot_nn(a_ref[...], wb[k0:k1, :])
        k0 = k1
    o_ref[...] = acc


def matmul_residual(a_parts, w, res, emit=False):
    m = a_parts[0].shape[0]
    kdim, n = w.shape
    tm = min(m, 1024)
    tn = 256 if emit else (1024 if kdim <= 2048 and n % 1024 == 0 else 512)
    assert m % tm == 0 and n % tn == 0 and (not emit or m == tm) and sum(a.shape[1] for a in a_parts) == kdim
    out_specs = [pl.BlockSpec((tm, tn), lambda i, j: (i, j))]
    out_shape = [jax.ShapeDtypeStruct((m, n), F32)]
    if emit:
        out_specs.append(pl.BlockSpec((kdim, tn), lambda i, j: (0, j)))
        out_shape.append(jax.ShapeDtypeStruct((kdim, n), BF16))
    out = pl.pallas_call(
        functools.partial(_mm_res_kernel, emit=emit, na=len(a_parts)),
        grid=(m // tm, n // tn),
        in_specs=[pl.BlockSpec((tm, a.shape[1]), lambda i, j: (i, 0)) for a in a_parts]
                 + [pl.BlockSpec((kdim, tn), lambda i, j: (0, j)),
                    pl.BlockSpec((tm, tn), lambda i, j: (i, j))],
        out_specs=out_specs,
        out_shape=out_shape,
        compiler_params=_cp(("parallel", "arbitrary"), vmem_mib=56),
        name="matmul_residual",
    )(*a_parts, w, res)
    return out if emit else out[0]


def _ffn_up_kernel(x_ref, xp_ref, g_ref, wv_ref, wg_ref, sv_ref, sg_ref, cwv_ref, cwg_ref, cbv_ref, cbg_ref,
                   o_ref, nsv_ref, nsg_ref, *rest, nb, tc, chunks, emit):
    xn_ref, xnp_ref = rest[-2:]
    i = pl.program_id(0)
    tm, tn = o_ref.shape

    def norm(x):
        return x * lax.rsqrt(jnp.mean(x * x, axis=-1, keepdims=True) + EPS) * g_ref[...]

    @pl.when(pl.program_id(1) == 0)
    def _():
        xn_ref[...] = norm(x_ref[...]).astype(BF16)
        if chunks > 1:
            xnp_ref[...] = norm(xp_ref[...])

    def half(w_ref, wb_ref, s_ref, cw_ref, cb_ref, ns_ref):
        wb = w_ref[...].astype(BF16)
        if emit:
            wb_ref[...] = wb
        u = _dot_nn(xn_ref[...], wb)
        cw = cw_ref[...]
        if nb > 1:
            step = _iota((tm, tn), 0) % tc
            s0 = jnp.broadcast_to(s_ref[:, 0:1, :], (nb, tc, tn)).reshape(tm, tn)
            s1 = jnp.broadcast_to(s_ref[:, 1:2, :], (nb, tc, tn)).reshape(tm, tn)
            u1 = jnp.where(step == 0, s1, pltpu.roll(u, 1, 0))
            u2 = jnp.where(step == 0, s0, jnp.where(step == 1, s1, pltpu.roll(u, 2, 0)))
            ns_ref[...] = u.reshape(nb, tc, tn)[:, tc - 2:tc, :]
            return cb_ref[...] + u2 * cw[0:1] + u1 * cw[1:2] + u * cw[2:3]
        ns_ref[0] = u[tm - 2:tm, :]
        prev = s_ref[0]
        if chunks > 1:
            up = _dot_nn(xnp_ref[...].astype(BF16), wb)
            prev = jnp.where(i % chunks == 0, prev, up[6:8, :])
        u1 = pltpu.roll(u, 1, 0)
        u2 = pltpu.roll(u, 2, 0)
        c = cb_ref[...] + u2 * cw[0:1] + u1 * cw[1:2] + u * cw[2:3]
        row = _iota((8, tn), 0)
        t1 = jnp.where(row == 0, prev[1:2, :], u1[0:8, :])
        t2 = jnp.where(row == 0, prev[0:1, :], jnp.where(row == 1, prev[1:2, :], u2[0:8, :]))
        top = cb_ref[...] + t2 * cw[0:1] + t1 * cw[1:2] + u[0:8, :] * cw[2:3]
        return jnp.concatenate([top, c[8:, :]], axis=0)

    wbv_ref, wbg_ref = (rest[0], rest[1]) if emit else (None, None)
    val = half(wv_ref, wbv_ref, sv_ref, cwv_ref, cbv_ref, nsv_ref)
    gate = half(wg_ref, wbg_ref, sg_ref, cwg_ref, cbg_ref, nsg_ref)
    o_ref[...] = (jax.nn.gelu(gate) * val).astype(o_ref.dtype)


def ffn_up(x, seq_len, state, norm_g, w_val, w_gate, gate_col, conv_w, conv_b, emit=False):
    m, d = x.shape
    f2 = conv_w.shape[1]
    f = f2 // 2
    tn = 256 if emit else 512
    tm = min(m, 1024)
    tc = min(seq_len, tm)
    nb = tm // tc
    chunks = seq_len // tc
    assert m % tm == 0 and tm % tc == 0 and seq_len % tc == 0 and tc % 8 == 0 and (nb == 1 or chunks == 1)
    assert f % tn == 0 and gate_col % tn == 0 and conv_w.shape[0] == 3 and (not emit or m == tm)
    ncb = f // tn
    goff = gate_col // tn
    r8 = tm // 8
    nseq = m // seq_len

    def wspec(rows, off):
        return pl.BlockSpec((rows, tn), lambda i, j: (0, j + off))

    def st(off):
        return pl.BlockSpec((nb, 2, tn), lambda i, j: (i // chunks, 0, j + off))

    tail = pl.BlockSpec((nb, 2, tn), lambda i, j: (i, 0, j))
    out_specs = [pl.BlockSpec((tm, tn), lambda i, j: (i, j)), tail, tail]
    out_shape = [jax.ShapeDtypeStruct((m, f), BF16), jax.ShapeDtypeStruct((nseq * chunks, 2, f), F32),
                 jax.ShapeDtypeStruct((nseq * chunks, 2, f), F32)]
    if emit:
        out_specs += [wspec(d, 0), wspec(d, 0)]
        out_shape += [jax.ShapeDtypeStruct((d, f), BF16), jax.ShapeDtypeStruct((d, f), BF16)]
    return pl.pallas_call(
        functools.partial(_ffn_up_kernel, nb=nb, tc=tc, chunks=chunks, emit=emit),
        grid=(m // tm, ncb),
        in_specs=[pl.BlockSpec((tm, d), lambda i, j: (i, 0)),
                  pl.BlockSpec((8, d), lambda i, j: (jnp.maximum(i * r8 - 1, 0), 0)),
                  pl.BlockSpec((1, d), lambda i, j: (0, 0)),
                  wspec(d, 0), wspec(d, goff), st(0), st(ncb),
                  wspec(3, 0), wspec(3, ncb), wspec(1, 0), wspec(1, ncb)],
        out_specs=out_specs,
        out_shape=out_shape,
        scratch_shapes=[pltpu.VMEM((tm, d), BF16), pltpu.VMEM((8, d), F32)],
        compiler_params=_cp(("parallel", "arbitrary"), vmem_mib=56),
        name="ffn_up",
    )(x, x, norm_g.reshape(1, d), w_val, w_gate, state, state, conv_w, conv_w,
      conv_b.reshape(1, f2), conv_b.reshape(1, f2))


def _rglru_kernel(xa_ref, ga_ref, pv_ref, cs_ref, h0_ref, cw_ref, cb_ref, wa_ref, ba_ref,
                  wi_ref, bi_ref, lam_ref, ya_ref, hl_ref, e_ref, hc_ref, *, tc, bb):
    t = pl.program_id(1)
    w = xa_ref.shape[-1]
    rows = bb * tc

    @pl.when(t == 0)
    def _():
        hc_ref[...] = h0_ref[...]
        e_ref[:, 5:8, :] = cs_ref[...]

    @pl.when(t > 0)
    def _():
        e_ref[:, 5:8, :] = pv_ref[:, 5:8, :]

    e_ref[:, 8:8 + tc, :] = xa_ref[...]
    cw = cw_ref[...]
    xc = (cb_ref[...] + e_ref[:, 5:5 + tc, :] * cw[0:1] + e_ref[:, 6:6 + tc, :] * cw[1:2]
          + e_ref[:, 7:7 + tc, :] * cw[2:3] + e_ref[:, 8:8 + tc, :] * cw[3:4]).reshape(rows, w)
    xb = xc.astype(BF16)

    def gate(w_ref, b_ref):
        parts = [_dot_nn(xb[:, g * LRU_GROUP:(g + 1) * LRU_GROUP], w_ref[g])
                 for g in range(w // LRU_GROUP)]
        return jax.nn.sigmoid(jnp.concatenate(parts, axis=1) + b_ref[...])

    r = gate(wa_ref, ba_ref)
    i = gate(wi_ref, bi_ref)
    log_a = -LRU_C * r * _softplus(-lam_ref[...])
    a = jnp.exp(log_a)
    u = jnp.sqrt(-jnp.tanh(log_a) * (1.0 + a * a)) * (i * xc)

    step = _iota((rows, w), 0) % tc
    s = 1
    while s < tc:
        a_sh = jnp.where(step >= s, pltpu.roll(a, s, 0), 1.0)
        u_sh = jnp.where(step >= s, pltpu.roll(u, s, 0), 0.0)
        u = u + a * u_sh
        a = a * a_sh
        s *= 2
    h = u.reshape(bb, tc, w) + a.reshape(bb, tc, w) * hc_ref[...]
    hlast = h[:, tc - 1:tc, :]
    hc_ref[...] = hlast
    hl_ref[...] = hlast
    ya_ref[...] = (h * jax.nn.gelu(ga_ref[...])).astype(ya_ref.dtype)


def rglru_branch(z, h0, conv_state, conv_w, conv_b, wa_bd, ba, wi_bd, bi, lam):
    b, t, _ = z.shape
    w = h0.shape[-1]
    tc = min(t, 256)
    bb = 8 if (tc == t and tc == 8 and b % 8 == 0) else 1
    assert t % tc == 0 and tc % 8 == 0 and conv_w.shape[0] == 4 and w % LRU_GROUP == 0
    r8 = tc // 8
    ng = w // LRU_GROUP
    vec = lambda: pl.BlockSpec((1, w), lambda i, j: (0, 0))
    ya, hl = pl.pallas_call(
        functools.partial(_rglru_kernel, tc=tc, bb=bb),
        grid=(b // bb, t // tc),
        in_specs=[pl.BlockSpec((bb, tc, w), lambda i, j: (i, j, 0)),
                  pl.BlockSpec((bb, tc, w), lambda i, j: (i, j, 1)),
                  pl.BlockSpec((bb, 8, w), lambda i, j: (i, jnp.maximum(j * r8 - 1, 0), 0)),
                  pl.BlockSpec((bb, 3, w), lambda i, j: (i, 0, 0)),
                  pl.BlockSpec((bb, 1, w), lambda i, j: (i, 0, 0)),
                  pl.BlockSpec((4, w), lambda i, j: (0, 0)),
                  vec(),
                  pl.BlockSpec((ng, LRU_GROUP, LRU_GROUP), lambda i, j: (0, 0, 0)),
                  vec(),
                  pl.BlockSpec((ng, LRU_GROUP, LRU_GROUP), lambda i, j: (0, 0, 0)),
                  vec(), vec()],
        out_specs=[pl.BlockSpec((bb, tc, w), lambda i, j: (i, j, 0)),
                   pl.BlockSpec((bb, 1, w), lambda i, j: (i, 0, 0))],
        out_shape=[jax.ShapeDtypeStruct((b, t, w), BF16), jax.ShapeDtypeStruct((b, 1, w), F32)],
        scratch_shapes=[pltpu.VMEM((bb, tc + 8, w), F32), pltpu.VMEM((bb, 1, w), F32)],
        compiler_params=_cp(("parallel", "arbitrary")),
        name="rglru",
    )(z, z, z, conv_state, h0.reshape(b, 1, w), conv_w, conv_b.reshape(1, w), wa_bd,
      ba.reshape(1, w), wi_bd, bi.reshape(1, w), lam.reshape(1, w))
    return ya, hl.reshape(b, w)


def _regroup_blockdiag(wb):
    n, c, _ = wb.shape
    per = LRU_GROUP // c
    g = n // per
    eye = jnp.eye(per, dtype=wb.dtype)
    out = jnp.einsum("gacd,ab->gacbd", wb.reshape(g, per, c, c), eye)
    return out.reshape(g, LRU_GROUP, LRU_GROUP).astype(BF16)


def _bias_kernel(tab_ref, bt_ref, bs_ref, *, past, lpad):
    h = pl.program_id(0)
    d0 = _iota((Q_TILE, Q_TILE), 0) - _iota((Q_TILE, Q_TILE), 1)
    bt_ref[0, 0] = _t5_bias(d0, tab_ref, h)
    bt_ref[0, 1] = _t5_bias(d0 + Q_TILE, tab_ref, h)
    t = bs_ref.shape[1]
    ds = past + _iota((t, lpad), 0) - _iota((t, lpad), 1)
    bs_ref[0] = _t5_bias(ds, tab_ref, h)


def bias_tiles(table, past, t_new, lpad):
    nh = table.shape[1]
    return pl.pallas_call(
        functools.partial(_bias_kernel, past=past, lpad=lpad),
        grid=(nh,),
        in_specs=[pl.BlockSpec(memory_space=pltpu.SMEM)],
        out_specs=[pl.BlockSpec((1, 2, Q_TILE, Q_TILE), lambda h: (h, 0, 0, 0)),
                   pl.BlockSpec((1, t_new, lpad), lambda h: (h, 0, 0))],
        out_shape=[jax.ShapeDtypeStruct((nh, 2, Q_TILE, Q_TILE), F32),
                   jax.ShapeDtypeStruct((nh, t_new, lpad), F32)],
        compiler_params=_cp(("arbitrary",)),
        name="t5_bias_tiles",
    )(table)


def _top_lanes(g, nvalid, k):
    lane = _iota(g.shape, 1).astype(F32)
    gg = jnp.where(lane < nvalid, g, NEG_INF)
    sel = jnp.zeros(g.shape, F32)
    for _ in range(k):
        mx = jnp.max(gg, axis=-1, keepdims=True)
        first = jnp.min(jnp.where(gg == mx, lane, float(g.shape[1])), axis=-1, keepdims=True)
        pick = (lane == first) & (mx > NEG_INF)
        sel = jnp.where(pick, 1.0, sel)
        gg = jnp.where(pick, NEG_INF, gg)
    return sel


def _moba_prompt_kernel(tab_ref, q_ref, k_ref, v_ref, qg_ref, kg_ref, bt_ref, y_ref, kn_ref, *, s):
    hp = pl.program_id(0)
    nb = s // MOBA_BLOCK
    dh = MOBA_HEAD_DIM
    scale = dh ** -0.5
    lane = _iota((1, LANES), 1)
    low = lane < dh

    def headnorm(x, g):
        x2 = x * x
        s0 = jnp.sum(jnp.where(low, x2, 0.0), axis=-1, keepdims=True)
        s1 = jnp.sum(jnp.where(low, 0.0, x2), axis=-1, keepdims=True)
        ms = jnp.where(low, s0, s1) * (1.0 / dh)
        return x * lax.rsqrt(ms + EPS) * g

    qn = headnorm(q_ref[0], qg_ref[...])
    kn = headnorm(k_ref[0], kg_ref[...])
    kn_ref[0] = kn
    kb = kn.astype(BF16)
    vb = v_ref[0].astype(BF16)
    blk_row = _iota((LANES, LANES), 0)
    kmat = jnp.zeros((LANES, LANES), F32)
    for j in range(nb):
        kmean = jnp.mean(kn[j * MOBA_BLOCK:(j + 1) * MOBA_BLOCK], axis=0, keepdims=True)
        kmat = jnp.where(blk_row == j, kmean, kmat)
    qi = _iota((Q_TILE, Q_TILE), 0)
    kj = _iota((Q_TILE, Q_TILE), 1)
    causal = qi >= kj

    near = []
    for hh in range(2):
        t31 = tab_ref[N_BUCKETS - 1, 2 * hp + hh]
        near.append((jnp.where(causal, bt_ref[hh, 0] - t31, NEG_INF), bt_ref[hh, 1] - t31))

    for c0 in range(0, nb, MOBA_ROUND_TILES):
        work = [(c, hh) for c in range(c0, min(c0 + MOBA_ROUND_TILES, nb)) for hh in range(2)]
        scs, sels = {}, {}
        for c, hh in work:
            hmask = low if hh == 0 else jnp.logical_not(low)
            qm = jnp.where(hmask, qn[c * Q_TILE:(c + 1) * Q_TILE], 0.0)
            scs[c, hh] = _dot_nt((qm * scale).astype(BF16), kb[0:(c + 1) * MOBA_BLOCK])
            if c > MOBA_TOPK:
                sels[c, hh] = _top_lanes(_dot3(_dot_nt, qm, kmat), c, MOBA_TOPK)
        probs, sums = {}, {}
        for c, hh in work:
            pieces = []
            for j in range(c + 1):
                sj = scs[c, hh][:, j * MOBA_BLOCK:(j + 1) * MOBA_BLOCK]
                if j == c:
                    sj = sj + near[hh][0]
                else:
                    if j == c - 1:
                        sj = sj + near[hh][1]
                    if c > MOBA_TOPK:
                        sj = jnp.where(sels[c, hh][:, j:j + 1] > 0.0, sj, NEG_INF)
                pieces.append(sj)
            srow = jnp.concatenate(pieces, axis=1) if c else pieces[0]
            mx = jnp.max(srow, axis=-1, keepdims=True)
            p = jnp.exp(srow - mx)
            sums[c, hh] = jnp.sum(p, axis=-1, keepdims=True)
            probs[c, hh] = p.astype(BF16)
        for c in sorted({c for c, _ in work}):
            outs = [_dot_nn(probs[c, hh], vb[0:(c + 1) * MOBA_BLOCK]) / sums[c, hh] for hh in range(2)]
            y_ref[0, c * Q_TILE:(c + 1) * Q_TILE, :] = jnp.where(low, outs[0], outs[1]).astype(y_ref.dtype)


def moba_prompt(z, table, bt, q_gain, k_gain, w):
    b, s, _ = z.shape
    assert s % MOBA_BLOCK == 0 and Q_TILE == MOBA_BLOCK and s // MOBA_BLOCK <= LANES
    nhp = w // LANES
    g2 = lambda g: jnp.tile(g, LANES // MOBA_HEAD_DIM).reshape(1, LANES)
    col = lambda off: pl.BlockSpec((1, s, LANES), lambda hp, i: (i, 0, off * nhp + hp))
    gspec = pl.BlockSpec((1, LANES), lambda hp, i: (0, 0))
    return pl.pallas_call(
        functools.partial(_moba_prompt_kernel, s=s),
        grid=(nhp, b),
        in_specs=[pl.BlockSpec(memory_space=pltpu.SMEM), col(2), col(3), col(4), gspec, gspec,
                  pl.BlockSpec((2, 2, Q_TILE, Q_TILE), lambda hp, i: (hp, 0, 0, 0))],
        out_specs=[pl.BlockSpec((1, s, LANES), lambda hp, i: (i, 0, hp)),
                   pl.BlockSpec((1, s, LANES), lambda hp, i: (i, 0, hp))],
        out_shape=[jax.ShapeDtypeStruct((b, s, w), BF16), jax.ShapeDtypeStruct((b, s, w), F32)],
        compiler_params=_cp(("parallel", "parallel")),
        name="moba_prompt",
    )(table, z, z, z, g2(q_gain), g2(k_gain), bt)


def _group_onehot(ngroups_pad, width, gsize):
    return jnp.where(_iota((ngroups_pad, width), 0) == _iota((ngroups_pad, width), 1) // gsize,
                     1.0, 0.0).astype(BF16)


def _group_rmsnorm(x, gain, gsize):
    onehot = _group_onehot(LANES, x.shape[1], gsize)
    ssq = _dot_exact01(_dot_nt, x * x, onehot)
    rinv = lax.rsqrt(ssq * (1.0 / gsize) + EPS)
    return x * _dot_exact01(_dot_nn, rinv, onehot) * gain


def _moba_sample_kernel(pt_ref, q_ref, k_ref, v_ref, qg_ref, kg_ref, bs_ref, *rest, npages, t):
    kp_refs = rest[:npages]
    vp_refs = rest[npages:2 * npages]
    y_ref, kn_ref = rest[2 * npages:]
    w = q_ref.shape[-1]
    dh = MOBA_HEAD_DIM
    nh = w // dh
    rows = nh * t
    ppb = MOBA_BLOCK // PAGE_SIZE
    nbp = npages // ppb
    past = npages * PAGE_SIZE
    scale = dh ** -0.5

    qn = _group_rmsnorm(q_ref[0], qg_ref[...], dh)
    kn = _group_rmsnorm(k_ref[0], kg_ref[...], dh)
    kn_ref[0] = kn
    own = _iota((rows, w), 0) // t == _iota((rows, w), 1) // dh
    qbd = jnp.where(own, jnp.concatenate([qn] * nh, axis=0), 0.0)
    qbb = qbd.astype(BF16)
    pad_rows = lambda x: jnp.concatenate([x, jnp.zeros((LANES - t, w), x.dtype)], axis=0)

    pieces = [_dot_nn(qbb, kp_refs[p][0].astype(BF16)) for p in range(npages)]
    pieces.append(_dot_nt(qbb, pad_rows(kn).astype(BF16)))
    sums = [jnp.sum(kp_refs[p][0], axis=1, keepdims=True) for p in range(npages)]
    lpad = past + LANES
    srow = jnp.concatenate(pieces, axis=1) * scale + bs_ref[...]

    blk_lane = _iota((w, LANES), 1)
    kmat = jnp.zeros((w, LANES), F32)
    for j in range(nbp):
        kmat = jnp.where(blk_lane == j, sum(sums[j * ppb:(j + 1) * ppb]) * (1.0 / MOBA_BLOCK), kmat)
    sel = _top_lanes(_dot3(_dot_nn, qbd, kmat), nbp, MOBA_TOPK)
    col = _iota((rows, lpad), 1)
    jrow = _iota((rows, lpad), 0) % t
    keep = (col >= past) & (col - past <= jrow)
    for j in range(nbp):
        inblk = (col >= j * MOBA_BLOCK) & (col < (j + 1) * MOBA_BLOCK)
        keep = keep | (inblk & (sel[:, j:j + 1] > 0.0))
    srow = jnp.where(keep, srow, NEG_INF)
    mx = jnp.max(srow, axis=-1, keepdims=True)
    pr = jnp.exp(srow - mx)
    l = jnp.sum(pr, axis=-1, keepdims=True)
    pb = pr.astype(BF16)
    acc = _dot_nn(pb[:, past:], pad_rows(v_ref[0]).astype(BF16))
    for p in range(npages):
        acc = acc + _dot_nt(pb[:, p * PAGE_SIZE:(p + 1) * PAGE_SIZE], vp_refs[p][0].astype(BF16))
    acc = acc / l
    lane_head = _iota((t, w), 1) // dh
    y = jnp.zeros((t, w), F32)
    for h in range(nh):
        y = jnp.where(lane_head == h, acc[h * t:(h + 1) * t, :], y)
    y_ref[0] = y.astype(y_ref.dtype)


def moba_sample(z, pool_k, pool_v, page_table, bs, q_gain, k_gain, w):
    db, t, _ = z.shape
    npages = page_table.shape[1]
    nh = w // MOBA_HEAD_DIM
    assert t == 8 and nh * t == LANES and npages % (MOBA_BLOCK // PAGE_SIZE) == 0
    lpad = npages * PAGE_SIZE + LANES
    gt = lambda g: jnp.tile(g, nh).reshape(1, w)
    col = lambda off: pl.BlockSpec((1, t, w), lambda i, pt: (i, 0, off))
    gspec = pl.BlockSpec((1, w), lambda i, pt: (0, 0))
    page = lambda p: pl.BlockSpec((1, w, PAGE_SIZE), lambda i, pt: (pt[i, p], 0, 0))
    grid_spec = pltpu.PrefetchScalarGridSpec(
        num_scalar_prefetch=1,
        grid=(db,),
        in_specs=[col(2), col(3), col(4), gspec, gspec,
                  pl.BlockSpec((nh * t, lpad), lambda i, pt: (0, 0))]
                 + [page(p) for p in range(npages)] * 2,
        out_specs=[pl.BlockSpec((1, t, w), lambda i, pt: (i, 0, 0)),
                   pl.BlockSpec((1, t, w), lambda i, pt: (i, 0, 0))],
    )
    return pl.pallas_call(
        functools.partial(_moba_sample_kernel, npages=npages, t=t),
        grid_spec=grid_spec,
        out_shape=[jax.ShapeDtypeStruct((db, t, w), BF16), jax.ShapeDtypeStruct((db, t, w), F32)],
        compiler_params=_cp(("arbitrary",), vmem_mib=56),
        name="moba_sample",
    )(page_table, z, z, z, gt(q_gain), gt(k_gain), bs.reshape(nh * t, lpad),
      *([pool_k] * npages), *([pool_v] * npages))


def _kv_prep_kernel(k_ref, v_ref, g_ref, kn_ref, knb_ref, vb_ref):
    x = k_ref[...]
    ms = jnp.mean(x * x, axis=-1, keepdims=True)
    kn = x * lax.rsqrt(ms + EPS) * g_ref[...]
    kn_ref[...] = kn
    knb_ref[...] = kn.astype(BF16)
    vb_ref[...] = v_ref[...].astype(BF16)


def kv_prep(z, gain, k_blk0, v_blk0, nheads):
    m = z.shape[0]
    tm = min(m, 1024)
    assert m % tm == 0
    out = pl.BlockSpec((tm, LANES), lambda i, h: (i, h))
    return pl.pallas_call(
        _kv_prep_kernel,
        grid=(m // tm, nheads),
        in_specs=[pl.BlockSpec((tm, LANES), lambda i, h: (i, k_blk0 + h)),
                  pl.BlockSpec((tm, LANES), lambda i, h: (i, v_blk0 + h)),
                  pl.BlockSpec((1, LANES), lambda i, h: (0, 0))],
        out_specs=[out, out, out],
        out_shape=[jax.ShapeDtypeStruct((m, nheads * LANES), F32),
                   jax.ShapeDtypeStruct((m, nheads * LANES), BF16),
                   jax.ShapeDtypeStruct((m, nheads * LANES), BF16)],
        compiler_params=_cp(("parallel", "parallel")),
        name="kv_prep",
    )(z, z, gain.reshape(1, LANES))


def _index_prompt_kernel(qi_ref, qw_ref, kk_ref, o_ref, *, s):
    c = pl.program_id(1)
    lane = _iota((1, LANES), 1)
    low = lane < IDX_DIM
    qw = qw_ref[0]

    def split(x):
        hi = x.astype(BF16).astype(F32)
        return hi, (x - hi).astype(BF16).astype(F32)

    def tile(cc):
        nk = (cc + 1) * Q_TILE
        kh, kl = split(jnp.where(low, kk_ref[0, 0:nk, :], 0.0))
        rhs = jnp.concatenate([kh + pltpu.roll(kh, IDX_DIM, 1), kl], axis=1).astype(BF16)
        acc = jnp.zeros((Q_TILE, nk), F32)
        for pair in range(IDX_HEADS // 2):
            qh, ql = split(qi_ref[0, :, pair * LANES:(pair + 1) * LANES])
            qh_sw = pltpu.roll(qh, IDX_DIM, 1)
            ql_sw = pltpu.roll(ql, IDX_DIM, 1)
            for half, (hi_low, lo_high) in enumerate(((qh, ql_sw), (qh_sw, ql))):
                ih = 2 * pair + half
                lhs = jnp.concatenate([jnp.where(low, hi_low, lo_high), jnp.where(low, hi_low, 0.0)],
                                      axis=1).astype(BF16)
                sc = _dot_nt(lhs, rhs)
                wcol = qw[:, IDX_DIM + ih:IDX_DIM + ih + 1] * (IDX_HEADS ** -0.5)
                acc = acc + wcol * jnp.maximum(sc * (IDX_DIM ** -0.5), 0.0)
        qpos = cc * Q_TILE + _iota((Q_TILE, nk), 0)
        o_ref[0, :, 0:nk] = jnp.where(_iota((Q_TILE, nk), 1) <= qpos, acc, NEG_INF)
        if nk < s:
            o_ref[0, :, nk:] = jnp.full((Q_TILE, s - nk), NEG_INF, F32)

    for cc in range(s // Q_TILE):
        pl.when(c == cc)(functools.partial(tile, cc))


def index_prompt(z, qi_blk, kw_blk):
    b, s, _ = z.shape
    assert s % Q_TILE == 0
    return pl.pallas_call(
        functools.partial(_index_prompt_kernel, s=s),
        grid=(b, s // Q_TILE),
        in_specs=[pl.BlockSpec((1, Q_TILE, IDX_HEADS * IDX_DIM), lambda i, c: (i, c, qi_blk)),
                  pl.BlockSpec((1, Q_TILE, LANES), lambda i, c: (i, c, kw_blk)),
                  pl.BlockSpec((1, s, LANES), lambda i, c: (i, 0, kw_blk))],
        out_specs=pl.BlockSpec((1, Q_TILE, s), lambda i, c: (i, c, 0)),
        out_shape=jax.ShapeDtypeStruct((b, s, s), F32),
        compiler_params=_cp(("parallel", "parallel")),
        name="index_prompt",
    )(z, z, z)


def _index_sample_kernel(pt_ref, qi_ref, qw_ref, *rest, npages, t):
    kp_refs = rest[:npages]
    o_ref = rest[npages]
    qi = qi_ref[0]
    qw = qw_ref[0]
    q2 = jnp.concatenate([qi[:, ih * IDX_DIM:(ih + 1) * IDX_DIM] for ih in range(IDX_HEADS)], axis=0)
    wcol = jnp.concatenate([qw[:, IDX_DIM + ih:IDX_DIM + ih + 1] for ih in range(IDX_HEADS)], axis=0)
    wcol = wcol * (IDX_HEADS ** -0.5)
    knew = jnp.concatenate([qw[:, 0:IDX_DIM], jnp.zeros((LANES - t, IDX_DIM), F32)], axis=0)

    def scores(dot, keys):
        sc = wcol * jnp.maximum(_dot3(dot, q2, keys) * (IDX_DIM ** -0.5), 0.0)
        return sum(sc[ih * t:(ih + 1) * t, :] for ih in range(IDX_HEADS))

    pieces = [scores(_dot_nn, kp_refs[p][0]) for p in range(npages)]
    new = scores(_dot_nt, knew)
    jj = _iota((t, LANES), 0)
    cc = _iota((t, LANES), 1)
    pieces.append(jnp.where(cc <= jj, new, NEG_INF))
    o_ref[0] = jnp.concatenate(pieces, axis=1)


def index_sample(z, pool_ik, page_table, qi_blk, kw_blk):
    db, t, _ = z.shape
    npages = page_table.shape[1]
    lpad = npages * PAGE_SIZE + LANES
    grid_spec = pltpu.PrefetchScalarGridSpec(
        num_scalar_prefetch=1,
        grid=(db,),
        in_specs=[pl.BlockSpec((1, t, IDX_HEADS * IDX_DIM), lambda i, pt: (i, 0, qi_blk)),
                  pl.BlockSpec((1, t, LANES), lambda i, pt: (i, 0, kw_blk))]
                 + [pl.BlockSpec((1, IDX_DIM, PAGE_SIZE), (lambda i, pt, p=p: (pt[i, p], 0, 0)))
                    for p in range(npages)],
        out_specs=pl.BlockSpec((1, t, lpad), lambda i, pt: (i, 0, 0)),
    )
    return pl.pallas_call(
        functools.partial(_index_sample_kernel, npages=npages, t=t),
        grid_spec=grid_spec,
        out_shape=jax.ShapeDtypeStruct((db, t, lpad), F32),
        compiler_params=_cp(("arbitrary",)),
        name="index_sample",
    )(page_table, z, z, *([pool_ik] * npages))


def _topk_mask_kernel(x_ref, o_ref, *, topk, period, base, tr, additive, ncausal):
    l = x_ref.shape[1]
    on, off = (0.0, NEG_INF) if additive else (1.0, 0.0)

    def body(nk):
        rowg = pl.program_id(0) * tr + _iota((tr, nk), 0)
        valid = _iota((tr, nk), 1) <= base + rowg % period
        bits = pltpu.bitcast(x_ref[:, 0:nk], I32)
        key = jnp.where(bits < 0, bits ^ 0x7FFFFFFF, bits)
        key = jnp.where(valid, key, INT_MIN)

        def step(i, lo):
            cand = lo + lax.shift_left(jnp.int32(1), 31 - i)
            cnt = jnp.sum(jnp.where(key >= cand, 1.0, 0.0), axis=-1, keepdims=True)
            return jnp.where(cnt >= topk, cand, lo)

        thr = lax.fori_loop(0, 32, step, jnp.full((tr, 1), INT_MIN, I32))
        gt = key > thr
        need = topk - jnp.sum(jnp.where(gt, 1.0, 0.0), axis=-1, keepdims=True)
        eq = jnp.where(key == thr, 1.0, 0.0)
        upper = jnp.where(_iota((LANES, LANES), 0) < _iota((LANES, LANES), 1), 1.0, 0.0).astype(BF16)
        before = jnp.zeros((tr, 1), F32)
        for j in range(nk // LANES):
            e = eq[:, j * LANES:(j + 1) * LANES]
            rank = before + _dot_nn(e.astype(BF16), upper)
            pick = gt[:, j * LANES:(j + 1) * LANES] | ((e > 0.0) & (rank < need))
            pick = pick & valid[:, j * LANES:(j + 1) * LANES]
            o_ref[:, j * LANES:(j + 1) * LANES] = jnp.where(pick, on, off)
            before = before + jnp.sum(e, axis=-1, keepdims=True)
        if nk < l:
            o_ref[:, nk:] = jnp.full((tr, l - nk), off, F32)

    if ncausal > 1:
        c = pl.program_id(0) % ncausal
        for cc in range(ncausal):
            pl.when(c == cc)(functools.partial(body, (cc + 1) * tr))
    else:
        body(l)


def topk_mask(scores, topk, period, base, additive):
    r, l = scores.shape
    tr = min(r, 256)
    assert r % tr == 0 and l % LANES == 0
    ncausal = period // tr if (base == 0 and period == l and period % tr == 0) else 1
    return pl.pallas_call(
        functools.partial(_topk_mask_kernel, topk=topk, period=period, base=base, tr=tr, additive=additive,
                          ncausal=ncausal),
        grid=(r // tr,),
        in_specs=[pl.BlockSpec((tr, l), lambda i: (i, 0))],
        out_specs=pl.BlockSpec((tr, l), lambda i: (i, 0)),
        out_shape=jax.ShapeDtypeStruct((r, l), F32),
        compiler_params=_cp(("parallel",)),
        name="topk_mask",
    )(scores)


def _dsa_prompt_kernel(tab_ref, q_ref, k_ref, v_ref, m_ref, bt_ref, qg_ref, y_ref, *, s):
    c = pl.program_id(1)
    g = pl.program_id(2)
    dh = DSA_HEAD_DIM
    grp = q_ref.shape[-1] // dh
    qbs, t31s = [], []
    for hh in range(grp):
        q = q_ref[0, :, hh * dh:(hh + 1) * dh]
        qn = q * lax.rsqrt(jnp.mean(q * q, axis=-1, keepdims=True) + EPS) * qg_ref[...]
        qbs.append((qn * (dh ** -0.5)).astype(BF16))
        t31s.append(tab_ref[N_BUCKETS - 1, g * grp + hh])

    def tile(cc):
        nk = (cc + 1) * Q_TILE
        madd = m_ref[0, :, 0:nk]
        scs = [_dot_nt(qbs[hh], k_ref[0, 0:nk, :]) for hh in range(grp)]
        probs, sums = [], []
        for hh in range(grp):
            sc = scs[hh] + madd
            pieces = [sc[:, 0:(cc - 1) * Q_TILE]] if cc > 1 else []
            if cc > 0:
                pieces.append(sc[:, (cc - 1) * Q_TILE:cc * Q_TILE] + (bt_ref[hh, 1] - t31s[hh]))
            pieces.append(sc[:, cc * Q_TILE:nk] + (bt_ref[hh, 0] - t31s[hh]))
            srow = jnp.concatenate(pieces, axis=1) if cc else pieces[0]
            mx = jnp.max(srow, axis=-1, keepdims=True)
            p = jnp.exp(srow - mx)
            sums.append(jnp.sum(p, axis=-1, keepdims=True))
            probs.append(p.astype(BF16))
        for hh in range(grp):
            y_ref[0, :, hh * dh:(hh + 1) * dh] = (_dot_nn(probs[hh], v_ref[0, 0:nk, :]) / sums[hh]
                                                  ).astype(y_ref.dtype)

    for cc in range(s // Q_TILE):
        pl.when(c == cc)(functools.partial(tile, cc))


def dsa_prompt(z, knb, vb, madd, table, bt, q_gain, nq, nkv):
    b, s, _ = z.shape
    grp = nq // nkv
    return pl.pallas_call(
        functools.partial(_dsa_prompt_kernel, s=s),
        grid=(b, s // Q_TILE, nkv),
        in_specs=[pl.BlockSpec(memory_space=pltpu.SMEM),
                  pl.BlockSpec((1, Q_TILE, grp * LANES), lambda i, c, g: (i, c, g)),
                  pl.BlockSpec((1, s, LANES), lambda i, c, g: (i, 0, g)),
                  pl.BlockSpec((1, s, LANES), lambda i, c, g: (i, 0, g)),
                  pl.BlockSpec((1, Q_TILE, s), lambda i, c, g: (i, c, 0)),
                  pl.BlockSpec((grp, 2, Q_TILE, Q_TILE), lambda i, c, g: (g, 0, 0, 0)),
                  pl.BlockSpec((1, LANES), lambda i, c, g: (0, 0))],
        out_specs=pl.BlockSpec((1, Q_TILE, grp * LANES), lambda i, c, g: (i, c, g)),
        out_shape=jax.ShapeDtypeStruct((b, s, nq * LANES), BF16),
        compiler_params=_cp(("parallel", "parallel", "arbitrary")),
        name="dsa_prompt",
    )(table, z, knb, vb, madd, bt, q_gain.reshape(1, LANES))


def _dsa_sample_kernel(pt_ref, q_ref, kn_ref, v_ref, m_ref, bs_ref, qg_ref, *rest, npages, t, nq, nkv):
    kp_refs = rest[:npages]
    vp_refs = rest[npages:2 * npages]
    y_ref = rest[2 * npages]
    dh = DSA_HEAD_DIM
    grp = nq // nkv
    rows = grp * t
    past = npages * PAGE_SIZE
    scale = dh ** -0.5
    q = q_ref[0]
    gain = qg_ref[...]
    keep = jnp.concatenate([m_ref[0]] * grp, axis=0) > 0.5
    pad_rows = lambda x: jnp.concatenate([x, jnp.zeros((LANES - t, dh), x.dtype)], axis=0)

    scores = []
    for g in range(nkv):
        qs = []
        for h in range(g * grp, (g + 1) * grp):
            qh = q[:, h * dh:(h + 1) * dh]
            qs.append(qh * lax.rsqrt(jnp.mean(qh * qh, axis=-1, keepdims=True) + EPS) * gain)
        qg = (jnp.concatenate(qs, axis=0) * scale).astype(BF16)
        pieces = [_dot_nt(qg, kp_refs[p][pl.ds(g, PAGE_SIZE, stride=nkv), :].astype(BF16))
                  for p in range(npages)]
        pieces.append(_dot_nt(qg, pad_rows(kn_ref[0, :, g * dh:(g + 1) * dh]).astype(BF16)))
        scores.append(jnp.concatenate(pieces, axis=1))
    probs = []
    for g in range(nkv):
        srow = jnp.where(keep, scores[g] + bs_ref[g * rows:(g + 1) * rows, :], NEG_INF)
        mx = jnp.max(srow, axis=-1, keepdims=True)
        pr = jnp.exp(srow - mx)
        probs.append((pr.astype(BF16), jnp.sum(pr, axis=-1, keepdims=True)))
    outs = []
    for g in range(nkv):
        pb, l = probs[g]
        acc = _dot_nn(pb[:, past:], pad_rows(v_ref[0, :, g * dh:(g + 1) * dh]).astype(BF16))
        for p in range(npages):
            acc = acc + _dot_nn(pb[:, p * PAGE_SIZE:(p + 1) * PAGE_SIZE],
                                vp_refs[p][pl.ds(g, PAGE_SIZE, stride=nkv), :].astype(BF16))
        acc = acc / l
        outs += [acc[hh * t:(hh + 1) * t, :] for hh in range(grp)]
    y_ref[0] = jnp.concatenate(outs, axis=1).astype(y_ref.dtype)


def dsa_sample(z, kn, mask, pool_k, pool_v, page_table, bs, q_gain, nq, nkv, v_blk):
    db, t, _ = z.shape
    npages = page_table.shape[1]
    wkv = nkv * DSA_HEAD_DIM
    lpad = npages * PAGE_SIZE + LANES
    page = lambda p: pl.BlockSpec((PAGE_SIZE * nkv, DSA_HEAD_DIM), lambda i, pt: (pt[i, p], 0))
    grid_spec = pltpu.PrefetchScalarGridSpec(
        num_scalar_prefetch=1,
        grid=(db,),
        in_specs=[pl.BlockSpec((1, t, nq * DSA_HEAD_DIM), lambda i, pt: (i, 0, 0)),
                  pl.BlockSpec((1, t, wkv), lambda i, pt: (i, 0, 0)),
                  pl.BlockSpec((1, t, wkv), lambda i, pt: (i, 0, v_blk)),
                  pl.BlockSpec((1, t, lpad), lambda i, pt: (i, 0, 0)),
                  pl.BlockSpec((nq * t, lpad), lambda i, pt: (0, 0)),
                  pl.BlockSpec((1, LANES), lambda i, pt: (0, 0))]
                 + [page(p) for p in range(npages)] * 2,
        out_specs=pl.BlockSpec((1, t, nq * DSA_HEAD_DIM), lambda i, pt: (i, 0, 0)),
    )
    return pl.pallas_call(
        functools.partial(_dsa_sample_kernel, npages=npages, t=t, nq=nq, nkv=nkv),
        grid_spec=grid_spec,
        out_shape=jax.ShapeDtypeStruct((db, t, nq * DSA_HEAD_DIM), BF16),
        compiler_params=_cp(("arbitrary",)),
        name="dsa_sample",
    )(page_table, z, kn, z, mask, bs.reshape(nq * t, lpad), q_gain.reshape(1, LANES),
      *([pool_k] * npages), *([pool_v] * npages))


def _conv_ffn(x2, seq_len, state, norm_g, weights, conv_w, conv_b):
    emit = len(weights) == 2
    if emit:
        w_val = w_gate = weights[0]
        gate_col = weights[0].shape[1] // 2
    else:
        w_val, w_gate = weights[:2]
        gate_col = 0
    mid, ns_val, ns_gate, *wb = ffn_up(x2, seq_len, state, norm_g, w_val, w_gate, gate_col, conv_w, conv_b, emit)
    y = matmul_residual((mid,), weights[-1], x2, emit)
    if emit:
        y, w_down_b = y
        wb.append(w_down_b)
    step = ns_val.shape[0] * seq_len // x2.shape[0]
    return y, jnp.concatenate([ns_val[step - 1::step], ns_gate[step - 1::step]], axis=-1), tuple(wb)


def kernel(x_prompt, x_sample, state_rglru_h, state_rglru_conv, cache_moba_k, cache_moba_v, state_ffn0_conv, cache_dsa_k, cache_dsa_v, cache_dsa_idx_k, state_ffn1_conv, page_table, rel_bias_table, l0_norm_g, l0_w_in, l0_conv_w, l0_conv_b, l0_gate_a_w, l0_gate_a_b, l0_gate_i_w, l0_gate_i_b, l0_lambda, l0_q_norm_g, l0_k_norm_g, l0_w_out, ffn0_norm_g, ffn0_w_up, ffn0_conv_w, ffn0_conv_b, ffn0_w_down, l1_norm_g, l1_w_in, l1_q_norm_g, l1_k_norm_g, l1_w_out, ffn1_norm_g, ffn1_w_up, ffn1_conv_w, ffn1_conv_b, ffn1_w_down):
    bp, s, d = x_prompt.shape
    db, t, _ = x_sample.shape
    w = state_rglru_h.shape[-1]
    f2 = ffn0_w_up.shape[1]
    n_phys = cache_moba_k.shape[0]
    npages = page_table.shape[1]
    past = npages * PAGE_SIZE
    lpad = past + LANES
    nmh = cache_moba_k.shape[2]
    nkv = cache_dsa_k.shape[2]
    nq = l1_w_out.shape[0] // DSA_HEAD_DIM
    assert cache_moba_k.shape[3] == MOBA_HEAD_DIM and cache_dsa_k.shape[3] == DSA_HEAD_DIM
    assert cache_moba_k.shape[1] == PAGE_SIZE and nmh * MOBA_HEAD_DIM == w
    assert rel_bias_table.shape == (N_BUCKETS, nmh) and nq == nmh
    assert l0_w_in.shape[1] == 5 * w

    xp = x_prompt.reshape(bp * s, d)
    xs = x_sample.reshape(db * t, d)
    bt, bsamp = bias_tiles(rel_bias_table, past, t, lpad)

    wa_bd = _regroup_blockdiag(l0_gate_a_w)
    wi_bd = _regroup_blockdiag(l0_gate_i_w)
    pool_mk = cache_moba_k.transpose(0, 2, 3, 1).reshape(n_phys, w, PAGE_SIZE)
    pool_mv = cache_moba_v.transpose(0, 2, 3, 1).reshape(n_phys, w, PAGE_SIZE)

    def even(x2, b, tt, h0, conv_state, attend, w_in, w_out, emit):
        z = norm_matmul(x2, l0_norm_g, w_in, emit)
        z, w_in_b = z if emit else (z, None)
        z = z.reshape(b, tt, 5 * w)
        ya, h_last = rglru_branch(z, h0, conv_state, l0_conv_w, l0_conv_b, wa_bd, l0_gate_a_b,
                                  wi_bd, l0_gate_i_b, l0_lambda)
        yb, kn = attend(z)
        out = matmul_residual((ya.reshape(b * tt, w), yb.reshape(b * tt, w)), w_out, x2, emit)
        out, w_out_b = out if emit else (out, None)
        new_conv = z[:, tt - (l0_conv_w.shape[0] - 1):, :w]
        v_rows = z[:, :, 4 * w:].reshape(b, tt, nmh, MOBA_HEAD_DIM)
        return (out, h_last, new_conv, kn.reshape(b, tt, nmh, MOBA_HEAD_DIM), v_rows), (w_in_b, w_out_b)

    (xs, h_s, conv_s, mk_s, mv_s), (w_in0_b, w_out0_b) = even(
        xs, db, t, state_rglru_h, state_rglru_conv,
        lambda z: moba_sample(z, pool_mk, pool_mv, page_table, bsamp, l0_q_norm_g, l0_k_norm_g, w),
        l0_w_in, l0_w_out, True)
    (xp, h_p, conv_p, mk_p, mv_p), _ = even(
        xp, bp, s, jnp.zeros((bp, w), F32), jnp.zeros((bp, l0_conv_w.shape[0] - 1, w), F32),
        lambda z: moba_prompt(z, rel_bias_table, bt, l0_q_norm_g, l0_k_norm_g, w),
        w_in0_b, w_out0_b, False)

    xs, f0_s, ffn0_wb = _conv_ffn(xs, t, state_ffn0_conv, ffn0_norm_g, (ffn0_w_up, ffn0_w_down),
                                  ffn0_conv_w, ffn0_conv_b)
    xp, f0_p, _ = _conv_ffn(xp, s, jnp.zeros((bp, ffn0_conv_w.shape[0] - 1, f2), F32), ffn0_norm_g, ffn0_wb,
                            ffn0_conv_w, ffn0_conv_b)

    wq = nq * DSA_HEAD_DIM
    wkv = nkv * DSA_HEAD_DIM
    wqi = IDX_HEADS * IDX_DIM
    n_in = l1_w_in.shape[1]
    assert n_in == wq + 2 * wkv + wqi + IDX_DIM + IDX_HEADS
    n_pad = -(-n_in // 512) * 512
    w_in1 = jnp.pad(l1_w_in, ((0, 0), (0, n_pad - n_in)))
    k_blk0 = wq // LANES
    v_blk0 = (wq + wkv) // LANES
    qi_off = wq + 2 * wkv
    assert qi_off % wqi == 0 and (qi_off + wqi) % LANES == 0 and (wq + wkv) % wkv == 0
    qi_blk = qi_off // wqi
    kw_blk = (qi_off + wqi) // LANES
    pool_dk = cache_dsa_k.reshape(n_phys * PAGE_SIZE * nkv, DSA_HEAD_DIM)
    pool_dv = cache_dsa_v.reshape(n_phys * PAGE_SIZE * nkv, DSA_HEAD_DIM)
    pool_ik = cache_dsa_idx_k.transpose(0, 2, 1)

    def odd(x2, b, tt, attend, w_in, w_out, emit):
        z2 = norm_matmul(x2, l1_norm_g, w_in, emit)
        z2, w_in_b = z2 if emit else (z2, None)
        kn, knb, vb = kv_prep(z2, l1_k_norm_g, k_blk0, v_blk0, nkv)
        z = z2.reshape(b, tt, n_pad)
        y = attend(z, kn.reshape(b, tt, wkv), knb.reshape(b, tt, wkv), vb.reshape(b, tt, wkv))
        out = matmul_residual((y.reshape(b * tt, wq),), w_out, x2, emit)
        out, w_out_b = out if emit else (out, None)
        v_rows = z[:, :, wq + wkv:wq + 2 * wkv].reshape(b, tt, nkv, DSA_HEAD_DIM)
        ki_rows = z[:, :, qi_off + wqi:qi_off + wqi + IDX_DIM]
        return (out, kn.reshape(b, tt, nkv, DSA_HEAD_DIM), v_rows, ki_rows), (w_in_b, w_out_b)

    def attend_prompt(z, kn, knb, vb):
        isc = index_prompt(z, qi_blk, kw_blk)
        mask = topk_mask(isc.reshape(bp * s, s), min(DSA_TOPK, s // 4), s, 0, True).reshape(bp, s, s)
        return dsa_prompt(z, knb, vb, mask, rel_bias_table, bt, l1_q_norm_g, nq, nkv)

    def attend_sample(z, kn, knb, vb):
        isc = index_sample(z, pool_ik, page_table, qi_blk, kw_blk)
        mask = topk_mask(isc.reshape(db * t, lpad), min(DSA_TOPK, (past + t) // 4), t, past, False)
        return dsa_sample(z, kn, mask.reshape(db, t, lpad), pool_dk, pool_dv, page_table, bsamp,
                          l1_q_norm_g, nq, nkv, (wq + wkv) // wkv)

    (xs, dk_s, dv_s, di_s), (w_in1_b, w_out1_b) = odd(xs, db, t, attend_sample, w_in1, l1_w_out, True)
    (xp, dk_p, dv_p, di_p), _ = odd(xp, bp, s, attend_prompt, w_in1_b, w_out1_b, False)

    xs, f1_s, ffn1_wb = _conv_ffn(xs, t, state_ffn1_conv, ffn1_norm_g, (ffn1_w_up, ffn1_w_down),
                                  ffn1_conv_w, ffn1_conv_b)
    xp, f1_p, _ = _conv_ffn(xp, s, jnp.zeros((bp, ffn1_conv_w.shape[0] - 1, f2), F32), ffn1_norm_g, ffn1_wb,
                            ffn1_conv_w, ffn1_conv_b)

    return (xp.reshape(bp, s, d), xs.reshape(db, t, d), h_p, h_s, conv_p, conv_s, mk_p, mk_s, mv_p, mv_s,
            f0_p, f0_s, dk_p, dk_s, dv_p, dv_s, di_p, di_s, f1_p, f1_s)
```

```python
import functools
import math

import numpy as np
import jax
import jax.numpy as jnp
from jax import lax
from jax.experimental import pallas as pl
from jax.experimental.pallas import tpu as pltpu

F32 = jnp.float32
BF16 = jnp.bfloat16
I32 = jnp.int32
NEG_INF = float("-inf")
INT_MIN = -(2 ** 31)

EPS = 1e-6
LRU_C = 8.0
LRU_GROUP = 256
MOBA_BLOCK = 256
MOBA_TOPK = 3
MOBA_HEAD_DIM = 64
DSA_HEAD_DIM = 128
DSA_TOPK = 256
IDX_DIM = 64
IDX_HEADS = 8
PAGE_SIZE = 128
N_BUCKETS = 32
MAX_DISTANCE = 128
Q_TILE = 256
MOBA_ROUND_TILES = 8
DSA_STEP_GROUPS = 1
LANES = 128
MIB = 1024 * 1024


def _t5_thresholds():
    exact = N_BUCKETS // 2
    n = np.arange(0, 4 * MAX_DISTANCE)
    nf = np.maximum(n, 1).astype(np.float32)
    large = exact + (np.log(nf / np.float32(exact)) / np.float32(math.log(MAX_DISTANCE / exact))
                     * np.float32(N_BUCKETS - exact)).astype(np.int32)
    bucket = np.where(n < exact, n, np.minimum(large, N_BUCKETS - 1))
    assert np.all(np.diff(bucket) >= 0) and bucket[-1] == N_BUCKETS - 1
    return [int(np.argmax(bucket >= b)) for b in range(N_BUCKETS)]


T5_THR = _t5_thresholds()


def _cp(sem, vmem_mib=48):
    return pltpu.CompilerParams(dimension_semantics=sem, vmem_limit_bytes=vmem_mib * MIB)


def _iota(shape, dim):
    return lax.broadcasted_iota(I32, shape, dim)


def _dot_nt(a, b):
    return lax.dot_general(a, b, (((1,), (1,)), ((), ())), preferred_element_type=F32)


def _dot_nn(a, b):
    return jnp.dot(a, b, preferred_element_type=F32)


def _split2(x):
    hi = x.astype(BF16)
    lo = (x - hi.astype(F32)).astype(BF16)
    return hi, lo


def _split3(x):
    hi = x.astype(BF16)
    r = x - hi.astype(F32)
    mid = r.astype(BF16)
    lo = (r - mid.astype(F32)).astype(BF16)
    return hi, mid, lo


def _dot3(dot, a, b):
    ah, al = _split2(a)
    bh, bl = _split2(b)
    return dot(ah, bh) + (dot(ah, bl) + dot(al, bh))


def _dot_exact01(dot, a, b01):
    h, m, l = _split3(a)
    return dot(h, b01) + (dot(m, b01) + dot(l, b01))


def _t5_bias(dist, tab_ref, h):
    acc = jnp.full(dist.shape, tab_ref[0, h], F32)
    for b in range(1, N_BUCKETS):
        acc = jnp.where(dist >= T5_THR[b], tab_ref[b, h], acc)
    return acc


def _softplus(x):
    return jnp.maximum(x, 0.0) + jnp.log1p(jnp.exp(-jnp.abs(x)))


def _norm_mm_kernel(x_ref, g_ref, w_ref, o_ref, *rest, emit):
    xn_ref = rest[-1]

    @pl.when(pl.program_id(1) == 0)
    def _():
        x = x_ref[...]
        ms = jnp.mean(x * x, axis=-1, keepdims=True)
        xn_ref[...] = (x * lax.rsqrt(ms + EPS) * g_ref[...]).astype(BF16)

    wb = w_ref[...].astype(BF16)
    if emit:
        rest[0][...] = wb
    o_ref[...] = _dot_nn(xn_ref[...], wb)


def norm_matmul(x, g, w, emit=False):
    m, d = x.shape
    n = w.shape[1]
    tm = min(m, 1024)
    tn = 1024 if (not emit and n % 1024 == 0) else 512
    assert m % tm == 0 and n % tn == 0 and (not emit or m == tm)
    out_specs = [pl.BlockSpec((tm, tn), lambda i, j: (i, j))]
    out_shape = [jax.ShapeDtypeStruct((m, n), F32)]
    if emit:
        out_specs.append(pl.BlockSpec((d, tn), lambda i, j: (0, j)))
        out_shape.append(jax.ShapeDtypeStruct((d, n), BF16))
    res = pl.pallas_call(
        functools.partial(_norm_mm_kernel, emit=emit),
        grid=(m // tm, n // tn),
        in_specs=[pl.BlockSpec((tm, d), lambda i, j: (i, 0)),
                  pl.BlockSpec((1, d), lambda i, j: (0, 0)),
                  pl.BlockSpec((d, tn), lambda i, j: (0, j))],
        out_specs=out_specs,
        out_shape=out_shape,
        scratch_shapes=[pltpu.VMEM((tm, d), BF16)],
        compiler_params=_cp(("parallel", "arbitrary")),
        name="norm_matmul",
    )(x, g.reshape(1, d), w)
    return res if emit else res[0]


def _mm_res_kernel(*refs, emit, na):
    a_refs = refs[:na]
    w_ref, r_ref, o_ref = refs[na:na + 3]
    wb = w_ref[...].astype(BF16)
    if emit:
        refs[na + 3][...] = wb
    acc = r_ref[...]
    k0 = 0
    for a_ref in a_refs:
        k1 = k0 + a_ref.shape[1]
        acc = acc + _dot_nn(a_ref[...], wb[k0:k1, :])
        k0 = k1
    o_ref[...] = acc


def matmul_residual(a_parts, w, res, emit=False):
    m = a_parts[0].shape[0]
    kdim, n = w.shape
    tm = min(m, 1024)
    tn = 256 if emit else (1024 if kdim <= 2048 and n % 1024 == 0 else 512)
    assert m % tm == 0 and n % tn == 0 and (not emit or m == tm) and sum(a.shape[1] for a in a_parts) == kdim
    out_specs = [pl.BlockSpec((tm, tn), lambda i, j: (i, j))]
    out_shape = [jax.ShapeDtypeStruct((m, n), F32)]
    if emit:
        out_specs.append(pl.BlockSpec((kdim, tn), lambda i, j: (0, j)))
        out_shape.append(jax.ShapeDtypeStruct((kdim, n), BF16))
    out = pl.pallas_call(
        functools.partial(_mm_res_kernel, emit=emit, na=len(a_parts)),
        grid=(m // tm, n // tn),
        in_specs=[pl.BlockSpec((tm, a.shape[1]), lambda i, j: (i, 0)) for a in a_parts]
                 + [pl.BlockSpec((kdim, tn), lambda i, j: (0, j)),
                    pl.BlockSpec((tm, tn), lambda i, j: (i, j))],
        out_specs=out_specs,
        out_shape=out_shape,
        compiler_params=_cp(("parallel", "arbitrary"), vmem_mib=56),
        name="matmul_residual",
    )(*a_parts, w, res)
    return out if emit else out[0]


def _ffn_up_kernel(x_ref, xp_ref, g_ref, wv_ref, wg_ref, sv_ref, sg_ref, cwv_ref, cwg_ref, cbv_ref, cbg_ref,
                   o_ref, nsv_ref, nsg_ref, *rest, nb, tc, chunks, emit):
    xn_ref, xnp_ref = rest[-2:]
    i = pl.program_id(0)
    tm, tn = o_ref.shape

    def norm(x):
        return x * lax.rsqrt(jnp.mean(x * x, axis=-1, keepdims=True) + EPS) * g_ref[...]

    @pl.when(pl.program_id(1) == 0)
    def _():
        xn_ref[...] = norm(x_ref[...]).astype(BF16)
        if chunks > 1:
            xnp_ref[...] = norm(xp_ref[...])

    def project(w_ref, wb_ref):
        wb = w_ref[...].astype(BF16)
        if emit:
            wb_ref[...] = wb
        u = _dot_nn(xn_ref[...], wb)
        up = _dot_nn(xnp_ref[...].astype(BF16), wb) if (nb == 1 and chunks > 1) else None
        return u, up

    def finish(u, up, s_ref, cw_ref, cb_ref, ns_ref):
        cw = cw_ref[...]
        if nb > 1:
            step = _iota((tm, tn), 0) % tc
            s0 = jnp.broadcast_to(s_ref[:, 0:1, :], (nb, tc, tn)).reshape(tm, tn)
            s1 = jnp.broadcast_to(s_ref[:, 1:2, :], (nb, tc, tn)).reshape(tm, tn)
            u1 = jnp.where(step == 0, s1, pltpu.roll(u, 1, 0))
            u2 = jnp.where(step == 0, s0, jnp.where(step == 1, s1, pltpu.roll(u, 2, 0)))
            ns_ref[...] = u.reshape(nb, tc, tn)[:, tc - 2:tc, :]
            return cb_ref[...] + u2 * cw[0:1] + u1 * cw[1:2] + u * cw[2:3]
        ns_ref[0] = u[tm - 2:tm, :]
        prev = s_ref[0]
        if chunks > 1:
            prev = jnp.where(i % chunks == 0, prev, up[6:8, :])
        u1 = pltpu.roll(u, 1, 0)
        u2 = pltpu.roll(u, 2, 0)
        c = cb_ref[...] + u2 * cw[0:1] + u1 * cw[1:2] + u * cw[2:3]
        row = _iota((8, tn), 0)
        t1 = jnp.where(row == 0, prev[1:2, :], u1[0:8, :])
        t2 = jnp.where(row == 0, prev[0:1, :], jnp.where(row == 1, prev[1:2, :], u2[0:8, :]))
        top = cb_ref[...] + t2 * cw[0:1] + t1 * cw[1:2] + u[0:8, :] * cw[2:3]
        return jnp.concatenate([top, c[8:, :]], axis=0)

    wbv_ref, wbg_ref = (rest[0], rest[1]) if emit else (None, None)
    uv = project(wv_ref, wbv_ref)
    ug = project(wg_ref, wbg_ref)
    val = finish(*uv, sv_ref, cwv_ref, cbv_ref, nsv_ref)
    gate = finish(*ug, sg_ref, cwg_ref, cbg_ref, nsg_ref)
    o_ref[...] = (jax.nn.gelu(gate) * val).astype(o_ref.dtype)


def ffn_up(x, seq_len, state, norm_g, w_val, w_gate, gate_col, conv_w, conv_b, emit=False):
    m, d = x.shape
    f2 = conv_w.shape[1]
    f = f2 // 2
    tn = 256 if emit else 512
    tm = min(m, 1024)
    tc = min(seq_len, tm)
    nb = tm // tc
    chunks = seq_len // tc
    assert m % tm == 0 and tm % tc == 0 and seq_len % tc == 0 and tc % 8 == 0 and (nb == 1 or chunks == 1)
    assert f % tn == 0 and gate_col % tn == 0 and conv_w.shape[0] == 3 and (not emit or m == tm)
    ncb = f // tn
    goff = gate_col // tn
    r8 = tm // 8
    nseq = m // seq_len

    def wspec(rows, off):
        return pl.BlockSpec((rows, tn), lambda i, j: (0, j + off))

    def st(off):
        return pl.BlockSpec((nb, 2, tn), lambda i, j: (i // chunks, 0, j + off))

    tail = pl.BlockSpec((nb, 2, tn), lambda i, j: (i, 0, j))
    out_specs = [pl.BlockSpec((tm, tn), lambda i, j: (i, j)), tail, tail]
    out_shape = [jax.ShapeDtypeStruct((m, f), BF16), jax.ShapeDtypeStruct((nseq * chunks, 2, f), F32),
                 jax.ShapeDtypeStruct((nseq * chunks, 2, f), F32)]
    if emit:
        out_specs += [wspec(d, 0), wspec(d, 0)]
        out_shape += [jax.ShapeDtypeStruct((d, f), BF16), jax.ShapeDtypeStruct((d, f), BF16)]
    return pl.pallas_call(
        functools.partial(_ffn_up_kernel, nb=nb, tc=tc, chunks=chunks, emit=emit),
        grid=(m // tm, ncb),
        in_specs=[pl.BlockSpec((tm, d), lambda i, j: (i, 0)),
                  pl.BlockSpec((8, d), lambda i, j: (jnp.maximum(i * r8 - 1, 0), 0)),
                  pl.BlockSpec((1, d), lambda i, j: (0, 0)),
                  wspec(d, 0), wspec(d, goff), st(0), st(ncb),
                  wspec(3, 0), wspec(3, ncb), wspec(1, 0), wspec(1, ncb)],
        out_specs=out_specs,
        out_shape=out_shape,
        scratch_shapes=[pltpu.VMEM((tm, d), BF16), pltpu.VMEM((8, d), F32)],
        compiler_params=_cp(("parallel", "arbitrary"), vmem_mib=56),
        name="ffn_up",
    )(x, x, norm_g.reshape(1, d), w_val, w_gate, state, state, conv_w, conv_w,
      conv_b.reshape(1, f2), conv_b.reshape(1, f2))


def _rglru_kernel(xa_ref, ga_ref, pv_ref, cs_ref, h0_ref, cw_ref, cb_ref, wa_ref, ba_ref,
                  wi_ref, bi_ref, lam_ref, ya_ref, hl_ref, e_ref, hc_ref, *, tc, bb):
    t = pl.program_id(1)
    w = xa_ref.shape[-1]
    rows = bb * tc

    @pl.when(t == 0)
    def _():
        hc_ref[...] = h0_ref[...]
        e_ref[:, 5:8, :] = cs_ref[...]

    @pl.when(t > 0)
    def _():
        e_ref[:, 5:8, :] = pv_ref[:, 5:8, :]

    e_ref[:, 8:8 + tc, :] = xa_ref[...]
    cw = cw_ref[...]
    xc = (cb_ref[...] + e_ref[:, 5:5 + tc, :] * cw[0:1] + e_ref[:, 6:6 + tc, :] * cw[1:2]
          + e_ref[:, 7:7 + tc, :] * cw[2:3] + e_ref[:, 8:8 + tc, :] * cw[3:4]).reshape(rows, w)
    xb = xc.astype(BF16)

    def gate_logits(w_ref):
        return jnp.concatenate([_dot_nn(xb[:, g * LRU_GROUP:(g + 1) * LRU_GROUP], w_ref[g])
                                for g in range(w // LRU_GROUP)], axis=1)

    logits_a = gate_logits(wa_ref)
    logits_i = gate_logits(wi_ref)
    r = jax.nn.sigmoid(logits_a + ba_ref[...])
    i = jax.nn.sigmoid(logits_i + bi_ref[...])
    log_a = -LRU_C * r * _softplus(-lam_ref[...])
    a = jnp.exp(log_a)
    u = jnp.sqrt(-jnp.tanh(log_a) * (1.0 + a * a)) * (i * xc)

    step = _iota((rows, w), 0) % tc
    s = 1
    while s < tc:
        a_sh = jnp.where(step >= s, pltpu.roll(a, s, 0), 1.0)
        u_sh = jnp.where(step >= s, pltpu.roll(u, s, 0), 0.0)
        u = u + a * u_sh
        a = a * a_sh
        s *= 2
    h = u.reshape(bb, tc, w) + a.reshape(bb, tc, w) * hc_ref[...]
    hlast = h[:, tc - 1:tc, :]
    hc_ref[...] = hlast
    hl_ref[...] = hlast
    ya_ref[...] = (h * jax.nn.gelu(ga_ref[...])).astype(ya_ref.dtype)


def rglru_branch(z, h0, conv_state, conv_w, conv_b, wa_bd, ba, wi_bd, bi, lam):
    b, t, _ = z.shape
    w = h0.shape[-1]
    tc = min(t, 256)
    bb = 8 if (tc == t and tc == 8 and b % 8 == 0) else 1
    assert t % tc == 0 and tc % 8 == 0 and conv_w.shape[0] == 4 and w % LRU_GROUP == 0
    r8 = tc // 8
    ng = w // LRU_GROUP
    vec = lambda: pl.BlockSpec((1, w), lambda i, j: (0, 0))
    ya, hl = pl.pallas_call(
        functools.partial(_rglru_kernel, tc=tc, bb=bb),
        grid=(b // bb, t // tc),
        in_specs=[pl.BlockSpec((bb, tc, w), lambda i, j: (i, j, 0)),
                  pl.BlockSpec((bb, tc, w), lambda i, j: (i, j, 1)),
                  pl.BlockSpec((bb, 8, w), lambda i, j: (i, jnp.maximum(j * r8 - 1, 0), 0)),
                  pl.BlockSpec((bb, 3, w), lambda i, j: (i, 0, 0)),
                  pl.BlockSpec((bb, 1, w), lambda i, j: (i, 0, 0)),
                  pl.BlockSpec((4, w), lambda i, j: (0, 0)),
                  vec(),
                  pl.BlockSpec((ng, LRU_GROUP, LRU_GROUP), lambda i, j: (0, 0, 0)),
                  vec(),
                  pl.BlockSpec((ng, LRU_GROUP, LRU_GROUP), lambda i, j: (0, 0, 0)),
                  vec(), vec()],
        out_specs=[pl.BlockSpec((bb, tc, w), lambda i, j: (i, j, 0)),
                   pl.BlockSpec((bb, 1, w), lambda i, j: (i, 0, 0))],
        out_shape=[jax.ShapeDtypeStruct((b, t, w), BF16), jax.ShapeDtypeStruct((b, 1, w), F32)],
        scratch_shapes=[pltpu.VMEM((bb, tc + 8, w), F32), pltpu.VMEM((bb, 1, w), F32)],
        compiler_params=_cp(("parallel", "arbitrary")),
        name="rglru",
    )(z, z, z, conv_state, h0.reshape(b, 1, w), conv_w, conv_b.reshape(1, w), wa_bd,
      ba.reshape(1, w), wi_bd, bi.reshape(1, w), lam.reshape(1, w))
    return ya, hl.reshape(b, w)


def _regroup_blockdiag(wb):
    n, c, _ = wb.shape
    per = LRU_GROUP // c
    g = n // per
    eye = jnp.eye(per, dtype=wb.dtype)
    out = jnp.einsum("gacd,ab->gacbd", wb.reshape(g, per, c, c), eye)
    return out.reshape(g, LRU_GROUP, LRU_GROUP).astype(BF16)


def _bias_kernel(tab_ref, bt_ref, bs_ref, *, past, lpad):
    h = pl.program_id(0)
    d0 = _iota((Q_TILE, Q_TILE), 0) - _iota((Q_TILE, Q_TILE), 1)
    bt_ref[0, 0] = _t5_bias(d0, tab_ref, h)
    bt_ref[0, 1] = _t5_bias(d0 + Q_TILE, tab_ref, h)
    t = bs_ref.shape[1]
    ds = past + _iota((t, lpad), 0) - _iota((t, lpad), 1)
    bs_ref[0] = _t5_bias(ds, tab_ref, h)


def bias_tiles(table, past, t_new, lpad):
    nh = table.shape[1]
    return pl.pallas_call(
        functools.partial(_bias_kernel, past=past, lpad=lpad),
        grid=(nh,),
        in_specs=[pl.BlockSpec(memory_space=pltpu.SMEM)],
        out_specs=[pl.BlockSpec((1, 2, Q_TILE, Q_TILE), lambda h: (h, 0, 0, 0)),
                   pl.BlockSpec((1, t_new, lpad), lambda h: (h, 0, 0))],
        out_shape=[jax.ShapeDtypeStruct((nh, 2, Q_TILE, Q_TILE), F32),
                   jax.ShapeDtypeStruct((nh, t_new, lpad), F32)],
        compiler_params=_cp(("arbitrary",)),
        name="t5_bias_tiles",
    )(table)


def _top_lanes(g, nvalid, k):
    lane = _iota(g.shape, 1).astype(F32)
    gg = jnp.where(lane < nvalid, g, NEG_INF)
    sel = jnp.zeros(g.shape, F32)
    for _ in range(k):
        mx = jnp.max(gg, axis=-1, keepdims=True)
        first = jnp.min(jnp.where(gg == mx, lane, float(g.shape[1])), axis=-1, keepdims=True)
        pick = (lane == first) & (mx > NEG_INF)
        sel = jnp.where(pick, 1.0, sel)
        gg = jnp.where(pick, NEG_INF, gg)
    return sel


def _moba_prompt_kernel(tab_ref, q_ref, k_ref, v_ref, qg_ref, kg_ref, bt_ref, y_ref, kn_ref, *, s):
    hp = pl.program_id(0)
    nb = s // MOBA_BLOCK
    dh = MOBA_HEAD_DIM
    scale = dh ** -0.5
    lane = _iota((1, LANES), 1)
    low = lane < dh

    def headnorm(x, g):
        x2 = x * x
        s0 = jnp.sum(jnp.where(low, x2, 0.0), axis=-1, keepdims=True)
        s1 = jnp.sum(jnp.where(low, 0.0, x2), axis=-1, keepdims=True)
        ms = jnp.where(low, s0, s1) * (1.0 / dh)
        return x * lax.rsqrt(ms + EPS) * g

    qn = headnorm(q_ref[0], qg_ref[...])
    kn = headnorm(k_ref[0], kg_ref[...])
    kn_ref[0] = kn
    kb = kn.astype(BF16)
    vb = v_ref[0].astype(BF16)
    blk_row = _iota((LANES, LANES), 0)
    kmat = jnp.zeros((LANES, LANES), F32)
    for j in range(nb):
        kmean = jnp.mean(kn[j * MOBA_BLOCK:(j + 1) * MOBA_BLOCK], axis=0, keepdims=True)
        kmat = jnp.where(blk_row == j, kmean, kmat)
    qi = _iota((Q_TILE, Q_TILE), 0)
    kj = _iota((Q_TILE, Q_TILE), 1)
    causal = qi >= kj

    near = []
    for hh in range(2):
        t31 = tab_ref[N_BUCKETS - 1, 2 * hp + hh]
        near.append((jnp.where(causal, bt_ref[hh, 0] - t31, NEG_INF), bt_ref[hh, 1] - t31))

    for c0 in range(0, nb, MOBA_ROUND_TILES):
        work = [(c, hh) for c in range(c0, min(c0 + MOBA_ROUND_TILES, nb)) for hh in range(2)]
        scs, sels = {}, {}
        for c, hh in work:
            hmask = low if hh == 0 else jnp.logical_not(low)
            qm = jnp.where(hmask, qn[c * Q_TILE:(c + 1) * Q_TILE], 0.0)
            scs[c, hh] = _dot_nt((qm * scale).astype(BF16), kb[0:(c + 1) * MOBA_BLOCK])
            if c > MOBA_TOPK:
                sels[c, hh] = _top_lanes(_dot3(_dot_nt, qm, kmat), c, MOBA_TOPK)
        probs, sums = {}, {}
        for c, hh in work:
            pieces = []
            for j in range(c + 1):
                sj = scs[c, hh][:, j * MOBA_BLOCK:(j + 1) * MOBA_BLOCK]
                if j == c:
                    sj = sj + near[hh][0]
                else:
                    if j == c - 1:
                        sj = sj + near[hh][1]
                    if c > MOBA_TOPK:
                        sj = jnp.where(sels[c, hh][:, j:j + 1] > 0.0, sj, NEG_INF)
                pieces.append(sj)
            srow = jnp.concatenate(pieces, axis=1) if c else pieces[0]
            mx = jnp.max(srow, axis=-1, keepdims=True)
            p = jnp.exp(srow - mx)
            sums[c, hh] = jnp.sum(p, axis=-1, keepdims=True)
            probs[c, hh] = p.astype(BF16)
        for c in sorted({c for c, _ in work}):
            outs = [_dot_nn(probs[c, hh], vb[0:(c + 1) * MOBA_BLOCK]) / sums[c, hh] for hh in range(2)]
            y_ref[0, c * Q_TILE:(c + 1) * Q_TILE, :] = jnp.where(low, outs[0], outs[1]).astype(y_ref.dtype)


def moba_prompt(z, table, bt, q_gain, k_gain, w):
    b, s, _ = z.shape
    assert s % MOBA_BLOCK == 0 and Q_TILE == MOBA_BLOCK and s // MOBA_BLOCK <= LANES
    nhp = w // LANES
    g2 = lambda g: jnp.tile(g, LANES // MOBA_HEAD_DIM).reshape(1, LANES)
    col = lambda off: pl.BlockSpec((1, s, LANES), lambda hp, i: (i, 0, off * nhp + hp))
    gspec = pl.BlockSpec((1, LANES), lambda hp, i: (0, 0))
    return pl.pallas_call(
        functools.partial(_moba_prompt_kernel, s=s),
        grid=(nhp, b),
        in_specs=[pl.BlockSpec(memory_space=pltpu.SMEM), col(2), col(3), col(4), gspec, gspec,
                  pl.BlockSpec((2, 2, Q_TILE, Q_TILE), lambda hp, i: (hp, 0, 0, 0))],
        out_specs=[pl.BlockSpec((1, s, LANES), lambda hp, i: (i, 0, hp)),
                   pl.BlockSpec((1, s, LANES), lambda hp, i: (i, 0, hp))],
        out_shape=[jax.ShapeDtypeStruct((b, s, w), BF16), jax.ShapeDtypeStruct((b, s, w), F32)],
        compiler_params=_cp(("parallel", "parallel")),
        name="moba_prompt",
    )(table, z, z, z, g2(q_gain), g2(k_gain), bt)


def _group_onehot(ngroups_pad, width, gsize):
    return jnp.where(_iota((ngroups_pad, width), 0) == _iota((ngroups_pad, width), 1) // gsize,
                     1.0, 0.0).astype(BF16)


def _group_rmsnorm(x, gain, gsize):
    onehot = _group_onehot(LANES, x.shape[1], gsize)
    ssq = _dot_exact01(_dot_nt, x * x, onehot)
    rinv = lax.rsqrt(ssq * (1.0 / gsize) + EPS)
    return x * _dot_exact01(_dot_nn, rinv, onehot) * gain


def _moba_sample_kernel(pt_ref, q_ref, k_ref, v_ref, qg_ref, kg_ref, bs_ref, *rest, npages, t):
    kp_refs = rest[:npages]
    vp_refs = rest[npages:2 * npages]
    y_ref, kn_ref = rest[2 * npages:]
    w = q_ref.shape[-1]
    dh = MOBA_HEAD_DIM
    nh = w // dh
    rows = nh * t
    ppb = MOBA_BLOCK // PAGE_SIZE
    nbp = npages // ppb
    past = npages * PAGE_SIZE
    scale = dh ** -0.5

    qn = _group_rmsnorm(q_ref[0], qg_ref[...], dh)
    kn = _group_rmsnorm(k_ref[0], kg_ref[...], dh)
    kn_ref[0] = kn
    own = _iota((rows, w), 0) // t == _iota((rows, w), 1) // dh
    qbd = jnp.where(own, jnp.concatenate([qn] * nh, axis=0), 0.0)
    qbb = qbd.astype(BF16)
    pad_rows = lambda x: jnp.concatenate([x, jnp.zeros((LANES - t, w), x.dtype)], axis=0)

    pieces = [_dot_nn(qbb, kp_refs[p][0].astype(BF16)) for p in range(npages)]
    pieces.append(_dot_nt(qbb, pad_rows(kn).astype(BF16)))
    sums = [jnp.sum(sum(kp_refs[p][0] for p in range(j * ppb, (j + 1) * ppb)), axis=1, keepdims=True)
            for j in range(nbp)]
    lpad = past + LANES
    srow = jnp.concatenate(pieces, axis=1) * scale + bs_ref[...]

    blk_lane = _iota((w, LANES), 1)
    kmat = jnp.zeros((w, LANES), F32)
    for j in range(nbp):
        kmat = jnp.where(blk_lane == j, sums[j] * (1.0 / MOBA_BLOCK), kmat)
    sel = _top_lanes(_dot3(_dot_nn, qbd, kmat), nbp, MOBA_TOPK)
    col = _iota((rows, lpad), 1)
    jrow = _iota((rows, lpad), 0) % t
    keep = (col >= past) & (col - past <= jrow)
    for j in range(nbp):
        inblk = (col >= j * MOBA_BLOCK) & (col < (j + 1) * MOBA_BLOCK)
        keep = keep | (inblk & (sel[:, j:j + 1] > 0.0))
    srow = jnp.where(keep, srow, NEG_INF)
    mx = jnp.max(srow, axis=-1, keepdims=True)
    pr = jnp.exp(srow - mx)
    l = jnp.sum(pr, axis=-1, keepdims=True)
    pb = pr.astype(BF16)
    acc = _dot_nn(pb[:, past:], pad_rows(v_ref[0]).astype(BF16))
    for p in range(npages):
        acc = acc + _dot_nt(pb[:, p * PAGE_SIZE:(p + 1) * PAGE_SIZE], vp_refs[p][0].astype(BF16))
    acc = acc / l
    lane_head = _iota((t, w), 1) // dh
    y = jnp.zeros((t, w), F32)
    for h in range(nh):
        y = jnp.where(lane_head == h, acc[h * t:(h + 1) * t, :], y)
    y_ref[0] = y.astype(y_ref.dtype)


def moba_sample(z, pool_k, pool_v, page_table, bs, q_gain, k_gain, w):
    db, t, _ = z.shape
    npages = page_table.shape[1]
    nh = w // MOBA_HEAD_DIM
    assert t == 8 and nh * t == LANES and npages % (MOBA_BLOCK // PAGE_SIZE) == 0
    lpad = npages * PAGE_SIZE + LANES
    gt = lambda g: jnp.tile(g, nh).reshape(1, w)
    col = lambda off: pl.BlockSpec((1, t, w), lambda i, pt: (i, 0, off))
    gspec = pl.BlockSpec((1, w), lambda i, pt: (0, 0))
    page = lambda p: pl.BlockSpec((1, w, PAGE_SIZE), lambda i, pt: (pt[i, p], 0, 0))
    grid_spec = pltpu.PrefetchScalarGridSpec(
        num_scalar_prefetch=1,
        grid=(db,),
        in_specs=[col(2), col(3), col(4), gspec, gspec,
                  pl.BlockSpec((nh * t, lpad), lambda i, pt: (0, 0))]
                 + [page(p) for p in range(npages)] * 2,
        out_specs=[pl.BlockSpec((1, t, w), lambda i, pt: (i, 0, 0)),
                   pl.BlockSpec((1, t, w), lambda i, pt: (i, 0, 0))],
    )
    return pl.pallas_call(
        functools.partial(_moba_sample_kernel, npages=npages, t=t),
        grid_spec=grid_spec,
        out_shape=[jax.ShapeDtypeStruct((db, t, w), BF16), jax.ShapeDtypeStruct((db, t, w), F32)],
        compiler_params=_cp(("arbitrary",), vmem_mib=56),
        name="moba_sample",
    )(page_table, z, z, z, gt(q_gain), gt(k_gain), bs.reshape(nh * t, lpad),
      *([pool_k] * npages), *([pool_v] * npages))


def _kv_prep_kernel(k_ref, v_ref, g_ref, kn_ref, knb_ref, vb_ref):
    x = k_ref[...]
    ms = jnp.mean(x * x, axis=-1, keepdims=True)
    kn = x * lax.rsqrt(ms + EPS) * g_ref[...]
    kn_ref[...] = kn
    knb_ref[...] = kn.astype(BF16)
    vb_ref[...] = v_ref[...].astype(BF16)


def kv_prep(z, gain, k_blk0, v_blk0, nheads):
    m = z.shape[0]
    tm = min(m, 1024)
    assert m % tm == 0
    out = pl.BlockSpec((tm, LANES), lambda i, h: (i, h))
    return pl.pallas_call(
        _kv_prep_kernel,
        grid=(m // tm, nheads),
        in_specs=[pl.BlockSpec((tm, LANES), lambda i, h: (i, k_blk0 + h)),
                  pl.BlockSpec((tm, LANES), lambda i, h: (i, v_blk0 + h)),
                  pl.BlockSpec((1, LANES), lambda i, h: (0, 0))],
        out_specs=[out, out, out],
        out_shape=[jax.ShapeDtypeStruct((m, nheads * LANES), F32),
                   jax.ShapeDtypeStruct((m, nheads * LANES), BF16),
                   jax.ShapeDtypeStruct((m, nheads * LANES), BF16)],
        compiler_params=_cp(("parallel", "parallel")),
        name="kv_prep",
    )(z, z, gain.reshape(1, LANES))


def _index_prompt_kernel(qi_ref, qw_ref, kk_ref, o_ref, *, s):
    c = pl.program_id(1)
    lane = _iota((1, LANES), 1)
    low = lane < IDX_DIM
    qw = qw_ref[0]

    def split(x):
        hi = x.astype(BF16).astype(F32)
        return hi, (x - hi).astype(BF16).astype(F32)

    def tile(cc):
        nk = (cc + 1) * Q_TILE
        kh, kl = split(jnp.where(low, kk_ref[0, 0:nk, :], 0.0))
        rhs = jnp.concatenate([kh + pltpu.roll(kh, IDX_DIM, 1), kl], axis=1).astype(BF16)
        scs = []
        for pair in range(IDX_HEADS // 2):
            qh, ql = split(qi_ref[0, :, pair * LANES:(pair + 1) * LANES])
            qh_sw = pltpu.roll(qh, IDX_DIM, 1)
            ql_sw = pltpu.roll(ql, IDX_DIM, 1)
            for hi_low, lo_high in ((qh, ql_sw), (qh_sw, ql)):
                lhs = jnp.concatenate([jnp.where(low, hi_low, lo_high), jnp.where(low, hi_low, 0.0)],
                                      axis=1).astype(BF16)
                scs.append(_dot_nt(lhs, rhs))
        acc = jnp.zeros((Q_TILE, nk), F32)
        for ih, sc in enumerate(scs):
            wcol = qw[:, IDX_DIM + ih:IDX_DIM + ih + 1] * (IDX_HEADS ** -0.5)
            acc = acc + wcol * jnp.maximum(sc * (IDX_DIM ** -0.5), 0.0)
        qpos = cc * Q_TILE + _iota((Q_TILE, nk), 0)
        o_ref[0, :, 0:nk] = jnp.where(_iota((Q_TILE, nk), 1) <= qpos, acc, NEG_INF)
        if nk < s:
            o_ref[0, :, nk:] = jnp.full((Q_TILE, s - nk), NEG_INF, F32)

    for cc in range(s // Q_TILE):
        pl.when(c == cc)(functools.partial(tile, cc))


def index_prompt(z, qi_blk, kw_blk):
    b, s, _ = z.shape
    assert s % Q_TILE == 0
    return pl.pallas_call(
        functools.partial(_index_prompt_kernel, s=s),
        grid=(b, s // Q_TILE),
        in_specs=[pl.BlockSpec((1, Q_TILE, IDX_HEADS * IDX_DIM), lambda i, c: (i, c, qi_blk)),
                  pl.BlockSpec((1, Q_TILE, LANES), lambda i, c: (i, c, kw_blk)),
                  pl.BlockSpec((1, s, LANES), lambda i, c: (i, 0, kw_blk))],
        out_specs=pl.BlockSpec((1, Q_TILE, s), lambda i, c: (i, c, 0)),
        out_shape=jax.ShapeDtypeStruct((b, s, s), F32),
        compiler_params=_cp(("parallel", "parallel")),
        name="index_prompt",
    )(z, z, z)


def _index_sample_kernel(pt_ref, qi_ref, qw_ref, *rest, npages, t):
    kp_refs = rest[:npages]
    o_ref = rest[npages]
    qi = qi_ref[0]
    qw = qw_ref[0]
    q2 = jnp.concatenate([qi[:, ih * IDX_DIM:(ih + 1) * IDX_DIM] for ih in range(IDX_HEADS)], axis=0)
    wcol = jnp.concatenate([qw[:, IDX_DIM + ih:IDX_DIM + ih + 1] for ih in range(IDX_HEADS)], axis=0)
    wcol = wcol * (IDX_HEADS ** -0.5)
    knew = jnp.concatenate([qw[:, 0:IDX_DIM], jnp.zeros((LANES - t, IDX_DIM), F32)], axis=0)

    def scores(dot, keys):
        sc = wcol * jnp.maximum(_dot3(dot, q2, keys) * (IDX_DIM ** -0.5), 0.0)
        return sum(sc[ih * t:(ih + 1) * t, :] for ih in range(IDX_HEADS))

    pieces = [scores(_dot_nn, kp_refs[p][0]) for p in range(npages)]
    new = scores(_dot_nt, knew)
    jj = _iota((t, LANES), 0)
    cc = _iota((t, LANES), 1)
    pieces.append(jnp.where(cc <= jj, new, NEG_INF))
    o_ref[0] = jnp.concatenate(pieces, axis=1)


def index_sample(z, pool_ik, page_table, qi_blk, kw_blk):
    db, t, _ = z.shape
    npages = page_table.shape[1]
    lpad = npages * PAGE_SIZE + LANES
    grid_spec = pltpu.PrefetchScalarGridSpec(
        num_scalar_prefetch=1,
        grid=(db,),
        in_specs=[pl.BlockSpec((1, t, IDX_HEADS * IDX_DIM), lambda i, pt: (i, 0, qi_blk)),
                  pl.BlockSpec((1, t, LANES), lambda i, pt: (i, 0, kw_blk))]
                 + [pl.BlockSpec((1, IDX_DIM, PAGE_SIZE), (lambda i, pt, p=p: (pt[i, p], 0, 0)))
                    for p in range(npages)],
        out_specs=pl.BlockSpec((1, t, lpad), lambda i, pt: (i, 0, 0)),
    )
    return pl.pallas_call(
        functools.partial(_index_sample_kernel, npages=npages, t=t),
        grid_spec=grid_spec,
        out_shape=jax.ShapeDtypeStruct((db, t, lpad), F32),
        compiler_params=_cp(("arbitrary",)),
        name="index_sample",
    )(page_table, z, z, *([pool_ik] * npages))


def _topk_mask_kernel(x_ref, o_ref, *, topk, period, base, tr, additive, ncausal):
    l = x_ref.shape[1]
    on, off = (0.0, NEG_INF) if additive else (1.0, 0.0)

    def body(nk):
        rowg = pl.program_id(0) * tr + _iota((tr, nk), 0)
        valid = _iota((tr, nk), 1) <= base + rowg % period
        bits = pltpu.bitcast(x_ref[:, 0:nk], I32)
        key = jnp.where(bits < 0, bits ^ 0x7FFFFFFF, bits)
        key = jnp.where(valid, key, INT_MIN)

        def step(i, lo):
            cand = lo + lax.shift_left(jnp.int32(1), 31 - i)
            cnt = jnp.sum(jnp.where(key >= cand, 1.0, 0.0), axis=-1, keepdims=True)
            return jnp.where(cnt >= topk, cand, lo)

        thr = lax.fori_loop(0, 32, step, jnp.full((tr, 1), INT_MIN, I32))
        gt = key > thr
        need = topk - jnp.sum(jnp.where(gt, 1.0, 0.0), axis=-1, keepdims=True)
        eq = jnp.where(key == thr, 1.0, 0.0)
        upper = jnp.where(_iota((LANES, LANES), 0) < _iota((LANES, LANES), 1), 1.0, 0.0).astype(BF16)
        before = jnp.zeros((tr, 1), F32)
        for j in range(nk // LANES):
            e = eq[:, j * LANES:(j + 1) * LANES]
            rank = before + _dot_nn(e.astype(BF16), upper)
            pick = gt[:, j * LANES:(j + 1) * LANES] | ((e > 0.0) & (rank < need))
            pick = pick & valid[:, j * LANES:(j + 1) * LANES]
            o_ref[:, j * LANES:(j + 1) * LANES] = jnp.where(pick, on, off)
            before = before + jnp.sum(e, axis=-1, keepdims=True)
        if nk < l:
            o_ref[:, nk:] = jnp.full((tr, l - nk), off, F32)

    if ncausal > 1:
        c = pl.program_id(0) % ncausal
        for cc in range(ncausal):
            pl.when(c == cc)(functools.partial(body, (cc + 1) * tr))
    else:
        body(l)


def topk_mask(scores, topk, period, base, additive):
    r, l = scores.shape
    tr = min(r, 256)
    assert r % tr == 0 and l % LANES == 0
    ncausal = period // tr if (base == 0 and period == l and period % tr == 0) else 1
    return pl.pallas_call(
        functools.partial(_topk_mask_kernel, topk=topk, period=period, base=base, tr=tr, additive=additive,
                          ncausal=ncausal),
        grid=(r // tr,),
        in_specs=[pl.BlockSpec((tr, l), lambda i: (i, 0))],
        out_specs=pl.BlockSpec((tr, l), lambda i: (i, 0)),
        out_shape=jax.ShapeDtypeStruct((r, l), F32),
        compiler_params=_cp(("parallel",)),
        name="topk_mask",
    )(scores)


def _dsa_prompt_kernel(tab_ref, q_ref, k_ref, v_ref, m_ref, bt_ref, qg_ref, y_ref, *, s, grp):
    c = pl.program_id(1)
    g = pl.program_id(2)
    dh = DSA_HEAD_DIM
    nh = q_ref.shape[-1] // dh
    qbs, t31s = [], []
    for hh in range(nh):
        q = q_ref[0, :, hh * dh:(hh + 1) * dh]
        qn = q * lax.rsqrt(jnp.mean(q * q, axis=-1, keepdims=True) + EPS) * qg_ref[...]
        qbs.append((qn * (dh ** -0.5)).astype(BF16))
        t31s.append(tab_ref[N_BUCKETS - 1, g * nh + hh])

    def tile(cc):
        nk = (cc + 1) * Q_TILE
        madd = m_ref[0, :, 0:nk]
        kv = lambda ref, hh: ref[0, 0:nk, (hh // grp) * dh:(hh // grp + 1) * dh]
        scs = [_dot_nt(qbs[hh], kv(k_ref, hh)) for hh in range(nh)]
        probs, sums = [], []
        for hh in range(nh):
            sc = scs[hh] + madd
            pieces = [sc[:, 0:(cc - 1) * Q_TILE]] if cc > 1 else []
            if cc > 0:
                pieces.append(sc[:, (cc - 1) * Q_TILE:cc * Q_TILE] + (bt_ref[hh, 1] - t31s[hh]))
            pieces.append(sc[:, cc * Q_TILE:nk] + (bt_ref[hh, 0] - t31s[hh]))
            srow = jnp.concatenate(pieces, axis=1) if cc else pieces[0]
            mx = jnp.max(srow, axis=-1, keepdims=True)
            p = jnp.exp(srow - mx)
            sums.append(jnp.sum(p, axis=-1, keepdims=True))
            probs.append(p.astype(BF16))
        for hh in range(nh):
            y_ref[0, :, hh * dh:(hh + 1) * dh] = (_dot_nn(probs[hh], kv(v_ref, hh)) / sums[hh]
                                                  ).astype(y_ref.dtype)

    for cc in range(s // Q_TILE):
        pl.when(c == cc)(functools.partial(tile, cc))


def dsa_prompt(z, knb, vb, madd, table, bt, q_gain, nq, nkv):
    b, s, _ = z.shape
    grp = nq // nkv
    gps = DSA_STEP_GROUPS if nkv % DSA_STEP_GROUPS == 0 else 1
    nh = gps * grp
    return pl.pallas_call(
        functools.partial(_dsa_prompt_kernel, s=s, grp=grp),
        grid=(b, s // Q_TILE, nkv // gps),
        in_specs=[pl.BlockSpec(memory_space=pltpu.SMEM),
                  pl.BlockSpec((1, Q_TILE, nh * LANES), lambda i, c, g: (i, c, g)),
                  pl.BlockSpec((1, s, gps * LANES), lambda i, c, g: (i, 0, g)),
                  pl.BlockSpec((1, s, gps * LANES), lambda i, c, g: (i, 0, g)),
                  pl.BlockSpec((1, Q_TILE, s), lambda i, c, g: (i, c, 0)),
                  pl.BlockSpec((nh, 2, Q_TILE, Q_TILE), lambda i, c, g: (g, 0, 0, 0)),
                  pl.BlockSpec((1, LANES), lambda i, c, g: (0, 0))],
        out_specs=pl.BlockSpec((1, Q_TILE, nh * LANES), lambda i, c, g: (i, c, g)),
        out_shape=jax.ShapeDtypeStruct((b, s, nq * LANES), BF16),
        compiler_params=_cp(("parallel", "parallel", "arbitrary")),
        name="dsa_prompt",
    )(table, z, knb, vb, madd, bt, q_gain.reshape(1, LANES))


def _dsa_sample_kernel(pt_ref, q_ref, kn_ref, v_ref, m_ref, bs_ref, qg_ref, *rest, npages, t, nq, nkv):
    kp_refs = rest[:npages]
    vp_refs = rest[npages:2 * npages]
    y_ref = rest[2 * npages]
    dh = DSA_HEAD_DIM
    grp = nq // nkv
    rows = grp * t
    past = npages * PAGE_SIZE
    scale = dh ** -0.5
    q = q_ref[0]
    gain = qg_ref[...]
    keep = jnp.concatenate([m_ref[0]] * grp, axis=0) > 0.5
    pad_rows = lambda x: jnp.concatenate([x, jnp.zeros((LANES - t, dh), x.dtype)], axis=0)

    scores = []
    for g in range(nkv):
        qs = []
        for h in range(g * grp, (g + 1) * grp):
            qh = q[:, h * dh:(h + 1) * dh]
            qs.append(qh * lax.rsqrt(jnp.mean(qh * qh, axis=-1, keepdims=True) + EPS) * gain)
        qg = (jnp.concatenate(qs, axis=0) * scale).astype(BF16)
        pieces = [_dot_nt(qg, kp_refs[p][pl.ds(g, PAGE_SIZE, stride=nkv), :].astype(BF16))
                  for p in range(npages)]
        pieces.append(_dot_nt(qg, pad_rows(kn_ref[0, :, g * dh:(g + 1) * dh]).astype(BF16)))
        scores.append(jnp.concatenate(pieces, axis=1))
    probs = []
    for g in range(nkv):
        srow = jnp.where(keep, scores[g] + bs_ref[g * rows:(g + 1) * rows, :], NEG_INF)
        mx = jnp.max(srow, axis=-1, keepdims=True)
        pr = jnp.exp(srow - mx)
        probs.append((pr.astype(BF16), jnp.sum(pr, axis=-1, keepdims=True)))
    outs = []
    for g in range(nkv):
        pb, l = probs[g]
        acc = _dot_nn(pb[:, past:], pad_rows(v_ref[0, :, g * dh:(g + 1) * dh]).astype(BF16))
        for p in range(npages):
            acc = acc + _dot_nn(pb[:, p * PAGE_SIZE:(p + 1) * PAGE_SIZE],
                                vp_refs[p][pl.ds(g, PAGE_SIZE, stride=nkv), :].astype(BF16))
        acc = acc / l
        outs += [acc[hh * t:(hh + 1) * t, :] for hh in range(grp)]
    y_ref[0] = jnp.concatenate(outs, axis=1).astype(y_ref.dtype)


def dsa_sample(z, kn, mask, pool_k, pool_v, page_table, bs, q_gain, nq, nkv, v_blk):
    db, t, _ = z.shape
    npages = page_table.shape[1]
    wkv = nkv * DSA_HEAD_DIM
    lpad = npages * PAGE_SIZE + LANES
    page = lambda p: pl.BlockSpec((PAGE_SIZE * nkv, DSA_HEAD_DIM), lambda i, pt: (pt[i, p], 0))
    grid_spec = pltpu.PrefetchScalarGridSpec(
        num_scalar_prefetch=1,
        grid=(db,),
        in_specs=[pl.BlockSpec((1, t, nq * DSA_HEAD_DIM), lambda i, pt: (i, 0, 0)),
                  pl.BlockSpec((1, t, wkv), lambda i, pt: (i, 0, 0)),
                  pl.BlockSpec((1, t, wkv), lambda i, pt: (i, 0, v_blk)),
                  pl.BlockSpec((1, t, lpad), lambda i, pt: (i, 0, 0)),
                  pl.BlockSpec((nq * t, lpad), lambda i, pt: (0, 0)),
                  pl.BlockSpec((1, LANES), lambda i, pt: (0, 0))]
                 + [page(p) for p in range(npages)] * 2,
        out_specs=pl.BlockSpec((1, t, nq * DSA_HEAD_DIM), lambda i, pt: (i, 0, 0)),
    )
    return pl.pallas_call(
        functools.partial(_dsa_sample_kernel, npages=npages, t=t, nq=nq, nkv=nkv),
        grid_spec=grid_spec,
        out_shape=jax.ShapeDtypeStruct((db, t, nq * DSA_HEAD_DIM), BF16),
        compiler_params=_cp(("arbitrary",)),
        name="dsa_sample",
    )(page_table, z, kn, z, mask, bs.reshape(nq * t, lpad), q_gain.reshape(1, LANES),
      *([pool_k] * npages), *([pool_v] * npages))


def _conv_ffn(x2, seq_len, state, norm_g, weights, conv_w, conv_b):
    emit = len(weights) == 2
    if emit:
        w_val = w_gate = weights[0]
        gate_col = weights[0].shape[1] // 2
    else:
        w_val, w_gate = weights[:2]
        gate_col = 0
    mid, ns_val, ns_gate, *wb = ffn_up(x2, seq_len, state, norm_g, w_val, w_gate, gate_col, conv_w, conv_b, emit)
    y = matmul_residual((mid,), weights[-1], x2, emit)
    if emit:
        y, w_down_b = y
        wb.append(w_down_b)
    step = ns_val.shape[0] * seq_len // x2.shape[0]
    return y, jnp.concatenate([ns_val[step - 1::step], ns_gate[step - 1::step]], axis=-1), tuple(wb)


def kernel(x_prompt, x_sample, state_rglru_h, state_rglru_conv, cache_moba_k, cache_moba_v, state_ffn0_conv, cache_dsa_k, cache_dsa_v, cache_dsa_idx_k, state_ffn1_conv, page_table, rel_bias_table, l0_norm_g, l0_w_in, l0_conv_w, l0_conv_b, l0_gate_a_w, l0_gate_a_b, l0_gate_i_w, l0_gate_i_b, l0_lambda, l0_q_norm_g, l0_k_norm_g, l0_w_out, ffn0_norm_g, ffn0_w_up, ffn0_conv_w, ffn0_conv_b, ffn0_w_down, l1_norm_g, l1_w_in, l1_q_norm_g, l1_k_norm_g, l1_w_out, ffn1_norm_g, ffn1_w_up, ffn1_conv_w, ffn1_conv_b, ffn1_w_down):
    bp, s, d = x_prompt.shape
    db, t, _ = x_sample.shape
    w = state_rglru_h.shape[-1]
    f2 = ffn0_w_up.shape[1]
    n_phys = cache_moba_k.shape[0]
    npages = page_table.shape[1]
    past = npages * PAGE_SIZE
    lpad = past + LANES
    nmh = cache_moba_k.shape[2]
    nkv = cache_dsa_k.shape[2]
    nq = l1_w_out.shape[0] // DSA_HEAD_DIM
    assert cache_moba_k.shape[3] == MOBA_HEAD_DIM and cache_dsa_k.shape[3] == DSA_HEAD_DIM
    assert cache_moba_k.shape[1] == PAGE_SIZE and nmh * MOBA_HEAD_DIM == w
    assert rel_bias_table.shape == (N_BUCKETS, nmh) and nq == nmh
    assert l0_w_in.shape[1] == 5 * w

    xp = x_prompt.reshape(bp * s, d)
    xs = x_sample.reshape(db * t, d)
    bt, bsamp = bias_tiles(rel_bias_table, past, t, lpad)

    wa_bd = _regroup_blockdiag(l0_gate_a_w)
    wi_bd = _regroup_blockdiag(l0_gate_i_w)
    pool_mk = cache_moba_k.transpose(0, 2, 3, 1).reshape(n_phys, w, PAGE_SIZE)
    pool_mv = cache_moba_v.transpose(0, 2, 3, 1).reshape(n_phys, w, PAGE_SIZE)

    def even(x2, b, tt, h0, conv_state, attend, w_in, w_out, emit):
        z = norm_matmul(x2, l0_norm_g, w_in, emit)
        z, w_in_b = z if emit else (z, None)
        z = z.reshape(b, tt, 5 * w)
        ya, h_last = rglru_branch(z, h0, conv_state, l0_conv_w, l0_conv_b, wa_bd, l0_gate_a_b,
                                  wi_bd, l0_gate_i_b, l0_lambda)
        yb, kn = attend(z)
        out = matmul_residual((ya.reshape(b * tt, w), yb.reshape(b * tt, w)), w_out, x2, emit)
        out, w_out_b = out if emit else (out, None)
        new_conv = z[:, tt - (l0_conv_w.shape[0] - 1):, :w]
        v_rows = z[:, :, 4 * w:].reshape(b, tt, nmh, MOBA_HEAD_DIM)
        return (out, h_last, new_conv, kn.reshape(b, tt, nmh, MOBA_HEAD_DIM), v_rows), (w_in_b, w_out_b)

    (xs, h_s, conv_s, mk_s, mv_s), (w_in0_b, w_out0_b) = even(
        xs, db, t, state_rglru_h, state_rglru_conv,
        lambda z: moba_sample(z, pool_mk, pool_mv, page_table, bsamp, l0_q_norm_g, l0_k_norm_g, w),
        l0_w_in, l0_w_out, True)
    (xp, h_p, conv_p, mk_p, mv_p), _ = even(
        xp, bp, s, jnp.zeros((bp, w), F32), jnp.zeros((bp, l0_conv_w.shape[0] - 1, w), F32),
        lambda z: moba_prompt(z, rel_bias_table, bt, l0_q_norm_g, l0_k_norm_g, w),
        w_in0_b, w_out0_b, False)

    xs, f0_s, ffn0_wb = _conv_ffn(xs, t, state_ffn0_conv, ffn0_norm_g, (ffn0_w_up, ffn0_w_down),
                                  ffn0_conv_w, ffn0_conv_b)
    xp, f0_p, _ = _conv_ffn(xp, s, jnp.zeros((bp, ffn0_conv_w.shape[0] - 1, f2), F32), ffn0_norm_g, ffn0_wb,
                            ffn0_conv_w, ffn0_conv_b)

    wq = nq * DSA_HEAD_DIM
    wkv = nkv * DSA_HEAD_DIM
    wqi = IDX_HEADS * IDX_DIM
    n_in = l1_w_in.shape[1]
    assert n_in == wq + 2 * wkv + wqi + IDX_DIM + IDX_HEADS
    n_pad = -(-n_in // 512) * 512
    w_in1 = jnp.pad(l1_w_in, ((0, 0), (0, n_pad - n_in)))
    k_blk0 = wq // LANES
    v_blk0 = (wq + wkv) // LANES
    qi_off = wq + 2 * wkv
    assert qi_off % wqi == 0 and (qi_off + wqi) % LANES == 0 and (wq + wkv) % wkv == 0
    qi_blk = qi_off // wqi
    kw_blk = (qi_off + wqi) // LANES
    pool_dk = cache_dsa_k.reshape(n_phys * PAGE_SIZE * nkv, DSA_HEAD_DIM)
    pool_dv = cache_dsa_v.reshape(n_phys * PAGE_SIZE * nkv, DSA_HEAD_DIM)
    pool_ik = cache_dsa_idx_k.transpose(0, 2, 1)

    def odd(x2, b, tt, attend, w_in, w_out, emit):
        z2 = norm_matmul(x2, l1_norm_g, w_in, emit)
        z2, w_in_b = z2 if emit else (z2, None)
        kn, knb, vb = kv_prep(z2, l1_k_norm_g, k_blk0, v_blk0, nkv)
        z = z2.reshape(b, tt, n_pad)
        y = attend(z, kn.reshape(b, tt, wkv), knb.reshape(b, tt, wkv), vb.reshape(b, tt, wkv))
        out = matmul_residual((y.reshape(b * tt, wq),), w_out, x2, emit)
        out, w_out_b = out if emit else (out, None)
        v_rows = z[:, :, wq + wkv:wq + 2 * wkv].reshape(b, tt, nkv, DSA_HEAD_DIM)
        ki_rows = z[:, :, qi_off + wqi:qi_off + wqi + IDX_DIM]
        return (out, kn.reshape(b, tt, nkv, DSA_HEAD_DIM), v_rows, ki_rows), (w_in_b, w_out_b)

    def attend_prompt(z, kn, knb, vb):
        isc = index_prompt(z, qi_blk, kw_blk)
        mask = topk_mask(isc.reshape(bp * s, s), min(DSA_TOPK, s // 4), s, 0, True).reshape(bp, s, s)
        return dsa_prompt(z, knb, vb, mask, rel_bias_table, bt, l1_q_norm_g, nq, nkv)

    def attend_sample(z, kn, knb, vb):
        isc = index_sample(z, pool_ik, page_table, qi_blk, kw_blk)
        mask = topk_mask(isc.reshape(db * t, lpad), min(DSA_TOPK, (past + t) // 4), t, past, False)
        return dsa_sample(z, kn, mask.reshape(db, t, lpad), pool_dk, pool_dv, page_table, bsamp,
                          l1_q_norm_g, nq, nkv, (wq + wkv) // wkv)

    (xs, dk_s, dv_s, di_s), (w_in1_b, w_out1_b) = odd(xs, db, t, attend_sample, w_in1, l1_w_out, True)
    (xp, dk_p, dv_p, di_p), _ = odd(xp, bp, s, attend_prompt, w_in1_b, w_out1_b, False)

    xs, f1_s, ffn1_wb = _conv_ffn(xs, t, state_ffn1_conv, ffn1_norm_g, (ffn1_w_up, ffn1_w_down),
                                  ffn1_conv_w, ffn1_conv_b)
    xp, f1_p, _ = _conv_ffn(xp, s, jnp.zeros((bp, ffn1_conv_w.shape[0] - 1, f2), F32), ffn1_norm_g, ffn1_wb,
                            ffn1_conv_w, ffn1_conv_b)

    return (xp.reshape(bp, s, d), xs.reshape(db, t, d), h_p, h_s, conv_p, conv_s, mk_p, mk_s, mv_p, mv_s,
            f0_p, f0_s, dk_p, dk_s, dv_p, dv_s, di_p, di_s, f1_p, f1_s)
```

```python
import functools
import math

import numpy as np
import jax
import jax.numpy as jnp
from jax import lax
from jax.experimental import pallas as pl
from jax.experimental.pallas import tpu as pltpu

F32 = jnp.float32
BF16 = jnp.bfloat16
I32 = jnp.int32
NEG_INF = float("-inf")
INT_MIN = -(2 ** 31)

EPS = 1e-6
LRU_C = 8.0
LRU_GROUP = 256
MOBA_BLOCK = 256
MOBA_TOPK = 3
MOBA_HEAD_DIM = 64
DSA_HEAD_DIM = 128
DSA_TOPK = 256
IDX_DIM = 64
IDX_HEADS = 8
PAGE_SIZE = 128
N_BUCKETS = 32
MAX_DISTANCE = 128
Q_TILE = 256
LANES = 128
MIB = 1024 * 1024


def _t5_thresholds():
    exact = N_BUCKETS // 2
    n = np.arange(0, 4 * MAX_DISTANCE)
    nf = np.maximum(n, 1).astype(np.float32)
    large = exact + (np.log(nf / np.float32(exact)) / np.float32(math.log(MAX_DISTANCE / exact))
                     * np.float32(N_BUCKETS - exact)).astype(np.int32)
    bucket = np.where(n < exact, n, np.minimum(large, N_BUCKETS - 1))
    assert np.all(np.diff(bucket) >= 0) and bucket[-1] == N_BUCKETS - 1
    return [int(np.argmax(bucket >= b)) for b in range(N_BUCKETS)]


T5_THR = _t5_thresholds()


def _cp(sem, vmem_mib=48):
    return pltpu.CompilerParams(dimension_semantics=sem, vmem_limit_bytes=vmem_mib * MIB)


def _iota(shape, dim):
    return lax.broadcasted_iota(I32, shape, dim)


def _dot_nt(a, b):
    return lax.dot_general(a, b, (((1,), (1,)), ((), ())), preferred_element_type=F32)


def _dot_nn(a, b):
    return jnp.dot(a, b, preferred_element_type=F32)


def _split2(x):
    hi = x.astype(BF16)
    lo = (x - hi.astype(F32)).astype(BF16)
    return hi, lo


def _split3(x):
    hi = x.astype(BF16)
    r = x - hi.astype(F32)
    mid = r.astype(BF16)
    lo = (r - mid.astype(F32)).astype(BF16)
    return hi, mid, lo


def _dot3(dot, a, b):
    ah, al = _split2(a)
    bh, bl = _split2(b)
    return dot(ah, bh) + (dot(ah, bl) + dot(al, bh))


def _dot_exact01(dot, a, b01):
    h, m, l = _split3(a)
    return dot(h, b01) + (dot(m, b01) + dot(l, b01))


def _t5_bias(dist, tab_ref, h):
    acc = jnp.full(dist.shape, tab_ref[0, h], F32)
    for b in range(1, N_BUCKETS):
        acc = jnp.where(dist >= T5_THR[b], tab_ref[b, h], acc)
    return acc


def _softplus(x):
    return jnp.maximum(x, 0.0) + jnp.log1p(jnp.exp(-jnp.abs(x)))


def _norm_mm_kernel(x_ref, g_ref, w_ref, o_ref, *rest, emit):
    xn_ref = rest[-1]

    @pl.when(pl.program_id(1) == 0)
    def _():
        x = x_ref[...]
        ms = jnp.mean(x * x, axis=-1, keepdims=True)
        xn_ref[...] = (x * lax.rsqrt(ms + EPS) * g_ref[...]).astype(BF16)

    wb = w_ref[...].astype(BF16)
    if emit:
        rest[0][...] = wb
    o_ref[...] = _dot_nn(xn_ref[...], wb)


def norm_matmul(x, g, w, emit=False):
    m, d = x.shape
    n = w.shape[1]
    tm = min(m, 1024)
    tn = 1024 if (not emit and n % 1024 == 0) else 512
    assert m % tm == 0 and n % tn == 0 and (not emit or m == tm)
    out_specs = [pl.BlockSpec((tm, tn), lambda i, j: (i, j))]
    out_shape = [jax.ShapeDtypeStruct((m, n), F32)]
    if emit:
        out_specs.append(pl.BlockSpec((d, tn), lambda i, j: (0, j)))
        out_shape.append(jax.ShapeDtypeStruct((d, n), BF16))
    res = pl.pallas_call(
        functools.partial(_norm_mm_kernel, emit=emit),
        grid=(m // tm, n // tn),
        in_specs=[pl.BlockSpec((tm, d), lambda i, j: (i, 0)),
                  pl.BlockSpec((1, d), lambda i, j: (0, 0)),
                  pl.BlockSpec((d, tn), lambda i, j: (0, j))],
        out_specs=out_specs,
        out_shape=out_shape,
        scratch_shapes=[pltpu.VMEM((tm, d), BF16)],
        compiler_params=_cp(("parallel", "arbitrary")),
        name="norm_matmul",
    )(x, g.reshape(1, d), w)
    return res if emit else res[0]


def _mm_res_kernel(*refs, emit, na):
    a_refs = refs[:na]
    w_ref, r_ref, o_ref = refs[na:na + 3]
    wb = w_ref[...].astype(BF16)
    if emit:
        refs[na + 3][...] = wb
    acc = r_ref[...]
    k0 = 0
    for a_ref in a_refs:
        k1 = k0 + a_ref.shape[1]
        acc = acc + _dot_nn(a_ref[...], wb[k0:k1, :])
        k0 = k1
    o_ref[...] = acc


def matmul_residual(a_parts, w, res, emit=False):
    m = a_parts[0].shape[0]
    kdim, n = w.shape
    tm = min(m, 1024)
    tn = 256 if emit else (1024 if kdim <= 2048 and n % 1024 == 0 else 512)
    assert m % tm == 0 and n % tn == 0 and (not emit or m == tm) and sum(a.shape[1] for a in a_parts) == kdim
    out_specs = [pl.BlockSpec((tm, tn), lambda i, j: (i, j))]
    out_shape = [jax.ShapeDtypeStruct((m, n), F32)]
    if emit:
        out_specs.append(pl.BlockSpec((kdim, tn), lambda i, j: (0, j)))
        out_shape.append(jax.ShapeDtypeStruct((kdim, n), BF16))
    out = pl.pallas_call(
        functools.partial(_mm_res_kernel, emit=emit, na=len(a_parts)),
        grid=(m // tm, n // tn),
        in_specs=[pl.BlockSpec((tm, a.shape[1]), lambda i, j: (i, 0)) for a in a_parts]
                 + [pl.BlockSpec((kdim, tn), lambda i, j: (0, j)),
                    pl.BlockSpec((tm, tn), lambda i, j: (i, j))],
        out_specs=out_specs,
        out_shape=out_shape,
        compiler_params=_cp(("parallel", "arbitrary"), vmem_mib=56),
        name="matmul_residual",
    )(*a_parts, w, res)
    return out if emit else out[0]


def _ffn_up_kernel(x_ref, xp_ref, g_ref, wv_ref, wg_ref, sv_ref, sg_ref, cwv_ref, cwg_ref, cbv_ref, cbg_ref,
                   o_ref, nsv_ref, nsg_ref, *rest, nb, tc, chunks, emit):
    xn_ref, xnp_ref = rest[-2:]
    i = pl.program_id(0)
    tm, tn = o_ref.shape

    def norm(x):
        return x * lax.rsqrt(jnp.mean(x * x, axis=-1, keepdims=True) + EPS) * g_ref[...]

    @pl.when(pl.program_id(1) == 0)
    def _():
        xn_ref[...] = norm(x_ref[...]).astype(BF16)
        if chunks > 1:
            xnp_ref[...] = norm(xp_ref[...])

    def project(w_ref, wb_ref):
        wb = w_ref[...].astype(BF16)
        if emit:
            wb_ref[...] = wb
        u = _dot_nn(xn_ref[...], wb)
        up = _dot_nn(xnp_ref[...].astype(BF16), wb) if (nb == 1 and chunks > 1) else None
        return u, up

    def finish(u, up, s_ref, cw_ref, cb_ref, ns_ref):
        cw = cw_ref[...]
        if nb > 1:
            step = _iota((tm, tn), 0) % tc
            s0 = jnp.broadcast_to(s_ref[:, 0:1, :], (nb, tc, tn)).reshape(tm, tn)
            s1 = jnp.broadcast_to(s_ref[:, 1:2, :], (nb, tc, tn)).reshape(tm, tn)
            u1 = jnp.where(step == 0, s1, pltpu.roll(u, 1, 0))
            u2 = jnp.where(step == 0, s0, jnp.where(step == 1, s1, pltpu.roll(u, 2, 0)))
            ns_ref[...] = u.reshape(nb, tc, tn)[:, tc - 2:tc, :]
            return cb_ref[...] + u2 * cw[0:1] + u1 * cw[1:2] + u * cw[2:3]
        ns_ref[0] = u[tm - 2:tm, :]
        prev = s_ref[0]
        if chunks > 1:
            prev = jnp.where(i % chunks == 0, prev, up[6:8, :])
        u1 = pltpu.roll(u, 1, 0)
        u2 = pltpu.roll(u, 2, 0)
        c = cb_ref[...] + u2 * cw[0:1] + u1 * cw[1:2] + u * cw[2:3]
        row = _iota((8, tn), 0)
        t1 = jnp.where(row == 0, prev[1:2, :], u1[0:8, :])
        t2 = jnp.where(row == 0, prev[0:1, :], jnp.where(row == 1, prev[1:2, :], u2[0:8, :]))
        top = cb_ref[...] + t2 * cw[0:1] + t1 * cw[1:2] + u[0:8, :] * cw[2:3]
        return jnp.concatenate([top, c[8:, :]], axis=0)

    wbv_ref, wbg_ref = (rest[0], rest[1]) if emit else (None, None)
    uv = project(wv_ref, wbv_ref)
    ug = project(wg_ref, wbg_ref)
    val = finish(*uv, sv_ref, cwv_ref, cbv_ref, nsv_ref)
    gate = finish(*ug, sg_ref, cwg_ref, cbg_ref, nsg_ref)
    o_ref[...] = (jax.nn.gelu(gate) * val).astype(o_ref.dtype)


def ffn_up(x, seq_len, state, norm_g, w_val, w_gate, gate_col, conv_w, conv_b, emit=False):
    m, d = x.shape
    f2 = conv_w.shape[1]
    f = f2 // 2
    tn = 256 if emit else 512
    tm = min(m, 1024)
    tc = min(seq_len, tm)
    nb = tm // tc
    chunks = seq_len // tc
    assert m % tm == 0 and tm % tc == 0 and seq_len % tc == 0 and tc % 8 == 0 and (nb == 1 or chunks == 1)
    assert f % tn == 0 and gate_col % tn == 0 and conv_w.shape[0] == 3 and (not emit or m == tm)
    ncb = f // tn
    goff = gate_col // tn
    r8 = tm // 8
    nseq = m // seq_len

    def wspec(rows, off):
        return pl.BlockSpec((rows, tn), lambda i, j: (0, j + off))

    def st(off):
        return pl.BlockSpec((nb, 2, tn), lambda i, j: (i // chunks, 0, j + off))

    tail = pl.BlockSpec((nb, 2, tn), lambda i, j: (i, 0, j))
    out_specs = [pl.BlockSpec((tm, tn), lambda i, j: (i, j)), tail, tail]
    out_shape = [jax.ShapeDtypeStruct((m, f), BF16), jax.ShapeDtypeStruct((nseq * chunks, 2, f), F32),
                 jax.ShapeDtypeStruct((nseq * chunks, 2, f), F32)]
    if emit:
        out_specs += [wspec(d, 0), wspec(d, 0)]
        out_shape += [jax.ShapeDtypeStruct((d, f), BF16), jax.ShapeDtypeStruct((d, f), BF16)]
    return pl.pallas_call(
        functools.partial(_ffn_up_kernel, nb=nb, tc=tc, chunks=chunks, emit=emit),
        grid=(m // tm, ncb),
        in_specs=[pl.BlockSpec((tm, d), lambda i, j: (i, 0)),
                  pl.BlockSpec((8, d), lambda i, j: (jnp.maximum(i * r8 - 1, 0), 0)),
                  pl.BlockSpec((1, d), lambda i, j: (0, 0)),
                  wspec(d, 0), wspec(d, goff), st(0), st(ncb),
                  wspec(3, 0), wspec(3, ncb), wspec(1, 0), wspec(1, ncb)],
        out_specs=out_specs,
        out_shape=out_shape,
        scratch_shapes=[pltpu.VMEM((tm, d), BF16), pltpu.VMEM((8, d), F32)],
        compiler_params=_cp(("parallel", "arbitrary"), vmem_mib=56),
        name="ffn_up",
    )(x, x, norm_g.reshape(1, d), w_val, w_gate, state, state, conv_w, conv_w,
      conv_b.reshape(1, f2), conv_b.reshape(1, f2))


def _rglru_kernel(xa_ref, ga_ref, pv_ref, cs_ref, h0_ref, cw_ref, cb_ref, wa_ref, ba_ref,
                  wi_ref, bi_ref, lam_ref, ya_ref, hl_ref, e_ref, hc_ref, *, tc, bb):
    t = pl.program_id(1)
    w = xa_ref.shape[-1]
    rows = bb * tc

    @pl.when(t == 0)
    def _():
        hc_ref[...] = h0_ref[...]
        e_ref[:, 5:8, :] = cs_ref[...]

    @pl.when(t > 0)
    def _():
        e_ref[:, 5:8, :] = pv_ref[:, 5:8, :]

    e_ref[:, 8:8 + tc, :] = xa_ref[...]
    cw = cw_ref[...]
    xc = (cb_ref[...] + e_ref[:, 5:5 + tc, :] * cw[0:1] + e_ref[:, 6:6 + tc, :] * cw[1:2]
          + e_ref[:, 7:7 + tc, :] * cw[2:3] + e_ref[:, 8:8 + tc, :] * cw[3:4]).reshape(rows, w)
    xb = xc.astype(BF16)

    def gate_logits(w_ref):
        return jnp.concatenate([_dot_nn(xb[:, g * LRU_GROUP:(g + 1) * LRU_GROUP], w_ref[g])
                                for g in range(w // LRU_GROUP)], axis=1)

    logits_a = gate_logits(wa_ref)
    logits_i = gate_logits(wi_ref)
    r = jax.nn.sigmoid(logits_a + ba_ref[...])
    i = jax.nn.sigmoid(logits_i + bi_ref[...])
    log_a = -LRU_C * r * _softplus(-lam_ref[...])
    a = jnp.exp(log_a)
    u = jnp.sqrt(-jnp.tanh(log_a) * (1.0 + a * a)) * (i * xc)

    step = _iota((rows, w), 0) % tc
    s = 1
    while s < tc:
        a_sh = jnp.where(step >= s, pltpu.roll(a, s, 0), 1.0)
        u_sh = jnp.where(step >= s, pltpu.roll(u, s, 0), 0.0)
        u = u + a * u_sh
        a = a * a_sh
        s *= 2
    h = u.reshape(bb, tc, w) + a.reshape(bb, tc, w) * hc_ref[...]
    hlast = h[:, tc - 1:tc, :]
    hc_ref[...] = hlast
    hl_ref[...] = hlast
    ya_ref[...] = (h * jax.nn.gelu(ga_ref[...])).astype(ya_ref.dtype)


def rglru_branch(z, h0, conv_state, conv_w, conv_b, wa_bd, ba, wi_bd, bi, lam):
    b, t, _ = z.shape
    w = h0.shape[-1]
    tc = min(t, 256)
    bb = 8 if (tc == t and tc == 8 and b % 8 == 0) else 1
    assert t % tc == 0 and tc % 8 == 0 and conv_w.shape[0] == 4 and w % LRU_GROUP == 0
    r8 = tc // 8
    ng = w // LRU_GROUP
    vec = lambda: pl.BlockSpec((1, w), lambda i, j: (0, 0))
    ya, hl = pl.pallas_call(
        functools.partial(_rglru_kernel, tc=tc, bb=bb),
        grid=(b // bb, t // tc),
        in_specs=[pl.BlockSpec((bb, tc, w), lambda i, j: (i, j, 0)),
                  pl.BlockSpec((bb, tc, w), lambda i, j: (i, j, 1)),
                  pl.BlockSpec((bb, 8, w), lambda i, j: (i, jnp.maximum(j * r8 - 1, 0), 0)),
                  pl.BlockSpec((bb, 3, w), lambda i, j: (i, 0, 0)),
                  pl.BlockSpec((bb, 1, w), lambda i, j: (i, 0, 0)),
                  pl.BlockSpec((4, w), lambda i, j: (0, 0)),
                  vec(),
                  pl.BlockSpec((ng, LRU_GROUP, LRU_GROUP), lambda i, j: (0, 0, 0)),
                  vec(),
                  pl.BlockSpec((ng, LRU_GROUP, LRU_GROUP), lambda i, j: (0, 0, 0)),
                  vec(), vec()],
        out_specs=[pl.BlockSpec((bb, tc, w), lambda i, j: (i, j, 0)),
                   pl.BlockSpec((bb, 1, w), lambda i, j: (i, 0, 0))],
        out_shape=[jax.ShapeDtypeStruct((b, t, w), BF16), jax.ShapeDtypeStruct((b, 1, w), F32)],
        scratch_shapes=[pltpu.VMEM((bb, tc + 8, w), F32), pltpu.VMEM((bb, 1, w), F32)],
        compiler_params=_cp(("parallel", "arbitrary")),
        name="rglru",
    )(z, z, z, conv_state, h0.reshape(b, 1, w), conv_w, conv_b.reshape(1, w), wa_bd,
      ba.reshape(1, w), wi_bd, bi.reshape(1, w), lam.reshape(1, w))
    return ya, hl.reshape(b, w)


def _regroup_blockdiag(wb):
    n, c, _ = wb.shape
    per = LRU_GROUP // c
    g = n // per
    eye = jnp.eye(per, dtype=wb.dtype)
    out = jnp.einsum("gacd,ab->gacbd", wb.reshape(g, per, c, c), eye)
    return out.reshape(g, LRU_GROUP, LRU_GROUP).astype(BF16)


def _bias_kernel(tab_ref, bt_ref, bs_ref, *, past, lpad):
    h = pl.program_id(0)
    d0 = _iota((Q_TILE, Q_TILE), 0) - _iota((Q_TILE, Q_TILE), 1)
    bt_ref[0, 0] = _t5_bias(d0, tab_ref, h)
    bt_ref[0, 1] = _t5_bias(d0 + Q_TILE, tab_ref, h)
    t = bs_ref.shape[1]
    ds = past + _iota((t, lpad), 0) - _iota((t, lpad), 1)
    bs_ref[0] = _t5_bias(ds, tab_ref, h)


def bias_tiles(table, past, t_new, lpad):
    nh = table.shape[1]
    return pl.pallas_call(
        functools.partial(_bias_kernel, past=past, lpad=lpad),
        grid=(nh,),
        in_specs=[pl.BlockSpec(memory_space=pltpu.SMEM)],
        out_specs=[pl.BlockSpec((1, 2, Q_TILE, Q_TILE), lambda h: (h, 0, 0, 0)),
                   pl.BlockSpec((1, t_new, lpad), lambda h: (h, 0, 0))],
        out_shape=[jax.ShapeDtypeStruct((nh, 2, Q_TILE, Q_TILE), F32),
                   jax.ShapeDtypeStruct((nh, t_new, lpad), F32)],
        compiler_params=_cp(("arbitrary",)),
        name="t5_bias_tiles",
    )(table)


def _top_lanes(g, nvalid, k):
    lane = _iota(g.shape, 1).astype(F32)
    gg = jnp.where(lane < nvalid, g, NEG_INF)
    sel = jnp.zeros(g.shape, F32)
    for _ in range(k):
        mx = jnp.max(gg, axis=-1, keepdims=True)
        first = jnp.min(jnp.where(gg == mx, lane, float(g.shape[1])), axis=-1, keepdims=True)
        pick = (lane == first) & (mx > NEG_INF)
        sel = jnp.where(pick, 1.0, sel)
        gg = jnp.where(pick, NEG_INF, gg)
    return sel


def _moba_prompt_kernel(tab_ref, q_ref, k_ref, v_ref, qg_ref, kg_ref, bt_ref, y_ref, kn_ref, vo_ref, *, s):
    hp = pl.program_id(0)
    nb = s // MOBA_BLOCK
    dh = MOBA_HEAD_DIM
    scale = dh ** -0.5
    lane = _iota((1, LANES), 1)
    low = lane < dh

    def headnorm(x, g):
        x2 = x * x
        s0 = jnp.sum(jnp.where(low, x2, 0.0), axis=-1, keepdims=True)
        s1 = jnp.sum(jnp.where(low, 0.0, x2), axis=-1, keepdims=True)
        ms = jnp.where(low, s0, s1) * (1.0 / dh)
        return x * lax.rsqrt(ms + EPS) * g

    qn = headnorm(q_ref[0], qg_ref[...])
    kn = headnorm(k_ref[0], kg_ref[...])
    kn_ref[0] = kn
    vo_ref[0] = v_ref[0]
    kb = kn.astype(BF16)
    vb = v_ref[0].astype(BF16)
    blk_row = _iota((LANES, LANES), 0)
    kmat = jnp.zeros((LANES, LANES), F32)
    for j in range(nb):
        kmean = jnp.mean(kn[j * MOBA_BLOCK:(j + 1) * MOBA_BLOCK], axis=0, keepdims=True)
        kmat = jnp.where(blk_row == j, kmean, kmat)
    qi = _iota((Q_TILE, Q_TILE), 0)
    kj = _iota((Q_TILE, Q_TILE), 1)
    causal = qi >= kj

    near = []
    for hh in range(2):
        t31 = tab_ref[N_BUCKETS - 1, 2 * hp + hh]
        near.append((jnp.where(causal, bt_ref[hh, 0] - t31, NEG_INF), bt_ref[hh, 1] - t31))

    work = [(c, hh) for c in range(nb) for hh in range(2)]
    scs, sels = {}, {}
    for c, hh in work:
        hmask = low if hh == 0 else jnp.logical_not(low)
        qm = jnp.where(hmask, qn[c * Q_TILE:(c + 1) * Q_TILE], 0.0)
        scs[c, hh] = _dot_nt((qm * scale).astype(BF16), kb[0:(c + 1) * MOBA_BLOCK])
        if c > MOBA_TOPK:
            sels[c, hh] = _top_lanes(_dot3(_dot_nt, qm, kmat), c, MOBA_TOPK)
    probs, sums = {}, {}
    for c, hh in work:
        pieces = []
        for j in range(c + 1):
            sj = scs[c, hh][:, j * MOBA_BLOCK:(j + 1) * MOBA_BLOCK]
            if j == c:
                sj = sj + near[hh][0]
            else:
                if j == c - 1:
                    sj = sj + near[hh][1]
                if c > MOBA_TOPK:
                    sj = jnp.where(sels[c, hh][:, j:j + 1] > 0.0, sj, NEG_INF)
            pieces.append(sj)
        srow = jnp.concatenate(pieces, axis=1) if c else pieces[0]
        mx = jnp.max(srow, axis=-1, keepdims=True)
        p = jnp.exp(srow - mx)
        sums[c, hh] = jnp.sum(p, axis=-1, keepdims=True)
        probs[c, hh] = p.astype(BF16)
    for c in sorted({c for c, _ in work}):
        outs = [_dot_nn(probs[c, hh], vb[0:(c + 1) * MOBA_BLOCK]) / sums[c, hh] for hh in range(2)]
        y_ref[0, c * Q_TILE:(c + 1) * Q_TILE, :] = jnp.where(low, outs[0], outs[1]).astype(y_ref.dtype)


def moba_prompt(z, table, bt, q_gain, k_gain, w):
    b, s, _ = z.shape
    assert s % MOBA_BLOCK == 0 and Q_TILE == MOBA_BLOCK and s // MOBA_BLOCK <= LANES
    nhp = w // LANES
    g2 = lambda g: jnp.tile(g, LANES // MOBA_HEAD_DIM).reshape(1, LANES)
    col = lambda off: pl.BlockSpec((1, s, LANES), lambda hp, i: (i, 0, off * nhp + hp))
    gspec = pl.BlockSpec((1, LANES), lambda hp, i: (0, 0))
    return pl.pallas_call(
        functools.partial(_moba_prompt_kernel, s=s),
        grid=(nhp, b),
        in_specs=[pl.BlockSpec(memory_space=pltpu.SMEM), col(2), col(3), col(4), gspec, gspec,
                  pl.BlockSpec((2, 2, Q_TILE, Q_TILE), lambda hp, i: (hp, 0, 0, 0))],
        out_specs=[pl.BlockSpec((1, s, LANES), lambda hp, i: (i, 0, hp))] * 3,
        out_shape=[jax.ShapeDtypeStruct((b, s, w), BF16), jax.ShapeDtypeStruct((b, s, w), F32),
                   jax.ShapeDtypeStruct((b, s, w), F32)],
        compiler_params=_cp(("parallel", "parallel")),
        name="moba_prompt",
    )(table, z, z, z, g2(q_gain), g2(k_gain), bt)


def _group_onehot(ngroups_pad, width, gsize):
    return jnp.where(_iota((ngroups_pad, width), 0) == _iota((ngroups_pad, width), 1) // gsize,
                     1.0, 0.0).astype(BF16)


def _group_rmsnorm(x, gain, gsize):
    onehot = _group_onehot(LANES, x.shape[1], gsize)
    ssq = _dot_exact01(_dot_nt, x * x, onehot)
    rinv = lax.rsqrt(ssq * (1.0 / gsize) + EPS)
    return x * _dot_exact01(_dot_nn, rinv, onehot) * gain


def _moba_sample_kernel(pt_ref, q_ref, k_ref, v_ref, qg_ref, kg_ref, bs_ref, *rest, npages, t):
    kp_refs = rest[:npages]
    vp_refs = rest[npages:2 * npages]
    y_ref, kn_ref = rest[2 * npages:]
    w = q_ref.shape[-1]
    dh = MOBA_HEAD_DIM
    nh = w // dh
    rows = nh * t
    ppb = MOBA_BLOCK // PAGE_SIZE
    nbp = npages // ppb
    past = npages * PAGE_SIZE
    scale = dh ** -0.5

    qn = _group_rmsnorm(q_ref[0], qg_ref[...], dh)
    kn = _group_rmsnorm(k_ref[0], kg_ref[...], dh)
    kn_ref[0] = kn
    own = _iota((rows, w), 0) // t == _iota((rows, w), 1) // dh
    qbd = jnp.where(own, jnp.concatenate([qn] * nh, axis=0), 0.0)
    qbb = qbd.astype(BF16)
    pad_rows = lambda x: jnp.concatenate([x, jnp.zeros((LANES - t, w), x.dtype)], axis=0)

    pieces = [_dot_nn(qbb, kp_refs[p][0].astype(BF16)) for p in range(npages)]
    pieces.append(_dot_nt(qbb, pad_rows(kn).astype(BF16)))
    sums = [jnp.sum(sum(kp_refs[p][0] for p in range(j * ppb, (j + 1) * ppb)), axis=1, keepdims=True)
            for j in range(nbp)]
    lpad = past + LANES
    srow = jnp.concatenate(pieces, axis=1) * scale + bs_ref[...]

    blk_lane = _iota((w, LANES), 1)
    kmat = jnp.zeros((w, LANES), F32)
    for j in range(nbp):
        kmat = jnp.where(blk_lane == j, sums[j] * (1.0 / MOBA_BLOCK), kmat)
    sel = _top_lanes(_dot3(_dot_nn, qbd, kmat), nbp, MOBA_TOPK)
    col = _iota((rows, lpad), 1)
    jrow = _iota((rows, lpad), 0) % t
    keep = (col >= past) & (col - past <= jrow)
    for j in range(nbp):
        inblk = (col >= j * MOBA_BLOCK) & (col < (j + 1) * MOBA_BLOCK)
        keep = keep | (inblk & (sel[:, j:j + 1] > 0.0))
    srow = jnp.where(keep, srow, NEG_INF)
    mx = jnp.max(srow, axis=-1, keepdims=True)
    pr = jnp.exp(srow - mx)
    l = jnp.sum(pr, axis=-1, keepdims=True)
    pb = pr.astype(BF16)
    acc = _dot_nn(pb[:, past:], pad_rows(v_ref[0]).astype(BF16))
    for p in range(npages):
        acc = acc + _dot_nt(pb[:, p * PAGE_SIZE:(p + 1) * PAGE_SIZE], vp_refs[p][0].astype(BF16))
    acc = acc / l
    lane_head = _iota((t, w), 1) // dh
    y = jnp.zeros((t, w), F32)
    for h in range(nh):
        y = jnp.where(lane_head == h, acc[h * t:(h + 1) * t, :], y)
    y_ref[0] = y.astype(y_ref.dtype)


def moba_sample(z, pool_k, pool_v, page_table, bs, q_gain, k_gain, w):
    db, t, _ = z.shape
    npages = page_table.shape[1]
    nh = w // MOBA_HEAD_DIM
    assert t == 8 and nh * t == LANES and npages % (MOBA_BLOCK // PAGE_SIZE) == 0
    lpad = npages * PAGE_SIZE + LANES
    gt = lambda g: jnp.tile(g, nh).reshape(1, w)
    col = lambda off: pl.BlockSpec((1, t, w), lambda i, pt: (i, 0, off))
    gspec = pl.BlockSpec((1, w), lambda i, pt: (0, 0))
    page = lambda p: pl.BlockSpec((1, w, PAGE_SIZE), lambda i, pt: (pt[i, p], 0, 0))
    grid_spec = pltpu.PrefetchScalarGridSpec(
        num_scalar_prefetch=1,
        grid=(db,),
        in_specs=[col(2), col(3), col(4), gspec, gspec,
                  pl.BlockSpec((nh * t, lpad), lambda i, pt: (0, 0))]
                 + [page(p) for p in range(npages)] * 2,
        out_specs=[pl.BlockSpec((1, t, w), lambda i, pt: (i, 0, 0)),
                   pl.BlockSpec((1, t, w), lambda i, pt: (i, 0, 0))],
    )
    return pl.pallas_call(
        functools.partial(_moba_sample_kernel, npages=npages, t=t),
        grid_spec=grid_spec,
        out_shape=[jax.ShapeDtypeStruct((db, t, w), BF16), jax.ShapeDtypeStruct((db, t, w), F32)],
        compiler_params=_cp(("arbitrary",), vmem_mib=56),
        name="moba_sample",
    )(page_table, z, z, z, gt(q_gain), gt(k_gain), bs.reshape(nh * t, lpad),
      *([pool_k] * npages), *([pool_v] * npages))


def _kv_prep_kernel(k_ref, v_ref, g_ref, kn_ref, knb_ref, vb_ref, vf_ref):
    x = k_ref[...]
    ms = jnp.mean(x * x, axis=-1, keepdims=True)
    kn = x * lax.rsqrt(ms + EPS) * g_ref[...]
    kn_ref[...] = kn
    knb_ref[...] = kn.astype(BF16)
    vb_ref[...] = v_ref[...].astype(BF16)
    vf_ref[...] = v_ref[...]


def kv_prep(z, gain, k_blk0, v_blk0, nheads):
    m = z.shape[0]
    tm = min(m, 1024)
    assert m % tm == 0
    out = pl.BlockSpec((tm, LANES), lambda i, h: (i, h))
    return pl.pallas_call(
        _kv_prep_kernel,
        grid=(m // tm, nheads),
        in_specs=[pl.BlockSpec((tm, LANES), lambda i, h: (i, k_blk0 + h)),
                  pl.BlockSpec((tm, LANES), lambda i, h: (i, v_blk0 + h)),
                  pl.BlockSpec((1, LANES), lambda i, h: (0, 0))],
        out_specs=[out, out, out, out],
        out_shape=[jax.ShapeDtypeStruct((m, nheads * LANES), F32),
                   jax.ShapeDtypeStruct((m, nheads * LANES), BF16),
                   jax.ShapeDtypeStruct((m, nheads * LANES), BF16),
                   jax.ShapeDtypeStruct((m, nheads * LANES), F32)],
        compiler_params=_cp(("parallel", "parallel")),
        name="kv_prep",
    )(z, z, gain.reshape(1, LANES))


def _index_prompt_kernel(qi_ref, qw_ref, kk_ref, o_ref, *, s):
    c = pl.program_id(1)
    lane = _iota((1, LANES), 1)
    low = lane < IDX_DIM
    qw = qw_ref[0]

    def split(x):
        hi = x.astype(BF16).astype(F32)
        return hi, (x - hi).astype(BF16).astype(F32)

    def tile(cc):
        nk = (cc + 1) * Q_TILE
        kh, kl = split(jnp.where(low, kk_ref[0, 0:nk, :], 0.0))
        rhs = jnp.concatenate([kh + pltpu.roll(kh, IDX_DIM, 1), kl], axis=1).astype(BF16)
        scs = []
        for pair in range(IDX_HEADS // 2):
            qh, ql = split(qi_ref[0, :, pair * LANES:(pair + 1) * LANES])
            qh_sw = pltpu.roll(qh, IDX_DIM, 1)
            ql_sw = pltpu.roll(ql, IDX_DIM, 1)
            for hi_low, lo_high in ((qh, ql_sw), (qh_sw, ql)):
                lhs = jnp.concatenate([jnp.where(low, hi_low, lo_high), jnp.where(low, hi_low, 0.0)],
                                      axis=1).astype(BF16)
                scs.append(_dot_nt(lhs, rhs))
        acc = jnp.zeros((Q_TILE, nk), F32)
        for ih, sc in enumerate(scs):
            wcol = qw[:, IDX_DIM + ih:IDX_DIM + ih + 1] * (IDX_HEADS ** -0.5)
            acc = acc + wcol * jnp.maximum(sc * (IDX_DIM ** -0.5), 0.0)
        qpos = cc * Q_TILE + _iota((Q_TILE, nk), 0)
        o_ref[0, :, 0:nk] = jnp.where(_iota((Q_TILE, nk), 1) <= qpos, acc, NEG_INF)
        if nk < s:
            o_ref[0, :, nk:] = jnp.full((Q_TILE, s - nk), NEG_INF, F32)

    for cc in range(s // Q_TILE):
        pl.when(c == cc)(functools.partial(tile, cc))


def index_prompt(z, qi_blk, kw_blk):
    b, s, _ = z.shape
    assert s % Q_TILE == 0
    return pl.pallas_call(
        functools.partial(_index_prompt_kernel, s=s),
        grid=(b, s // Q_TILE),
        in_specs=[pl.BlockSpec((1, Q_TILE, IDX_HEADS * IDX_DIM), lambda i, c: (i, c, qi_blk)),
                  pl.BlockSpec((1, Q_TILE, LANES), lambda i, c: (i, c, kw_blk)),
                  pl.BlockSpec((1, s, LANES), lambda i, c: (i, 0, kw_blk))],
        out_specs=pl.BlockSpec((1, Q_TILE, s), lambda i, c: (i, c, 0)),
        out_shape=jax.ShapeDtypeStruct((b, s, s), F32),
        compiler_params=_cp(("parallel", "parallel")),
        name="index_prompt",
    )(z, z, z)


def _index_sample_kernel(pt_ref, qi_ref, qw_ref, *rest, npages, t):
    kp_refs = rest[:npages]
    o_ref = rest[npages]
    qi = qi_ref[0]
    qw = qw_ref[0]
    q2 = jnp.concatenate([qi[:, ih * IDX_DIM:(ih + 1) * IDX_DIM] for ih in range(IDX_HEADS)], axis=0)
    wcol = jnp.concatenate([qw[:, IDX_DIM + ih:IDX_DIM + ih + 1] for ih in range(IDX_HEADS)], axis=0)
    wcol = wcol * (IDX_HEADS ** -0.5)
    knew = jnp.concatenate([qw[:, 0:IDX_DIM], jnp.zeros((LANES - t, IDX_DIM), F32)], axis=0)

    def scores(dot, keys):
        sc = wcol * jnp.maximum(_dot3(dot, q2, keys) * (IDX_DIM ** -0.5), 0.0)
        return sum(sc[ih * t:(ih + 1) * t, :] for ih in range(IDX_HEADS))

    pieces = [scores(_dot_nn, kp_refs[p][0]) for p in range(npages)]
    new = scores(_dot_nt, knew)
    jj = _iota((t, LANES), 0)
    cc = _iota((t, LANES), 1)
    pieces.append(jnp.where(cc <= jj, new, NEG_INF))
    o_ref[0] = jnp.concatenate(pieces, axis=1)


def index_sample(z, pool_ik, page_table, qi_blk, kw_blk):
    db, t, _ = z.shape
    npages = page_table.shape[1]
    lpad = npages * PAGE_SIZE + LANES
    grid_spec = pltpu.PrefetchScalarGridSpec(
        num_scalar_prefetch=1,
        grid=(db,),
        in_specs=[pl.BlockSpec((1, t, IDX_HEADS * IDX_DIM), lambda i, pt: (i, 0, qi_blk)),
                  pl.BlockSpec((1, t, LANES), lambda i, pt: (i, 0, kw_blk))]
                 + [pl.BlockSpec((1, IDX_DIM, PAGE_SIZE), (lambda i, pt, p=p: (pt[i, p], 0, 0)))
                    for p in range(npages)],
        out_specs=pl.BlockSpec((1, t, lpad), lambda i, pt: (i, 0, 0)),
    )
    return pl.pallas_call(
        functools.partial(_index_sample_kernel, npages=npages, t=t),
        grid_spec=grid_spec,
        out_shape=jax.ShapeDtypeStruct((db, t, lpad), F32),
        compiler_params=_cp(("arbitrary",)),
        name="index_sample",
    )(page_table, z, z, *([pool_ik] * npages))


def _topk_mask_kernel(x_ref, o_ref, *, topk, period, base, tr, additive, ncausal):
    l = x_ref.shape[1]
    on, off = (0.0, NEG_INF) if additive else (1.0, 0.0)

    def body(nk):
        rowg = pl.program_id(0) * tr + _iota((tr, nk), 0)
        valid = _iota((tr, nk), 1) <= base + rowg % period
        bits = pltpu.bitcast(x_ref[:, 0:nk], I32)
        key = jnp.where(bits < 0, bits ^ 0x7FFFFFFF, bits)
        key = jnp.where(valid, key, INT_MIN)

        def step(i, lo):
            cand = lo + lax.shift_left(jnp.int32(1), 31 - i)
            cnt = jnp.sum(jnp.where(key >= cand, 1.0, 0.0), axis=-1, keepdims=True)
            return jnp.where(cnt >= topk, cand, lo)

        thr = lax.fori_loop(0, 32, step, jnp.full((tr, 1), INT_MIN, I32))
        gt = key > thr
        need = topk - jnp.sum(jnp.where(gt, 1.0, 0.0), axis=-1, keepdims=True)
        eq = jnp.where(key == thr, 1.0, 0.0)
        upper = jnp.where(_iota((LANES, LANES), 0) < _iota((LANES, LANES), 1), 1.0, 0.0).astype(BF16)
        before = jnp.zeros((tr, 1), F32)
        for j in range(nk // LANES):
            e = eq[:, j * LANES:(j + 1) * LANES]
            rank = before + _dot_nn(e.astype(BF16), upper)
            pick = gt[:, j * LANES:(j + 1) * LANES] | ((e > 0.0) & (rank < need))
            pick = pick & valid[:, j * LANES:(j + 1) * LANES]
            o_ref[:, j * LANES:(j + 1) * LANES] = jnp.where(pick, on, off)
            before = before + jnp.sum(e, axis=-1, keepdims=True)
        if nk < l:
            o_ref[:, nk:] = jnp.full((tr, l - nk), off, F32)

    if ncausal > 1:
        c = pl.program_id(0) % ncausal
        for cc in range(ncausal):
            pl.when(c == cc)(functools.partial(body, (cc + 1) * tr))
    else:
        body(l)


def topk_mask(scores, topk, period, base, additive):
    r, l = scores.shape
    tr = min(r, 256)
    assert r % tr == 0 and l % LANES == 0
    ncausal = period // tr if (base == 0 and period == l and period % tr == 0) else 1
    return pl.pallas_call(
        functools.partial(_topk_mask_kernel, topk=topk, period=period, base=base, tr=tr, additive=additive,
                          ncausal=ncausal),
        grid=(r // tr,),
        in_specs=[pl.BlockSpec((tr, l), lambda i: (i, 0))],
        out_specs=pl.BlockSpec((tr, l), lambda i: (i, 0)),
        out_shape=jax.ShapeDtypeStruct((r, l), F32),
        compiler_params=_cp(("parallel",)),
        name="topk_mask",
    )(scores)


def _dsa_prompt_kernel(tab_ref, q_ref, k_ref, v_ref, m_ref, bt_ref, qg_ref, y_ref, *, s):
    c = pl.program_id(1)
    g = pl.program_id(2)
    dh = DSA_HEAD_DIM
    nh = q_ref.shape[-1] // dh
    qbs, t31s = [], []
    for hh in range(nh):
        q = q_ref[0, :, hh * dh:(hh + 1) * dh]
        qn = q * lax.rsqrt(jnp.mean(q * q, axis=-1, keepdims=True) + EPS) * qg_ref[...]
        qbs.append((qn * (dh ** -0.5)).astype(BF16))
        t31s.append(tab_ref[N_BUCKETS - 1, g * nh + hh])

    def tile(cc):
        nk = (cc + 1) * Q_TILE
        madd = m_ref[0, :, 0:nk]
        scs = [_dot_nt(qbs[hh], k_ref[0, 0:nk, :]) for hh in range(nh)]
        probs, sums = [], []
        for hh in range(nh):
            sc = scs[hh] + madd
            pieces = [sc[:, 0:(cc - 1) * Q_TILE]] if cc > 1 else []
            if cc > 0:
                pieces.append(sc[:, (cc - 1) * Q_TILE:cc * Q_TILE] + (bt_ref[hh, 1] - t31s[hh]))
            pieces.append(sc[:, cc * Q_TILE:nk] + (bt_ref[hh, 0] - t31s[hh]))
            srow = jnp.concatenate(pieces, axis=1) if cc else pieces[0]
            mx = jnp.max(srow, axis=-1, keepdims=True)
            p = jnp.exp(srow - mx)
            sums.append(jnp.sum(p, axis=-1, keepdims=True))
            probs.append(p.astype(BF16))
        for hh in range(nh):
            y_ref[0, :, hh * dh:(hh + 1) * dh] = (_dot_nn(probs[hh], v_ref[0, 0:nk, :]) / sums[hh]
                                                  ).astype(y_ref.dtype)

    for cc in range(s // Q_TILE):
        pl.when(c == cc)(functools.partial(tile, cc))


def dsa_prompt(z, knb, vb, madd, table, bt, q_gain, nq, nkv):
    b, s, _ = z.shape
    nh = nq // nkv
    return pl.pallas_call(
        functools.partial(_dsa_prompt_kernel, s=s),
        grid=(b, s // Q_TILE, nkv),
        in_specs=[pl.BlockSpec(memory_space=pltpu.SMEM),
                  pl.BlockSpec((1, Q_TILE, nh * LANES), lambda i, c, g: (i, c, g)),
                  pl.BlockSpec((1, s, LANES), lambda i, c, g: (i, 0, g)),
                  pl.BlockSpec((1, s, LANES), lambda i, c, g: (i, 0, g)),
                  pl.BlockSpec((1, Q_TILE, s), lambda i, c, g: (i, c, 0)),
                  pl.BlockSpec((nh, 2, Q_TILE, Q_TILE), lambda i, c, g: (g, 0, 0, 0)),
                  pl.BlockSpec((1, LANES), lambda i, c, g: (0, 0))],
        out_specs=pl.BlockSpec((1, Q_TILE, nh * LANES), lambda i, c, g: (i, c, g)),
        out_shape=jax.ShapeDtypeStruct((b, s, nq * LANES), BF16),
        compiler_params=_cp(("parallel", "parallel", "arbitrary")),
        name="dsa_prompt",
    )(table, z, knb, vb, madd, bt, q_gain.reshape(1, LANES))


def _dsa_sample_kernel(pt_ref, q_ref, kn_ref, v_ref, m_ref, bs_ref, qg_ref, *rest, npages, t, nq, nkv):
    kp_refs = rest[:npages]
    vp_refs = rest[npages:2 * npages]
    y_ref = rest[2 * npages]
    dh = DSA_HEAD_DIM
    grp = nq // nkv
    rows = grp * t
    past = npages * PAGE_SIZE
    scale = dh ** -0.5
    q = q_ref[0]
    gain = qg_ref[...]
    keep = jnp.concatenate([m_ref[0]] * grp, axis=0) > 0.5
    pad_rows = lambda x: jnp.concatenate([x, jnp.zeros((LANES - t, dh), x.dtype)], axis=0)

    scores = []
    for g in range(nkv):
        qs = []
        for h in range(g * grp, (g + 1) * grp):
            qh = q[:, h * dh:(h + 1) * dh]
            qs.append(qh * lax.rsqrt(jnp.mean(qh * qh, axis=-1, keepdims=True) + EPS) * gain)
        qg = (jnp.concatenate(qs, axis=0) * scale).astype(BF16)
        pieces = [_dot_nt(qg, kp_refs[p][pl.ds(g, PAGE_SIZE, stride=nkv), :].astype(BF16))
                  for p in range(npages)]
        pieces.append(_dot_nt(qg, pad_rows(kn_ref[0, :, g * dh:(g + 1) * dh]).astype(BF16)))
        scores.append(jnp.concatenate(pieces, axis=1))
    probs = []
    for g in range(nkv):
        srow = jnp.where(keep, scores[g] + bs_ref[g * rows:(g + 1) * rows, :], NEG_INF)
        mx = jnp.max(srow, axis=-1, keepdims=True)
        pr = jnp.exp(srow - mx)
        probs.append((pr.astype(BF16), jnp.sum(pr, axis=-1, keepdims=True)))
    outs = []
    for g in range(nkv):
        pb, l = probs[g]
        acc = _dot_nn(pb[:, past:], pad_rows(v_ref[0, :, g * dh:(g + 1) * dh]).astype(BF16))
        for p in range(npages):
            acc = acc + _dot_nn(pb[:, p * PAGE_SIZE:(p + 1) * PAGE_SIZE],
                                vp_refs[p][pl.ds(g, PAGE_SIZE, stride=nkv), :].astype(BF16))
        acc = acc / l
        outs += [acc[hh * t:(hh + 1) * t, :] for hh in range(grp)]
    y_ref[0] = jnp.concatenate(outs, axis=1).astype(y_ref.dtype)


def dsa_sample(z, kn, mask, pool_k, pool_v, page_table, bs, q_gain, nq, nkv, v_blk):
    db, t, _ = z.shape
    npages = page_table.shape[1]
    wkv = nkv * DSA_HEAD_DIM
    lpad = npages * PAGE_SIZE + LANES
    page = lambda p: pl.BlockSpec((PAGE_SIZE * nkv, DSA_HEAD_DIM), lambda i, pt: (pt[i, p], 0))
    grid_spec = pltpu.PrefetchScalarGridSpec(
        num_scalar_prefetch=1,
        grid=(db,),
        in_specs=[pl.BlockSpec((1, t, nq * DSA_HEAD_DIM), lambda i, pt: (i, 0, 0)),
                  pl.BlockSpec((1, t, wkv), lambda i, pt: (i, 0, 0)),
                  pl.BlockSpec((1, t, wkv), lambda i, pt: (i, 0, v_blk)),
                  pl.BlockSpec((1, t, lpad), lambda i, pt: (i, 0, 0)),
                  pl.BlockSpec((nq * t, lpad), lambda i, pt: (0, 0)),
                  pl.BlockSpec((1, LANES), lambda i, pt: (0, 0))]
                 + [page(p) for p in range(npages)] * 2,
        out_specs=pl.BlockSpec((1, t, nq * DSA_HEAD_DIM), lambda i, pt: (i, 0, 0)),
    )
    return pl.pallas_call(
        functools.partial(_dsa_sample_kernel, npages=npages, t=t, nq=nq, nkv=nkv),
        grid_spec=grid_spec,
        out_shape=jax.ShapeDtypeStruct((db, t, nq * DSA_HEAD_DIM), BF16),
        compiler_params=_cp(("arbitrary",)),
        name="dsa_sample",
    )(page_table, z, kn, z, mask, bs.reshape(nq * t, lpad), q_gain.reshape(1, LANES),
      *([pool_k] * npages), *([pool_v] * npages))


def _conv_ffn(x2, seq_len, state, norm_g, weights, conv_w, conv_b):
    emit = len(weights) == 2
    if emit:
        w_val = w_gate = weights[0]
        gate_col = weights[0].shape[1] // 2
    else:
        w_val, w_gate = weights[:2]
        gate_col = 0
    mid, ns_val, ns_gate, *wb = ffn_up(x2, seq_len, state, norm_g, w_val, w_gate, gate_col, conv_w, conv_b, emit)
    y = matmul_residual((mid,), weights[-1], x2, emit)
    if emit:
        y, w_down_b = y
        wb.append(w_down_b)
    step = ns_val.shape[0] * seq_len // x2.shape[0]
    return y, jnp.concatenate([ns_val[step - 1::step], ns_gate[step - 1::step]], axis=-1), tuple(wb)


def kernel(x_prompt, x_sample, state_rglru_h, state_rglru_conv, cache_moba_k, cache_moba_v, state_ffn0_conv, cache_dsa_k, cache_dsa_v, cache_dsa_idx_k, state_ffn1_conv, page_table, rel_bias_table, l0_norm_g, l0_w_in, l0_conv_w, l0_conv_b, l0_gate_a_w, l0_gate_a_b, l0_gate_i_w, l0_gate_i_b, l0_lambda, l0_q_norm_g, l0_k_norm_g, l0_w_out, ffn0_norm_g, ffn0_w_up, ffn0_conv_w, ffn0_conv_b, ffn0_w_down, l1_norm_g, l1_w_in, l1_q_norm_g, l1_k_norm_g, l1_w_out, ffn1_norm_g, ffn1_w_up, ffn1_conv_w, ffn1_conv_b, ffn1_w_down):
    bp, s, d = x_prompt.shape
    db, t, _ = x_sample.shape
    w = state_rglru_h.shape[-1]
    f2 = ffn0_w_up.shape[1]
    n_phys = cache_moba_k.shape[0]
    npages = page_table.shape[1]
    past = npages * PAGE_SIZE
    lpad = past + LANES
    nmh = cache_moba_k.shape[2]
    nkv = cache_dsa_k.shape[2]
    nq = l1_w_out.shape[0] // DSA_HEAD_DIM
    assert cache_moba_k.shape[3] == MOBA_HEAD_DIM and cache_dsa_k.shape[3] == DSA_HEAD_DIM
    assert cache_moba_k.shape[1] == PAGE_SIZE and nmh * MOBA_HEAD_DIM == w
    assert rel_bias_table.shape == (N_BUCKETS, nmh) and nq == nmh
    assert l0_w_in.shape[1] == 5 * w

    xp = x_prompt.reshape(bp * s, d)
    xs = x_sample.reshape(db * t, d)
    bt, bsamp = bias_tiles(rel_bias_table, past, t, lpad)

    wa_bd = _regroup_blockdiag(l0_gate_a_w)
    wi_bd = _regroup_blockdiag(l0_gate_i_w)
    pool_mk = cache_moba_k.transpose(0, 2, 3, 1).reshape(n_phys, w, PAGE_SIZE)
    pool_mv = cache_moba_v.transpose(0, 2, 3, 1).reshape(n_phys, w, PAGE_SIZE)

    def even(x2, b, tt, h0, conv_state, attend, w_in, w_out, emit):
        z = norm_matmul(x2, l0_norm_g, w_in, emit)
        z, w_in_b = z if emit else (z, None)
        z = z.reshape(b, tt, 5 * w)
        ya, h_last = rglru_branch(z, h0, conv_state, l0_conv_w, l0_conv_b, wa_bd, l0_gate_a_b,
                                  wi_bd, l0_gate_i_b, l0_lambda)
        yb, kn, *v_out = attend(z)
        out = matmul_residual((ya.reshape(b * tt, w), yb.reshape(b * tt, w)), w_out, x2, emit)
        out, w_out_b = out if emit else (out, None)
        new_conv = z[:, tt - (l0_conv_w.shape[0] - 1):, :w]
        v_rows = (v_out[0] if v_out else z[:, :, 4 * w:]).reshape(b, tt, nmh, MOBA_HEAD_DIM)
        return (out, h_last, new_conv, kn.reshape(b, tt, nmh, MOBA_HEAD_DIM), v_rows), (w_in_b, w_out_b)

    (xs, h_s, conv_s, mk_s, mv_s), (w_in0_b, w_out0_b) = even(
        xs, db, t, state_rglru_h, state_rglru_conv,
        lambda z: moba_sample(z, pool_mk, pool_mv, page_table, bsamp, l0_q_norm_g, l0_k_norm_g, w),
        l0_w_in, l0_w_out, True)
    (xp, h_p, conv_p, mk_p, mv_p), _ = even(
        xp, bp, s, jnp.zeros((bp, w), F32), jnp.zeros((bp, l0_conv_w.shape[0] - 1, w), F32),
        lambda z: moba_prompt(z, rel_bias_table, bt, l0_q_norm_g, l0_k_norm_g, w),
        w_in0_b, w_out0_b, False)

    xs, f0_s, ffn0_wb = _conv_ffn(xs, t, state_ffn0_conv, ffn0_norm_g, (ffn0_w_up, ffn0_w_down),
                                  ffn0_conv_w, ffn0_conv_b)
    xp, f0_p, _ = _conv_ffn(xp, s, jnp.zeros((bp, ffn0_conv_w.shape[0] - 1, f2), F32), ffn0_norm_g, ffn0_wb,
                            ffn0_conv_w, ffn0_conv_b)

    wq = nq * DSA_HEAD_DIM
    wkv = nkv * DSA_HEAD_DIM
    wqi = IDX_HEADS * IDX_DIM
    n_in = l1_w_in.shape[1]
    assert n_in == wq + 2 * wkv + wqi + IDX_DIM + IDX_HEADS
    n_pad = -(-n_in // 512) * 512
    w_in1 = jnp.pad(l1_w_in, ((0, 0), (0, n_pad - n_in)))
    k_blk0 = wq // LANES
    v_blk0 = (wq + wkv) // LANES
    qi_off = wq + 2 * wkv
    assert qi_off % wqi == 0 and (qi_off + wqi) % LANES == 0 and (wq + wkv) % wkv == 0
    qi_blk = qi_off // wqi
    kw_blk = (qi_off + wqi) // LANES
    pool_dk = cache_dsa_k.reshape(n_phys * PAGE_SIZE * nkv, DSA_HEAD_DIM)
    pool_dv = cache_dsa_v.reshape(n_phys * PAGE_SIZE * nkv, DSA_HEAD_DIM)
    pool_ik = cache_dsa_idx_k.transpose(0, 2, 1)

    def odd(x2, b, tt, attend, w_in, w_out, emit):
        z2 = norm_matmul(x2, l1_norm_g, w_in, emit)
        z2, w_in_b = z2 if emit else (z2, None)
        kn, knb, vb, vf = kv_prep(z2, l1_k_norm_g, k_blk0, v_blk0, nkv)
        z = z2.reshape(b, tt, n_pad)
        y = attend(z, kn.reshape(b, tt, wkv), knb.reshape(b, tt, wkv), vb.reshape(b, tt, wkv))
        out = matmul_residual((y.reshape(b * tt, wq),), w_out, x2, emit)
        out, w_out_b = out if emit else (out, None)
        v_rows = vf.reshape(b, tt, nkv, DSA_HEAD_DIM)
        ki_rows = z[:, :, qi_off + wqi:qi_off + wqi + IDX_DIM]
        return (out, kn.reshape(b, tt, nkv, DSA_HEAD_DIM), v_rows, ki_rows), (w_in_b, w_out_b)

    def attend_prompt(z, kn, knb, vb):
        isc = index_prompt(z, qi_blk, kw_blk)
        mask = topk_mask(isc.reshape(bp * s, s), min(DSA_TOPK, s // 4), s, 0, True).reshape(bp, s, s)
        return dsa_prompt(z, knb, vb, mask, rel_bias_table, bt, l1_q_norm_g, nq, nkv)

    def attend_sample(z, kn, knb, vb):
        isc = index_sample(z, pool_ik, page_table, qi_blk, kw_blk)
        mask = topk_mask(isc.reshape(db * t, lpad), min(DSA_TOPK, (past + t) // 4), t, past, False)
        return dsa_sample(z, kn, mask.reshape(db, t, lpad), pool_dk, pool_dv, page_table, bsamp,
                          l1_q_norm_g, nq, nkv, (wq + wkv) // wkv)

    (xs, dk_s, dv_s, di_s), (w_in1_b, w_out1_b) = odd(xs, db, t, attend_sample, w_in1, l1_w_out, True)
    (xp, dk_p, dv_p, di_p), _ = odd(xp, bp, s, attend_prompt, w_in1_b, w_out1_b, False)

    xs, f1_s, ffn1_wb = _conv_ffn(xs, t, state_ffn1_conv, ffn1_norm_g, (ffn1_w_up, ffn1_w_down),
                                  ffn1_conv_w, ffn1_conv_b)
    xp, f1_p, _ = _conv_ffn(xp, s, jnp.zeros((bp, ffn1_conv_w.shape[0] - 1, f2), F32), ffn1_norm_g, ffn1_wb,
                            ffn1_conv_w, ffn1_conv_b)

    return (xp.reshape(bp, s, d), xs.reshape(db, t, d), h_p, h_s, conv_p, conv_s, mk_p, mk_s, mv_p, mv_s,
            f0_p, f0_s, dk_p, dk_s, dv_p, dv_s, di_p, di_s, f1_p, f1_s)
```

```python
import functools
import math

import numpy as np
import jax
import jax.numpy as jnp
from jax import lax
from jax.experimental import pallas as pl
from jax.experimental.pallas import tpu as pltpu

F32 = jnp.float32
BF16 = jnp.bfloat16
I32 = jnp.int32
NEG_INF = float("-inf")
INT_MIN = -(2 ** 31)

EPS = 1e-6
LRU_C = 8.0
LRU_GROUP = 256
MOBA_BLOCK = 256
MOBA_TOPK = 3
MOBA_HEAD_DIM = 64
DSA_HEAD_DIM = 128
DSA_TOPK = 256
IDX_DIM = 64
IDX_HEADS = 8
PAGE_SIZE = 128
N_BUCKETS = 32
MAX_DISTANCE = 128
Q_TILE = 256
PAGE_SLOTS = 3
LANES = 128
MIB = 1024 * 1024


def _t5_thresholds():
    exact = N_BUCKETS // 2
    n = np.arange(0, 4 * MAX_DISTANCE)
    nf = np.maximum(n, 1).astype(np.float32)
    large = exact + (np.log(nf / np.float32(exact)) / np.float32(math.log(MAX_DISTANCE / exact))
                     * np.float32(N_BUCKETS - exact)).astype(np.int32)
    bucket = np.where(n < exact, n, np.minimum(large, N_BUCKETS - 1))
    assert np.all(np.diff(bucket) >= 0) and bucket[-1] == N_BUCKETS - 1
    return [int(np.argmax(bucket >= b)) for b in range(N_BUCKETS)]


T5_THR = _t5_thresholds()


def _cp(sem, vmem_mib=48):
    return pltpu.CompilerParams(dimension_semantics=sem, vmem_limit_bytes=vmem_mib * MIB)


def _iota(shape, dim):
    return lax.broadcasted_iota(I32, shape, dim)


def _dot_nt(a, b):
    return lax.dot_general(a, b, (((1,), (1,)), ((), ())), preferred_element_type=F32)


def _dot_nn(a, b):
    return jnp.dot(a, b, preferred_element_type=F32)


def _split2(x):
    hi = x.astype(BF16)
    lo = (x - hi.astype(F32)).astype(BF16)
    return hi, lo


def _split3(x):
    hi = x.astype(BF16)
    r = x - hi.astype(F32)
    mid = r.astype(BF16)
    lo = (r - mid.astype(F32)).astype(BF16)
    return hi, mid, lo


def _dot3(dot, a, b):
    ah, al = _split2(a)
    bh, bl = _split2(b)
    return dot(ah, bh) + (dot(ah, bl) + dot(al, bh))


def _dot_exact01(dot, a, b01):
    h, m, l = _split3(a)
    return dot(h, b01) + (dot(m, b01) + dot(l, b01))


def _t5_bias(dist, tab_ref, h):
    acc = jnp.full(dist.shape, tab_ref[0, h], F32)
    for b in range(1, N_BUCKETS):
        acc = jnp.where(dist >= T5_THR[b], tab_ref[b, h], acc)
    return acc


def _softplus(x):
    return jnp.maximum(x, 0.0) + jnp.log1p(jnp.exp(-jnp.abs(x)))


def _norm_mm_kernel(x_ref, g_ref, w_ref, o_ref, *rest, emit):
    xn_ref = rest[-1]

    @pl.when(pl.program_id(1) == 0)
    def _():
        x = x_ref[...]
        ms = jnp.mean(x * x, axis=-1, keepdims=True)
        xn_ref[...] = (x * lax.rsqrt(ms + EPS) * g_ref[...]).astype(BF16)

    wb = w_ref[...].astype(BF16)
    if emit:
        rest[0][...] = wb
    o_ref[...] = _dot_nn(xn_ref[...], wb)


def norm_matmul(x, g, w, emit=False):
    m, d = x.shape
    n = w.shape[1]
    tm = min(m, 1024)
    tn = 1024 if (not emit and n % 1024 == 0) else 512
    assert m % tm == 0 and n % tn == 0 and (not emit or m == tm)
    out_specs = [pl.BlockSpec((tm, tn), lambda i, j: (i, j))]
    out_shape = [jax.ShapeDtypeStruct((m, n), F32)]
    if emit:
        out_specs.append(pl.BlockSpec((d, tn), lambda i, j: (0, j)))
        out_shape.append(jax.ShapeDtypeStruct((d, n), BF16))
    res = pl.pallas_call(
        functools.partial(_norm_mm_kernel, emit=emit),
        grid=(m // tm, n // tn),
        in_specs=[pl.BlockSpec((tm, d), lambda i, j: (i, 0)),
                  pl.BlockSpec((1, d), lambda i, j: (0, 0)),
                  pl.BlockSpec((d, tn), lambda i, j: (0, j))],
        out_specs=out_specs,
        out_shape=out_shape,
        scratch_shapes=[pltpu.VMEM((tm, d), BF16)],
        compiler_params=_cp(("parallel", "arbitrary")),
        name="norm_matmul",
    )(x, g.reshape(1, d), w)
    return res if emit else res[0]


def _mm_res_kernel(*refs, emit, na):
    a_refs = refs[:na]
    w_ref, r_ref, o_ref = refs[na:na + 3]
    wb = w_ref[...].astype(BF16)
    if emit:
        refs[na + 3][...] = wb
    acc = r_ref[...]
    k0 = 0
    for a_ref in a_refs:
        k1 = k0 + a_ref.shape[1]
        acc = acc + _dot_nn(a_ref[...], wb[k0:k1, :])
        k0 = k1
    o_ref[...] = acc


def matmul_residual(a_parts, w, res, emit=False):
    m = a_parts[0].shape[0]
    kdim, n = w.shape
    tm = min(m, 1024)
    tn = 256 if emit else (1024 if kdim <= 2048 and n % 1024 == 0 else 512)
    assert m % tm == 0 and n % tn == 0 and (not emit or m == tm) and sum(a.shape[1] for a in a_parts) == kdim
    out_specs = [pl.BlockSpec((tm, tn), lambda i, j: (i, j))]
    out_shape = [jax.ShapeDtypeStruct((m, n), F32)]
    if emit:
        out_specs.append(pl.BlockSpec((kdim, tn), lambda i, j: (0, j)))
        out_shape.append(jax.ShapeDtypeStruct((kdim, n), BF16))
    out = pl.pallas_call(
        functools.partial(_mm_res_kernel, emit=emit, na=len(a_parts)),
        grid=(m // tm, n // tn),
        in_specs=[pl.BlockSpec((tm, a.shape[1]), lambda i, j: (i, 0)) for a in a_parts]
                 + [pl.BlockSpec((kdim, tn), lambda i, j: (0, j)),
                    pl.BlockSpec((tm, tn), lambda i, j: (i, j))],
        out_specs=out_specs,
        out_shape=out_shape,
        compiler_params=_cp(("parallel", "arbitrary"), vmem_mib=56),
        name="matmul_residual",
    )(*a_parts, w, res)
    return out if emit else out[0]


def _ffn_up_kernel(x_ref, xp_ref, g_ref, wv_ref, wg_ref, sv_ref, sg_ref, cwv_ref, cwg_ref, cbv_ref, cbg_ref,
                   o_ref, nsv_ref, nsg_ref, *rest, nb, tc, chunks, emit):
    xn_ref, xnp_ref = rest[-2:]
    i = pl.program_id(0)
    tm, tn = o_ref.shape

    def norm(x):
        return x * lax.rsqrt(jnp.mean(x * x, axis=-1, keepdims=True) + EPS) * g_ref[...]

    @pl.when(pl.program_id(1) == 0)
    def _():
        xn_ref[...] = norm(x_ref[...]).astype(BF16)
        if chunks > 1:
            xnp_ref[...] = norm(xp_ref[...])

    def project(w_ref, wb_ref):
        wb = w_ref[...].astype(BF16)
        if emit:
            wb_ref[...] = wb
        u = _dot_nn(xn_ref[...], wb)
        up = _dot_nn(xnp_ref[...].astype(BF16), wb) if (nb == 1 and chunks > 1) else None
        return u, up

    def finish(u, up, s_ref, cw_ref, cb_ref, ns_ref):
        cw = cw_ref[...]
        if nb > 1:
            step = _iota((tm, tn), 0) % tc
            s0 = jnp.broadcast_to(s_ref[:, 0:1, :], (nb, tc, tn)).reshape(tm, tn)
            s1 = jnp.broadcast_to(s_ref[:, 1:2, :], (nb, tc, tn)).reshape(tm, tn)
            u1 = jnp.where(step == 0, s1, pltpu.roll(u, 1, 0))
            u2 = jnp.where(step == 0, s0, jnp.where(step == 1, s1, pltpu.roll(u, 2, 0)))
            ns_ref[...] = u.reshape(nb, tc, tn)[:, tc - 2:tc, :]
            return cb_ref[...] + u2 * cw[0:1] + u1 * cw[1:2] + u * cw[2:3]
        ns_ref[0] = u[tm - 2:tm, :]
        prev = s_ref[0]
        if chunks > 1:
            prev = jnp.where(i % chunks == 0, prev, up[6:8, :])
        u1 = pltpu.roll(u, 1, 0)
        u2 = pltpu.roll(u, 2, 0)
        c = cb_ref[...] + u2 * cw[0:1] + u1 * cw[1:2] + u * cw[2:3]
        row = _iota((8, tn), 0)
        t1 = jnp.where(row == 0, prev[1:2, :], u1[0:8, :])
        t2 = jnp.where(row == 0, prev[0:1, :], jnp.where(row == 1, prev[1:2, :], u2[0:8, :]))
        top = cb_ref[...] + t2 * cw[0:1] + t1 * cw[1:2] + u[0:8, :] * cw[2:3]
        return jnp.concatenate([top, c[8:, :]], axis=0)

    wbv_ref, wbg_ref = (rest[0], rest[1]) if emit else (None, None)
    uv = project(wv_ref, wbv_ref)
    ug = project(wg_ref, wbg_ref)
    val = finish(*uv, sv_ref, cwv_ref, cbv_ref, nsv_ref)
    gate = finish(*ug, sg_ref, cwg_ref, cbg_ref, nsg_ref)
    o_ref[...] = (jax.nn.gelu(gate) * val).astype(o_ref.dtype)


def ffn_up(x, seq_len, state, norm_g, w_val, w_gate, gate_col, conv_w, conv_b, emit=False):
    m, d = x.shape
    f2 = conv_w.shape[1]
    f = f2 // 2
    tn = 256 if emit else 512
    tm = min(m, 1024)
    tc = min(seq_len, tm)
    nb = tm // tc
    chunks = seq_len // tc
    assert m % tm == 0 and tm % tc == 0 and seq_len % tc == 0 and tc % 8 == 0 and (nb == 1 or chunks == 1)
    assert f % tn == 0 and gate_col % tn == 0 and conv_w.shape[0] == 3 and (not emit or m == tm)
    ncb = f // tn
    goff = gate_col // tn
    r8 = tm // 8
    nseq = m // seq_len

    def wspec(rows, off):
        return pl.BlockSpec((rows, tn), lambda i, j: (0, j + off))

    def st(off):
        return pl.BlockSpec((nb, 2, tn), lambda i, j: (i // chunks, 0, j + off))

    tail = pl.BlockSpec((nb, 2, tn), lambda i, j: (i, 0, j))
    out_specs = [pl.BlockSpec((tm, tn), lambda i, j: (i, j)), tail, tail]
    out_shape = [jax.ShapeDtypeStruct((m, f), BF16), jax.ShapeDtypeStruct((nseq * chunks, 2, f), F32),
                 jax.ShapeDtypeStruct((nseq * chunks, 2, f), F32)]
    if emit:
        out_specs += [wspec(d, 0), wspec(d, 0)]
        out_shape += [jax.ShapeDtypeStruct((d, f), BF16), jax.ShapeDtypeStruct((d, f), BF16)]
    return pl.pallas_call(
        functools.partial(_ffn_up_kernel, nb=nb, tc=tc, chunks=chunks, emit=emit),
        grid=(m // tm, ncb),
        in_specs=[pl.BlockSpec((tm, d), lambda i, j: (i, 0)),
                  pl.BlockSpec((8, d), lambda i, j: (jnp.maximum(i * r8 - 1, 0), 0)),
                  pl.BlockSpec((1, d), lambda i, j: (0, 0)),
                  wspec(d, 0), wspec(d, goff), st(0), st(ncb),
                  wspec(3, 0), wspec(3, ncb), wspec(1, 0), wspec(1, ncb)],
        out_specs=out_specs,
        out_shape=out_shape,
        scratch_shapes=[pltpu.VMEM((tm, d), BF16), pltpu.VMEM((8, d), F32)],
        compiler_params=_cp(("parallel", "arbitrary"), vmem_mib=56),
        name="ffn_up",
    )(x, x, norm_g.reshape(1, d), w_val, w_gate, state, state, conv_w, conv_w,
      conv_b.reshape(1, f2), conv_b.reshape(1, f2))


def _rglru_kernel(xa_ref, ga_ref, pv_ref, cs_ref, h0_ref, cw_ref, cb_ref, wa_ref, ba_ref,
                  wi_ref, bi_ref, lam_ref, ya_ref, hl_ref, e_ref, hc_ref, *, tc, bb):
    t = pl.program_id(1)
    w = xa_ref.shape[-1]
    rows = bb * tc

    @pl.when(t == 0)
    def _():
        hc_ref[...] = h0_ref[...]
        e_ref[:, 5:8, :] = cs_ref[...]

    @pl.when(t > 0)
    def _():
        e_ref[:, 5:8, :] = pv_ref[:, 5:8, :]

    e_ref[:, 8:8 + tc, :] = xa_ref[...]
    cw = cw_ref[...]
    xc = (cb_ref[...] + e_ref[:, 5:5 + tc, :] * cw[0:1] + e_ref[:, 6:6 + tc, :] * cw[1:2]
          + e_ref[:, 7:7 + tc, :] * cw[2:3] + e_ref[:, 8:8 + tc, :] * cw[3:4]).reshape(rows, w)
    xb = xc.astype(BF16)

    def gate_logits(w_ref):
        return jnp.concatenate([_dot_nn(xb[:, g * LRU_GROUP:(g + 1) * LRU_GROUP], w_ref[g])
                                for g in range(w // LRU_GROUP)], axis=1)

    logits_a = gate_logits(wa_ref)
    logits_i = gate_logits(wi_ref)
    r = jax.nn.sigmoid(logits_a + ba_ref[...])
    i = jax.nn.sigmoid(logits_i + bi_ref[...])
    log_a = -LRU_C * r * _softplus(-lam_ref[...])
    a = jnp.exp(log_a)
    u = jnp.sqrt(-jnp.tanh(log_a) * (1.0 + a * a)) * (i * xc)

    step = _iota((rows, w), 0) % tc
    s = 1
    while s < tc:
        a_sh = jnp.where(step >= s, pltpu.roll(a, s, 0), 1.0)
        u_sh = jnp.where(step >= s, pltpu.roll(u, s, 0), 0.0)
        u = u + a * u_sh
        a = a * a_sh
        s *= 2
    h = u.reshape(bb, tc, w) + a.reshape(bb, tc, w) * hc_ref[...]
    hlast = h[:, tc - 1:tc, :]
    hc_ref[...] = hlast
    hl_ref[...] = hlast
    ya_ref[...] = (h * jax.nn.gelu(ga_ref[...])).astype(ya_ref.dtype)


def rglru_branch(z, h0, conv_state, conv_w, conv_b, wa_bd, ba, wi_bd, bi, lam):
    b, t, _ = z.shape
    w = h0.shape[-1]
    tc = min(t, 256)
    bb = 8 if (tc == t and tc == 8 and b % 8 == 0) else 1
    assert t % tc == 0 and tc % 8 == 0 and conv_w.shape[0] == 4 and w % LRU_GROUP == 0
    r8 = tc // 8
    ng = w // LRU_GROUP
    vec = lambda: pl.BlockSpec((1, w), lambda i, j: (0, 0))
    ya, hl = pl.pallas_call(
        functools.partial(_rglru_kernel, tc=tc, bb=bb),
        grid=(b // bb, t // tc),
        in_specs=[pl.BlockSpec((bb, tc, w), lambda i, j: (i, j, 0)),
                  pl.BlockSpec((bb, tc, w), lambda i, j: (i, j, 1)),
                  pl.BlockSpec((bb, 8, w), lambda i, j: (i, jnp.maximum(j * r8 - 1, 0), 0)),
                  pl.BlockSpec((bb, 3, w), lambda i, j: (i, 0, 0)),
                  pl.BlockSpec((bb, 1, w), lambda i, j: (i, 0, 0)),
                  pl.BlockSpec((4, w), lambda i, j: (0, 0)),
                  vec(),
                  pl.BlockSpec((ng, LRU_GROUP, LRU_GROUP), lambda i, j: (0, 0, 0)),
                  vec(),
                  pl.BlockSpec((ng, LRU_GROUP, LRU_GROUP), lambda i, j: (0, 0, 0)),
                  vec(), vec()],
        out_specs=[pl.BlockSpec((bb, tc, w), lambda i, j: (i, j, 0)),
                   pl.BlockSpec((bb, 1, w), lambda i, j: (i, 0, 0))],
        out_shape=[jax.ShapeDtypeStruct((b, t, w), BF16), jax.ShapeDtypeStruct((b, 1, w), F32)],
        scratch_shapes=[pltpu.VMEM((bb, tc + 8, w), F32), pltpu.VMEM((bb, 1, w), F32)],
        compiler_params=_cp(("parallel", "arbitrary")),
        name="rglru",
    )(z, z, z, conv_state, h0.reshape(b, 1, w), conv_w, conv_b.reshape(1, w), wa_bd,
      ba.reshape(1, w), wi_bd, bi.reshape(1, w), lam.reshape(1, w))
    return ya, hl.reshape(b, w)


def _regroup_blockdiag(wb):
    n, c, _ = wb.shape
    per = LRU_GROUP // c
    g = n // per
    eye = jnp.eye(per, dtype=wb.dtype)
    out = jnp.einsum("gacd,ab->gacbd", wb.reshape(g, per, c, c), eye)
    return out.reshape(g, LRU_GROUP, LRU_GROUP).astype(BF16)


def _bias_kernel(tab_ref, bt_ref, bs_ref, *, past, lpad):
    h = pl.program_id(0)
    d0 = _iota((Q_TILE, Q_TILE), 0) - _iota((Q_TILE, Q_TILE), 1)
    bt_ref[0, 0] = _t5_bias(d0, tab_ref, h)
    bt_ref[0, 1] = _t5_bias(d0 + Q_TILE, tab_ref, h)
    t = bs_ref.shape[1]
    ds = past + _iota((t, lpad), 0) - _iota((t, lpad), 1)
    bs_ref[0] = _t5_bias(ds, tab_ref, h)


def bias_tiles(table, past, t_new, lpad):
    nh = table.shape[1]
    return pl.pallas_call(
        functools.partial(_bias_kernel, past=past, lpad=lpad),
        grid=(nh,),
        in_specs=[pl.BlockSpec(memory_space=pltpu.SMEM)],
        out_specs=[pl.BlockSpec((1, 2, Q_TILE, Q_TILE), lambda h: (h, 0, 0, 0)),
                   pl.BlockSpec((1, t_new, lpad), lambda h: (h, 0, 0))],
        out_shape=[jax.ShapeDtypeStruct((nh, 2, Q_TILE, Q_TILE), F32),
                   jax.ShapeDtypeStruct((nh, t_new, lpad), F32)],
        compiler_params=_cp(("arbitrary",)),
        name="t5_bias_tiles",
    )(table)


def _top_lanes(g, nvalid, k):
    lane = _iota(g.shape, 1).astype(F32)
    gg = jnp.where(lane < nvalid, g, NEG_INF)
    sel = jnp.zeros(g.shape, F32)
    for _ in range(k):
        mx = jnp.max(gg, axis=-1, keepdims=True)
        first = jnp.min(jnp.where(gg == mx, lane, float(g.shape[1])), axis=-1, keepdims=True)
        pick = (lane == first) & (mx > NEG_INF)
        sel = jnp.where(pick, 1.0, sel)
        gg = jnp.where(pick, NEG_INF, gg)
    return sel


def _moba_prompt_kernel(tab_ref, q_ref, k_ref, v_ref, qg_ref, kg_ref, bt_ref, y_ref, kn_ref, vo_ref, *, s):
    hp = pl.program_id(0)
    nb = s // MOBA_BLOCK
    dh = MOBA_HEAD_DIM
    scale = dh ** -0.5
    lane = _iota((1, LANES), 1)
    low = lane < dh

    def headnorm(x, g):
        x2 = x * x
        s0 = jnp.sum(jnp.where(low, x2, 0.0), axis=-1, keepdims=True)
        s1 = jnp.sum(jnp.where(low, 0.0, x2), axis=-1, keepdims=True)
        ms = jnp.where(low, s0, s1) * (1.0 / dh)
        return x * lax.rsqrt(ms + EPS) * g

    qn = headnorm(q_ref[0], qg_ref[...])
    kn = headnorm(k_ref[0], kg_ref[...])
    kn_ref[0] = kn
    vo_ref[0] = v_ref[0]
    kb = kn.astype(BF16)
    vb = v_ref[0].astype(BF16)
    blk_row = _iota((LANES, LANES), 0)
    kmat = jnp.zeros((LANES, LANES), F32)
    for j in range(nb):
        kmean = jnp.mean(kn[j * MOBA_BLOCK:(j + 1) * MOBA_BLOCK], axis=0, keepdims=True)
        kmat = jnp.where(blk_row == j, kmean, kmat)
    qi = _iota((Q_TILE, Q_TILE), 0)
    kj = _iota((Q_TILE, Q_TILE), 1)
    causal = qi >= kj

    near = []
    for hh in range(2):
        t31 = tab_ref[N_BUCKETS - 1, 2 * hp + hh]
        near.append((jnp.where(causal, bt_ref[hh, 0] - t31, NEG_INF), bt_ref[hh, 1] - t31))

    work = [(c, hh) for c in range(nb) for hh in range(2)]
    scs, sels = {}, {}
    for c, hh in work:
        hmask = low if hh == 0 else jnp.logical_not(low)
        qm = jnp.where(hmask, qn[c * Q_TILE:(c + 1) * Q_TILE], 0.0)
        scs[c, hh] = _dot_nt((qm * scale).astype(BF16), kb[0:(c + 1) * MOBA_BLOCK])
        if c > MOBA_TOPK:
            sels[c, hh] = _top_lanes(_dot3(_dot_nt, qm, kmat), c, MOBA_TOPK)
    probs, sums = {}, {}
    for c, hh in work:
        pieces = []
        for j in range(c + 1):
            sj = scs[c, hh][:, j * MOBA_BLOCK:(j + 1) * MOBA_BLOCK]
            if j == c:
                sj = sj + near[hh][0]
            else:
                if j == c - 1:
                    sj = sj + near[hh][1]
                if c > MOBA_TOPK:
                    sj = jnp.where(sels[c, hh][:, j:j + 1] > 0.0, sj, NEG_INF)
            pieces.append(sj)
        srow = jnp.concatenate(pieces, axis=1) if c else pieces[0]
        mx = jnp.max(srow, axis=-1, keepdims=True)
        p = jnp.exp(srow - mx)
        sums[c, hh] = jnp.sum(p, axis=-1, keepdims=True)
        probs[c, hh] = p.astype(BF16)
    for c in sorted({c for c, _ in work}):
        outs = [_dot_nn(probs[c, hh], vb[0:(c + 1) * MOBA_BLOCK]) / sums[c, hh] for hh in range(2)]
        y_ref[0, c * Q_TILE:(c + 1) * Q_TILE, :] = jnp.where(low, outs[0], outs[1]).astype(y_ref.dtype)


def moba_prompt(z, table, bt, q_gain, k_gain, w):
    b, s, _ = z.shape
    assert s % MOBA_BLOCK == 0 and Q_TILE == MOBA_BLOCK and s // MOBA_BLOCK <= LANES
    nhp = w // LANES
    g2 = lambda g: jnp.tile(g, LANES // MOBA_HEAD_DIM).reshape(1, LANES)
    col = lambda off: pl.BlockSpec((1, s, LANES), lambda hp, i: (i, 0, off * nhp + hp))
    gspec = pl.BlockSpec((1, LANES), lambda hp, i: (0, 0))
    return pl.pallas_call(
        functools.partial(_moba_prompt_kernel, s=s),
        grid=(nhp, b),
        in_specs=[pl.BlockSpec(memory_space=pltpu.SMEM), col(2), col(3), col(4), gspec, gspec,
                  pl.BlockSpec((2, 2, Q_TILE, Q_TILE), lambda hp, i: (hp, 0, 0, 0))],
        out_specs=[pl.BlockSpec((1, s, LANES), lambda hp, i: (i, 0, hp))] * 3,
        out_shape=[jax.ShapeDtypeStruct((b, s, w), BF16), jax.ShapeDtypeStruct((b, s, w), F32),
                   jax.ShapeDtypeStruct((b, s, w), F32)],
        compiler_params=_cp(("parallel", "parallel")),
        name="moba_prompt",
    )(table, z, z, z, g2(q_gain), g2(k_gain), bt)


def _group_onehot(ngroups_pad, width, gsize):
    return jnp.where(_iota((ngroups_pad, width), 0) == _iota((ngroups_pad, width), 1) // gsize,
                     1.0, 0.0).astype(BF16)


def _group_rmsnorm(x, gain, gsize):
    onehot = _group_onehot(LANES, x.shape[1], gsize)
    ssq = _dot_exact01(_dot_nt, x * x, onehot)
    rinv = lax.rsqrt(ssq * (1.0 / gsize) + EPS)
    return x * _dot_exact01(_dot_nn, rinv, onehot) * gain


def _moba_sample_kernel(pt_ref, q_ref, k_ref, v_ref, qg_ref, kg_ref, bs_ref, *rest, npages, t):
    kp_refs = rest[:npages]
    vp_refs = rest[npages:2 * npages]
    y_ref, kn_ref = rest[2 * npages:]
    w = q_ref.shape[-1]
    dh = MOBA_HEAD_DIM
    nh = w // dh
    rows = nh * t
    ppb = MOBA_BLOCK // PAGE_SIZE
    nbp = npages // ppb
    past = npages * PAGE_SIZE
    scale = dh ** -0.5

    qn = _group_rmsnorm(q_ref[0], qg_ref[...], dh)
    kn = _group_rmsnorm(k_ref[0], kg_ref[...], dh)
    kn_ref[0] = kn
    own = _iota((rows, w), 0) // t == _iota((rows, w), 1) // dh
    qbd = jnp.where(own, jnp.concatenate([qn] * nh, axis=0), 0.0)
    qbb = qbd.astype(BF16)
    pad_rows = lambda x: jnp.concatenate([x, jnp.zeros((LANES - t, w), x.dtype)], axis=0)

    pieces = [_dot_nn(qbb, kp_refs[p][0].astype(BF16)) for p in range(npages)]
    pieces.append(_dot_nt(qbb, pad_rows(kn).astype(BF16)))
    sums = [jnp.sum(sum(kp_refs[p][0] for p in range(j * ppb, (j + 1) * ppb)), axis=1, keepdims=True)
            for j in range(nbp)]
    lpad = past + LANES
    srow = jnp.concatenate(pieces, axis=1) * scale + bs_ref[...]

    blk_lane = _iota((w, LANES), 1)
    kmat = jnp.zeros((w, LANES), F32)
    for j in range(nbp):
        kmat = jnp.where(blk_lane == j, sums[j] * (1.0 / MOBA_BLOCK), kmat)
    sel = _top_lanes(_dot3(_dot_nn, qbd, kmat), nbp, MOBA_TOPK)
    col = _iota((rows, lpad), 1)
    jrow = _iota((rows, lpad), 0) % t
    keep = (col >= past) & (col - past <= jrow)
    for j in range(nbp):
        inblk = (col >= j * MOBA_BLOCK) & (col < (j + 1) * MOBA_BLOCK)
        keep = keep | (inblk & (sel[:, j:j + 1] > 0.0))
    srow = jnp.where(keep, srow, NEG_INF)
    mx = jnp.max(srow, axis=-1, keepdims=True)
    pr = jnp.exp(srow - mx)
    l = jnp.sum(pr, axis=-1, keepdims=True)
    pb = pr.astype(BF16)
    acc = _dot_nn(pb[:, past:], pad_rows(v_ref[0]).astype(BF16))
    for p in range(npages):
        acc = acc + _dot_nt(pb[:, p * PAGE_SIZE:(p + 1) * PAGE_SIZE], vp_refs[p][0].astype(BF16))
    acc = acc / l
    lane_head = _iota((t, w), 1) // dh
    y = jnp.zeros((t, w), F32)
    for h in range(nh):
        y = jnp.where(lane_head == h, acc[h * t:(h + 1) * t, :], y)
    y_ref[0] = y.astype(y_ref.dtype)


def moba_sample(z, pool_k, pool_v, page_table, bs, q_gain, k_gain, w):
    db, t, _ = z.shape
    npages = page_table.shape[1]
    nh = w // MOBA_HEAD_DIM
    assert t == 8 and nh * t == LANES and npages % (MOBA_BLOCK // PAGE_SIZE) == 0
    lpad = npages * PAGE_SIZE + LANES
    gt = lambda g: jnp.tile(g, nh).reshape(1, w)
    col = lambda off: pl.BlockSpec((1, t, w), lambda i, pt: (i, 0, off))
    gspec = pl.BlockSpec((1, w), lambda i, pt: (0, 0))
    page = lambda p: pl.BlockSpec((1, w, PAGE_SIZE), lambda i, pt: (pt[i, p], 0, 0))
    grid_spec = pltpu.PrefetchScalarGridSpec(
        num_scalar_prefetch=1,
        grid=(db,),
        in_specs=[col(2), col(3), col(4), gspec, gspec,
                  pl.BlockSpec((nh * t, lpad), lambda i, pt: (0, 0))]
                 + [page(p) for p in range(npages)] * 2,
        out_specs=[pl.BlockSpec((1, t, w), lambda i, pt: (i, 0, 0)),
                   pl.BlockSpec((1, t, w), lambda i, pt: (i, 0, 0))],
    )
    return pl.pallas_call(
        functools.partial(_moba_sample_kernel, npages=npages, t=t),
        grid_spec=grid_spec,
        out_shape=[jax.ShapeDtypeStruct((db, t, w), BF16), jax.ShapeDtypeStruct((db, t, w), F32)],
        compiler_params=_cp(("arbitrary",), vmem_mib=56),
        name="moba_sample",
    )(page_table, z, z, z, gt(q_gain), gt(k_gain), bs.reshape(nh * t, lpad),
      *([pool_k] * npages), *([pool_v] * npages))


def _kv_prep_kernel(k_ref, v_ref, g_ref, kn_ref, knb_ref, vb_ref, vf_ref):
    x = k_ref[...]
    ms = jnp.mean(x * x, axis=-1, keepdims=True)
    kn = x * lax.rsqrt(ms + EPS) * g_ref[...]
    kn_ref[...] = kn
    knb_ref[...] = kn.astype(BF16)
    vb_ref[...] = v_ref[...].astype(BF16)
    vf_ref[...] = v_ref[...]


def kv_prep(z, gain, k_blk0, v_blk0, nheads):
    m = z.shape[0]
    tm = min(m, 1024)
    assert m % tm == 0
    out = pl.BlockSpec((tm, LANES), lambda i, h: (i, h))
    return pl.pallas_call(
        _kv_prep_kernel,
        grid=(m // tm, nheads),
        in_specs=[pl.BlockSpec((tm, LANES), lambda i, h: (i, k_blk0 + h)),
                  pl.BlockSpec((tm, LANES), lambda i, h: (i, v_blk0 + h)),
                  pl.BlockSpec((1, LANES), lambda i, h: (0, 0))],
        out_specs=[out, out, out, out],
        out_shape=[jax.ShapeDtypeStruct((m, nheads * LANES), F32),
                   jax.ShapeDtypeStruct((m, nheads * LANES), BF16),
                   jax.ShapeDtypeStruct((m, nheads * LANES), BF16),
                   jax.ShapeDtypeStruct((m, nheads * LANES), F32)],
        compiler_params=_cp(("parallel", "parallel")),
        name="kv_prep",
    )(z, z, gain.reshape(1, LANES))


def _index_prompt_kernel(qi_ref, qw_ref, kk_ref, o_ref, *, s):
    c = pl.program_id(1)
    lane = _iota((1, LANES), 1)
    low = lane < IDX_DIM
    qw = qw_ref[0]

    def split(x):
        hi = x.astype(BF16).astype(F32)
        return hi, (x - hi).astype(BF16).astype(F32)

    def tile(cc):
        nk = (cc + 1) * Q_TILE
        kh, kl = split(jnp.where(low, kk_ref[0, 0:nk, :], 0.0))
        rhs = jnp.concatenate([kh + pltpu.roll(kh, IDX_DIM, 1), kl], axis=1).astype(BF16)
        scs = []
        for pair in range(IDX_HEADS // 2):
            qh, ql = split(qi_ref[0, :, pair * LANES:(pair + 1) * LANES])
            qh_sw = pltpu.roll(qh, IDX_DIM, 1)
            ql_sw = pltpu.roll(ql, IDX_DIM, 1)
            for hi_low, lo_high in ((qh, ql_sw), (qh_sw, ql)):
                lhs = jnp.concatenate([jnp.where(low, hi_low, lo_high), jnp.where(low, hi_low, 0.0)],
                                      axis=1).astype(BF16)
                scs.append(_dot_nt(lhs, rhs))
        acc = jnp.zeros((Q_TILE, nk), F32)
        for ih, sc in enumerate(scs):
            wcol = qw[:, IDX_DIM + ih:IDX_DIM + ih + 1] * (IDX_HEADS ** -0.5)
            acc = acc + wcol * jnp.maximum(sc * (IDX_DIM ** -0.5), 0.0)
        qpos = cc * Q_TILE + _iota((Q_TILE, nk), 0)
        o_ref[0, :, 0:nk] = jnp.where(_iota((Q_TILE, nk), 1) <= qpos, acc, NEG_INF)
        if nk < s:
            o_ref[0, :, nk:] = jnp.full((Q_TILE, s - nk), NEG_INF, F32)

    for cc in range(s // Q_TILE):
        pl.when(c == cc)(functools.partial(tile, cc))


def index_prompt(z, qi_blk, kw_blk):
    b, s, _ = z.shape
    assert s % Q_TILE == 0
    return pl.pallas_call(
        functools.partial(_index_prompt_kernel, s=s),
        grid=(b, s // Q_TILE),
        in_specs=[pl.BlockSpec((1, Q_TILE, IDX_HEADS * IDX_DIM), lambda i, c: (i, c, qi_blk)),
                  pl.BlockSpec((1, Q_TILE, LANES), lambda i, c: (i, c, kw_blk)),
                  pl.BlockSpec((1, s, LANES), lambda i, c: (i, 0, kw_blk))],
        out_specs=pl.BlockSpec((1, Q_TILE, s), lambda i, c: (i, c, 0)),
        out_shape=jax.ShapeDtypeStruct((b, s, s), F32),
        compiler_params=_cp(("parallel", "parallel")),
        name="index_prompt",
    )(z, z, z)


def _index_sample_kernel(pt_ref, qi_ref, qw_ref, *rest, npages, t):
    kp_refs = rest[:npages]
    o_ref = rest[npages]
    qi = qi_ref[0]
    qw = qw_ref[0]
    q2 = jnp.concatenate([qi[:, ih * IDX_DIM:(ih + 1) * IDX_DIM] for ih in range(IDX_HEADS)], axis=0)
    wcol = jnp.concatenate([qw[:, IDX_DIM + ih:IDX_DIM + ih + 1] for ih in range(IDX_HEADS)], axis=0)
    wcol = wcol * (IDX_HEADS ** -0.5)
    knew = jnp.concatenate([qw[:, 0:IDX_DIM], jnp.zeros((LANES - t, IDX_DIM), F32)], axis=0)

    def scores(dot, keys):
        sc = wcol * jnp.maximum(_dot3(dot, q2, keys) * (IDX_DIM ** -0.5), 0.0)
        return sum(sc[ih * t:(ih + 1) * t, :] for ih in range(IDX_HEADS))

    pieces = [scores(_dot_nn, kp_refs[p][0]) for p in range(npages)]
    new = scores(_dot_nt, knew)
    jj = _iota((t, LANES), 0)
    cc = _iota((t, LANES), 1)
    pieces.append(jnp.where(cc <= jj, new, NEG_INF))
    o_ref[0] = jnp.concatenate(pieces, axis=1)


def index_sample(z, pool_ik, page_table, qi_blk, kw_blk):
    db, t, _ = z.shape
    npages = page_table.shape[1]
    lpad = npages * PAGE_SIZE + LANES
    grid_spec = pltpu.PrefetchScalarGridSpec(
        num_scalar_prefetch=1,
        grid=(db,),
        in_specs=[pl.BlockSpec((1, t, IDX_HEADS * IDX_DIM), lambda i, pt: (i, 0, qi_blk)),
                  pl.BlockSpec((1, t, LANES), lambda i, pt: (i, 0, kw_blk))]
                 + [pl.BlockSpec((1, IDX_DIM, PAGE_SIZE), (lambda i, pt, p=p: (pt[i, p], 0, 0)))
                    for p in range(npages)],
        out_specs=pl.BlockSpec((1, t, lpad), lambda i, pt: (i, 0, 0)),
    )
    return pl.pallas_call(
        functools.partial(_index_sample_kernel, npages=npages, t=t),
        grid_spec=grid_spec,
        out_shape=jax.ShapeDtypeStruct((db, t, lpad), F32),
        compiler_params=_cp(("arbitrary",)),
        name="index_sample",
    )(page_table, z, z, *([pool_ik] * npages))


def _topk_mask_kernel(x_ref, o_ref, *, topk, period, base, tr, additive, ncausal):
    l = x_ref.shape[1]
    on, off = (0.0, NEG_INF) if additive else (1.0, 0.0)

    def body(nk):
        rowg = pl.program_id(0) * tr + _iota((tr, nk), 0)
        valid = _iota((tr, nk), 1) <= base + rowg % period
        bits = pltpu.bitcast(x_ref[:, 0:nk], I32)
        key = jnp.where(bits < 0, bits ^ 0x7FFFFFFF, bits)
        key = jnp.where(valid, key, INT_MIN)

        def step(i, lo):
            cand = lo + lax.shift_left(jnp.int32(1), 31 - i)
            cnt = jnp.sum(jnp.where(key >= cand, 1.0, 0.0), axis=-1, keepdims=True)
            return jnp.where(cnt >= topk, cand, lo)

        thr = lax.fori_loop(0, 32, step, jnp.full((tr, 1), INT_MIN, I32))
        gt = key > thr
        need = topk - jnp.sum(jnp.where(gt, 1.0, 0.0), axis=-1, keepdims=True)
        eq = jnp.where(key == thr, 1.0, 0.0)
        upper = jnp.where(_iota((LANES, LANES), 0) < _iota((LANES, LANES), 1), 1.0, 0.0).astype(BF16)
        before = jnp.zeros((tr, 1), F32)
        for j in range(nk // LANES):
            e = eq[:, j * LANES:(j + 1) * LANES]
            rank = before + _dot_nn(e.astype(BF16), upper)
            pick = gt[:, j * LANES:(j + 1) * LANES] | ((e > 0.0) & (rank < need))
            pick = pick & valid[:, j * LANES:(j + 1) * LANES]
            o_ref[:, j * LANES:(j + 1) * LANES] = jnp.where(pick, on, off)
            before = before + jnp.sum(e, axis=-1, keepdims=True)
        if nk < l:
            o_ref[:, nk:] = jnp.full((tr, l - nk), off, F32)

    if ncausal > 1:
        c = pl.program_id(0) % ncausal
        for cc in range(ncausal):
            pl.when(c == cc)(functools.partial(body, (cc + 1) * tr))
    else:
        body(l)


def topk_mask(scores, topk, period, base, additive):
    r, l = scores.shape
    tr = min(r, 256)
    assert r % tr == 0 and l % LANES == 0
    ncausal = period // tr if (base == 0 and period == l and period % tr == 0) else 1
    return pl.pallas_call(
        functools.partial(_topk_mask_kernel, topk=topk, period=period, base=base, tr=tr, additive=additive,
                          ncausal=ncausal),
        grid=(r // tr,),
        in_specs=[pl.BlockSpec((tr, l), lambda i: (i, 0))],
        out_specs=pl.BlockSpec((tr, l), lambda i: (i, 0)),
        out_shape=jax.ShapeDtypeStruct((r, l), F32),
        compiler_params=_cp(("parallel",)),
        name="topk_mask",
    )(scores)


def _dsa_prompt_kernel(tab_ref, q_ref, k_ref, v_ref, m_ref, bt_ref, qg_ref, y_ref, *, s):
    c = pl.program_id(1)
    g = pl.program_id(2)
    dh = DSA_HEAD_DIM
    nh = q_ref.shape[-1] // dh
    qbs, t31s = [], []
    for hh in range(nh):
        q = q_ref[0, :, hh * dh:(hh + 1) * dh]
        qn = q * lax.rsqrt(jnp.mean(q * q, axis=-1, keepdims=True) + EPS) * qg_ref[...]
        qbs.append((qn * (dh ** -0.5)).astype(BF16))
        t31s.append(tab_ref[N_BUCKETS - 1, g * nh + hh])

    def tile(cc):
        nk = (cc + 1) * Q_TILE
        madd = m_ref[0, :, 0:nk]
        scs = [_dot_nt(qbs[hh], k_ref[0, 0:nk, :]) for hh in range(nh)]
        probs, sums = [], []
        for hh in range(nh):
            sc = scs[hh] + madd
            pieces = [sc[:, 0:(cc - 1) * Q_TILE]] if cc > 1 else []
            if cc > 0:
                pieces.append(sc[:, (cc - 1) * Q_TILE:cc * Q_TILE] + (bt_ref[hh, 1] - t31s[hh]))
            pieces.append(sc[:, cc * Q_TILE:nk] + (bt_ref[hh, 0] - t31s[hh]))
            srow = jnp.concatenate(pieces, axis=1) if cc else pieces[0]
            mx = jnp.max(srow, axis=-1, keepdims=True)
            p = jnp.exp(srow - mx)
            sums.append(jnp.sum(p, axis=-1, keepdims=True))
            probs.append(p.astype(BF16))
        for hh in range(nh):
            y_ref[0, :, hh * dh:(hh + 1) * dh] = (_dot_nn(probs[hh], v_ref[0, 0:nk, :]) / sums[hh]
                                                  ).astype(y_ref.dtype)

    for cc in range(s // Q_TILE):
        pl.when(c == cc)(functools.partial(tile, cc))


def dsa_prompt(z, knb, vb, madd, table, bt, q_gain, nq, nkv):
    b, s, _ = z.shape
    nh = nq // nkv
    return pl.pallas_call(
        functools.partial(_dsa_prompt_kernel, s=s),
        grid=(b, s // Q_TILE, nkv),
        in_specs=[pl.BlockSpec(memory_space=pltpu.SMEM),
                  pl.BlockSpec((1, Q_TILE, nh * LANES), lambda i, c, g: (i, c, g)),
                  pl.BlockSpec((1, s, LANES), lambda i, c, g: (i, 0, g)),
                  pl.BlockSpec((1, s, LANES), lambda i, c, g: (i, 0, g)),
                  pl.BlockSpec((1, Q_TILE, s), lambda i, c, g: (i, c, 0)),
                  pl.BlockSpec((nh, 2, Q_TILE, Q_TILE), lambda i, c, g: (g, 0, 0, 0)),
                  pl.BlockSpec((1, LANES), lambda i, c, g: (0, 0))],
        out_specs=pl.BlockSpec((1, Q_TILE, nh * LANES), lambda i, c, g: (i, c, g)),
        out_shape=jax.ShapeDtypeStruct((b, s, nq * LANES), BF16),
        compiler_params=_cp(("parallel", "parallel", "arbitrary")),
        name="dsa_prompt",
    )(table, z, knb, vb, madd, bt, q_gain.reshape(1, LANES))


def _dsa_sample_kernel(pt_ref, q_ref, kn_ref, v_ref, m_ref, bs_ref, qg_ref, pk_ref, pv_ref, y_ref,
                       kbuf, vbuf, sems, *, npages, t, nq, nkv):
    i = pl.program_id(0)
    nseq = pl.num_programs(0)
    prows = PAGE_SIZE * nkv

    def page_copies(seq, slot):
        cps = []
        for p in range(npages):
            src = pl.ds(pt_ref[seq, p] * prows, prows)
            cps.append(pltpu.make_async_copy(pk_ref.at[src, :], kbuf.at[slot, p], sems.at[slot, 0]))
            cps.append(pltpu.make_async_copy(pv_ref.at[src, :], vbuf.at[slot, p], sems.at[slot, 1]))
        return cps

    @pl.when(i == 0)
    def _():
        for s in range(PAGE_SLOTS - 1):
            for cp in page_copies(s, s):
                cp.start()

    ahead = i + (PAGE_SLOTS - 1)

    @pl.when(ahead < nseq)
    def _():
        for cp in page_copies(ahead, ahead % PAGE_SLOTS):
            cp.start()

    slot = i % PAGE_SLOTS
    for cp in page_copies(i, slot):
        cp.wait()
    kp_refs = [kbuf.at[slot, p] for p in range(npages)]
    vp_refs = [vbuf.at[slot, p] for p in range(npages)]
    dh = DSA_HEAD_DIM
    grp = nq // nkv
    rows = grp * t
    past = npages * PAGE_SIZE
    scale = dh ** -0.5
    q = q_ref[0]
    gain = qg_ref[...]
    keep = jnp.concatenate([m_ref[0]] * grp, axis=0) > 0.5
    pad_rows = lambda x: jnp.concatenate([x, jnp.zeros((LANES - t, dh), x.dtype)], axis=0)

    scores = []
    for g in range(nkv):
        qs = []
        for h in range(g * grp, (g + 1) * grp):
            qh = q[:, h * dh:(h + 1) * dh]
            qs.append(qh * lax.rsqrt(jnp.mean(qh * qh, axis=-1, keepdims=True) + EPS) * gain)
        qg = (jnp.concatenate(qs, axis=0) * scale).astype(BF16)
        pieces = [_dot_nt(qg, kp_refs[p][pl.ds(g, PAGE_SIZE, stride=nkv), :].astype(BF16))
                  for p in range(npages)]
        pieces.append(_dot_nt(qg, pad_rows(kn_ref[0, :, g * dh:(g + 1) * dh]).astype(BF16)))
        scores.append(jnp.concatenate(pieces, axis=1))
    probs = []
    for g in range(nkv):
        srow = jnp.where(keep, scores[g] + bs_ref[g * rows:(g + 1) * rows, :], NEG_INF)
        mx = jnp.max(srow, axis=-1, keepdims=True)
        pr = jnp.exp(srow - mx)
        probs.append((pr.astype(BF16), jnp.sum(pr, axis=-1, keepdims=True)))
    outs = []
    for g in range(nkv):
        pb, l = probs[g]
        acc = _dot_nn(pb[:, past:], pad_rows(v_ref[0, :, g * dh:(g + 1) * dh]).astype(BF16))
        for p in range(npages):
            acc = acc + _dot_nn(pb[:, p * PAGE_SIZE:(p + 1) * PAGE_SIZE],
                                vp_refs[p][pl.ds(g, PAGE_SIZE, stride=nkv), :].astype(BF16))
        acc = acc / l
        outs += [acc[hh * t:(hh + 1) * t, :] for hh in range(grp)]
    y_ref[0] = jnp.concatenate(outs, axis=1).astype(y_ref.dtype)


def dsa_sample(z, kn, mask, pool_k, pool_v, page_table, bs, q_gain, nq, nkv, v_blk):
    db, t, _ = z.shape
    npages = page_table.shape[1]
    wkv = nkv * DSA_HEAD_DIM
    lpad = npages * PAGE_SIZE + LANES
    assert db >= PAGE_SLOTS - 1
    slots = (PAGE_SLOTS, npages, PAGE_SIZE * nkv, DSA_HEAD_DIM)
    grid_spec = pltpu.PrefetchScalarGridSpec(
        num_scalar_prefetch=1,
        grid=(db,),
        in_specs=[pl.BlockSpec((1, t, nq * DSA_HEAD_DIM), lambda i, pt: (i, 0, 0)),
                  pl.BlockSpec((1, t, wkv), lambda i, pt: (i, 0, 0)),
                  pl.BlockSpec((1, t, wkv), lambda i, pt: (i, 0, v_blk)),
                  pl.BlockSpec((1, t, lpad), lambda i, pt: (i, 0, 0)),
                  pl.BlockSpec((nq * t, lpad), lambda i, pt: (0, 0)),
                  pl.BlockSpec((1, LANES), lambda i, pt: (0, 0)),
                  pl.BlockSpec(memory_space=pl.ANY), pl.BlockSpec(memory_space=pl.ANY)],
        out_specs=pl.BlockSpec((1, t, nq * DSA_HEAD_DIM), lambda i, pt: (i, 0, 0)),
        scratch_shapes=[pltpu.VMEM(slots, F32), pltpu.VMEM(slots, F32),
                        pltpu.SemaphoreType.DMA((PAGE_SLOTS, 2))],
    )
    return pl.pallas_call(
        functools.partial(_dsa_sample_kernel, npages=npages, t=t, nq=nq, nkv=nkv),
        grid_spec=grid_spec,
        out_shape=jax.ShapeDtypeStruct((db, t, nq * DSA_HEAD_DIM), BF16),
        compiler_params=_cp(("arbitrary",)),
        name="dsa_sample",
    )(page_table, z, kn, z, mask, bs.reshape(nq * t, lpad), q_gain.reshape(1, LANES), pool_k, pool_v)


def _conv_ffn(x2, seq_len, state, norm_g, weights, conv_w, conv_b):
    emit = len(weights) == 2
    if emit:
        w_val = w_gate = weights[0]
        gate_col = weights[0].shape[1] // 2
    else:
        w_val, w_gate = weights[:2]
        gate_col = 0
    mid, ns_val, ns_gate, *wb = ffn_up(x2, seq_len, state, norm_g, w_val, w_gate, gate_col, conv_w, conv_b, emit)
    y = matmul_residual((mid,), weights[-1], x2, emit)
    if emit:
        y, w_down_b = y
        wb.append(w_down_b)
    step = ns_val.shape[0] * seq_len // x2.shape[0]
    return y, jnp.concatenate([ns_val[step - 1::step], ns_gate[step - 1::step]], axis=-1), tuple(wb)


def kernel(x_prompt, x_sample, state_rglru_h, state_rglru_conv, cache_moba_k, cache_moba_v, state_ffn0_conv, cache_dsa_k, cache_dsa_v, cache_dsa_idx_k, state_ffn1_conv, page_table, rel_bias_table, l0_norm_g, l0_w_in, l0_conv_w, l0_conv_b, l0_gate_a_w, l0_gate_a_b, l0_gate_i_w, l0_gate_i_b, l0_lambda, l0_q_norm_g, l0_k_norm_g, l0_w_out, ffn0_norm_g, ffn0_w_up, ffn0_conv_w, ffn0_conv_b, ffn0_w_down, l1_norm_g, l1_w_in, l1_q_norm_g, l1_k_norm_g, l1_w_out, ffn1_norm_g, ffn1_w_up, ffn1_conv_w, ffn1_conv_b, ffn1_w_down):
    bp, s, d = x_prompt.shape
    db, t, _ = x_sample.shape
    w = state_rglru_h.shape[-1]
    f2 = ffn0_w_up.shape[1]
    n_phys = cache_moba_k.shape[0]
    npages = page_table.shape[1]
    past = npages * PAGE_SIZE
    lpad = past + LANES
    nmh = cache_moba_k.shape[2]
    nkv = cache_dsa_k.shape[2]
    nq = l1_w_out.shape[0] // DSA_HEAD_DIM
    assert cache_moba_k.shape[3] == MOBA_HEAD_DIM and cache_dsa_k.shape[3] == DSA_HEAD_DIM
    assert cache_moba_k.shape[1] == PAGE_SIZE and nmh * MOBA_HEAD_DIM == w
    assert rel_bias_table.shape == (N_BUCKETS, nmh) and nq == nmh
    assert l0_w_in.shape[1] == 5 * w

    xp = x_prompt.reshape(bp * s, d)
    xs = x_sample.reshape(db * t, d)
    bt, bsamp = bias_tiles(rel_bias_table, past, t, lpad)

    wa_bd = _regroup_blockdiag(l0_gate_a_w)
    wi_bd = _regroup_blockdiag(l0_gate_i_w)
    pool_mk = cache_moba_k.transpose(0, 2, 3, 1).reshape(n_phys, w, PAGE_SIZE)
    pool_mv = cache_moba_v.transpose(0, 2, 3, 1).reshape(n_phys, w, PAGE_SIZE)

    def even(x2, b, tt, h0, conv_state, attend, w_in, w_out, emit):
        z = norm_matmul(x2, l0_norm_g, w_in, emit)
        z, w_in_b = z if emit else (z, None)
        z = z.reshape(b, tt, 5 * w)
        ya, h_last = rglru_branch(z, h0, conv_state, l0_conv_w, l0_conv_b, wa_bd, l0_gate_a_b,
                                  wi_bd, l0_gate_i_b, l0_lambda)
        yb, kn, *v_out = attend(z)
        out = matmul_residual((ya.reshape(b * tt, w), yb.reshape(b * tt, w)), w_out, x2, emit)
        out, w_out_b = out if emit else (out, None)
        new_conv = z[:, tt - (l0_conv_w.shape[0] - 1):, :w]
        v_rows = (v_out[0] if v_out else z[:, :, 4 * w:]).reshape(b, tt, nmh, MOBA_HEAD_DIM)
        return (out, h_last, new_conv, kn.reshape(b, tt, nmh, MOBA_HEAD_DIM), v_rows), (w_in_b, w_out_b)

    (xs, h_s, conv_s, mk_s, mv_s), (w_in0_b, w_out0_b) = even(
        xs, db, t, state_rglru_h, state_rglru_conv,
        lambda z: moba_sample(z, pool_mk, pool_mv, page_table, bsamp, l0_q_norm_g, l0_k_norm_g, w),
        l0_w_in, l0_w_out, True)
    (xp, h_p, conv_p, mk_p, mv_p), _ = even(
        xp, bp, s, jnp.zeros((bp, w), F32), jnp.zeros((bp, l0_conv_w.shape[0] - 1, w), F32),
        lambda z: moba_prompt(z, rel_bias_table, bt, l0_q_norm_g, l0_k_norm_g, w),
        w_in0_b, w_out0_b, False)

    xs, f0_s, ffn0_wb = _conv_ffn(xs, t, state_ffn0_conv, ffn0_norm_g, (ffn0_w_up, ffn0_w_down),
                                  ffn0_conv_w, ffn0_conv_b)
    xp, f0_p, _ = _conv_ffn(xp, s, jnp.zeros((bp, ffn0_conv_w.shape[0] - 1, f2), F32), ffn0_norm_g, ffn0_wb,
                            ffn0_conv_w, ffn0_conv_b)

    wq = nq * DSA_HEAD_DIM
    wkv = nkv * DSA_HEAD_DIM
    wqi = IDX_HEADS * IDX_DIM
    n_in = l1_w_in.shape[1]
    assert n_in == wq + 2 * wkv + wqi + IDX_DIM + IDX_HEADS
    n_pad = -(-n_in // 512) * 512
    w_in1 = jnp.pad(l1_w_in, ((0, 0), (0, n_pad - n_in)))
    k_blk0 = wq // LANES
    v_blk0 = (wq + wkv) // LANES
    qi_off = wq + 2 * wkv
    assert qi_off % wqi == 0 and (qi_off + wqi) % LANES == 0 and (wq + wkv) % wkv == 0
    qi_blk = qi_off // wqi
    kw_blk = (qi_off + wqi) // LANES
    pool_dk = cache_dsa_k.reshape(n_phys * PAGE_SIZE * nkv, DSA_HEAD_DIM)
    pool_dv = cache_dsa_v.reshape(n_phys * PAGE_SIZE * nkv, DSA_HEAD_DIM)
    pool_ik = cache_dsa_idx_k.transpose(0, 2, 1)

    def odd(x2, b, tt, attend, w_in, w_out, emit):
        z2 = norm_matmul(x2, l1_norm_g, w_in, emit)
        z2, w_in_b = z2 if emit else (z2, None)
        kn, knb, vb, vf = kv_prep(z2, l1_k_norm_g, k_blk0, v_blk0, nkv)
        z = z2.reshape(b, tt, n_pad)
        y = attend(z, kn.reshape(b, tt, wkv), knb.reshape(b, tt, wkv), vb.reshape(b, tt, wkv))
        out = matmul_residual((y.reshape(b * tt, wq),), w_out, x2, emit)
        out, w_out_b = out if emit else (out, None)
        v_rows = vf.reshape(b, tt, nkv, DSA_HEAD_DIM)
        ki_rows = z[:, :, qi_off + wqi:qi_off + wqi + IDX_DIM]
        return (out, kn.reshape(b, tt, nkv, DSA_HEAD_DIM), v_rows, ki_rows), (w_in_b, w_out_b)

    def attend_prompt(z, kn, knb, vb):
        isc = index_prompt(z, qi_blk, kw_blk)
        mask = topk_mask(isc.reshape(bp * s, s), min(DSA_TOPK, s // 4), s, 0, True).reshape(bp, s, s)
        return dsa_prompt(z, knb, vb, mask, rel_bias_table, bt, l1_q_norm_g, nq, nkv)

    def attend_sample(z, kn, knb, vb):
        isc = index_sample(z, pool_ik, page_table, qi_blk, kw_blk)
        mask = topk_mask(isc.reshape(db * t, lpad), min(DSA_TOPK, (past + t) // 4), t, past, False)
        return dsa_sample(z, kn, mask.reshape(db, t, lpad), pool_dk, pool_dv, page_table, bsamp,
                          l1_q_norm_g, nq, nkv, (wq + wkv) // wkv)

    (xs, dk_s, dv_s, di_s), (w_in1_b, w_out1_b) = odd(xs, db, t, attend_sample, w_in1, l1_w_out, True)
    (xp, dk_p, dv_p, di_p), _ = odd(xp, bp, s, attend_prompt, w_in1_b, w_out1_b, False)

    xs, f1_s, ffn1_wb = _conv_ffn(xs, t, state_ffn1_conv, ffn1_norm_g, (ffn1_w_up, ffn1_w_down),
                                  ffn1_conv_w, ffn1_conv_b)
    xp, f1_p, _ = _conv_ffn(xp, s, jnp.zeros((bp, ffn1_conv_w.shape[0] - 1, f2), F32), ffn1_norm_g, ffn1_wb,
                            ffn1_conv_w, ffn1_conv_b)

    return (xp.reshape(bp, s, d), xs.reshape(db, t, d), h_p, h_s, conv_p, conv_s, mk_p, mk_s, mv_p, mv_s,
            f0_p, f0_s, dk_p, dk_s, dv_p, dv_s, di_p, di_s, f1_p, f1_s)
```
